```python
import math
import jax, jax.numpy as jnp
from jax import lax
import numpy as np

D_MODEL = 1024
BATCH = 8
SEQ = 2048
DEPTH = 1
DEC_BATCH = 128
DEC_SEQ = 8
PAST_LEN = 16384
PAGE_SIZE = 128

N_RET_HEADS = 8
RET_DK = D_MODEL // N_RET_HEADS
RET_DV = 2 * RET_DK
D_RET_QK = N_RET_HEADS * RET_DK
D_RET_V = N_RET_HEADS * RET_DV
D_CONV = D_MODEL
CONV_W = 3
D_FF = 4 * D_MODEL
D_PLE = 256
RET_CHUNK = 128
ROPE_BASE = 10000.0
EPS = 1e-6
SPLIT_SIZES = (D_RET_QK, D_RET_QK, D_RET_V, D_RET_V, D_CONV, D_CONV, D_CONV, D_MODEL, D_MODEL)
D_IN_TOTAL = sum(SPLIT_SIZES)
SPLIT_OFFSETS = tuple(int(v) for v in np.cumsum(SPLIT_SIZES)[:-1])

kernel_name = "retention_shortconv_gated_hybrid_step"


def rmsnorm(x, g):
    xf = x.astype(jnp.float32)
    y = xf * lax.rsqrt(jnp.mean(xf * xf, axis=-1, keepdims=True) + EPS)
    return (y * g.astype(jnp.float32)).astype(x.dtype)


def rope(x, pos):
    d = x.shape[-1]
    inv_freq = ROPE_BASE ** (-jnp.arange(0, d, 2, dtype=jnp.float32) / d)
    ang = pos.astype(jnp.float32)[:, None] * inv_freq[None, :]
    c = jnp.cos(ang)[None, :, None, :]
    s = jnp.sin(ang)[None, :, None, :]
    x1, x2 = x[..., : d // 2], x[..., d // 2:]
    return jnp.concatenate([x1 * c - x2 * s, x2 * c + x1 * s], axis=-1)


def retention(q, k, v, s0, chunk):
    B, L, H, dk = q.shape
    dv = v.shape[-1]
    n = L // chunk
    def to_chunks(t):
        return t.reshape(B, n, chunk, H, t.shape[-1]).transpose(1, 0, 3, 2, 4)
    qc, kc, vc = to_chunks(q), to_chunks(k), to_chunks(v)
    log_g = jnp.log(1.0 - 2.0 ** (-5.0 - jnp.arange(H, dtype=jnp.float32)))
    idx = jnp.arange(chunk, dtype=jnp.float32)
    diff = idx[:, None] - idx[None, :]
    decay = jnp.where(diff >= 0, jnp.exp(jnp.maximum(diff, 0.0)[None] * log_g[:, None, None]), 0.0)
    q_decay = jnp.exp((idx + 1.0)[None, :] * log_g[:, None])[None, :, :, None]
    k_decay = jnp.exp((chunk - 1.0 - idx)[None, :] * log_g[:, None])[None, :, :, None]
    chunk_decay = jnp.exp(chunk * log_g)[None, :, None, None]

    def step(S, xs):
        qi, ki, vi = xs
        scores = jnp.einsum('bhid,bhjd->bhij', qi, ki) * decay[None]
        inner = jnp.einsum('bhij,bhjv->bhiv', scores, vi)
        cross = jnp.einsum('bhid,bhdv->bhiv', qi, S) * q_decay
        S_new = S * chunk_decay + jnp.einsum('bhjd,bhjv->bhdv', ki * k_decay, vi)
        return S_new, inner + cross

    s_fin, out = lax.scan(step, s0, (qc, kc, vc))
    out = out.transpose(1, 0, 3, 2, 4).reshape(B, L, H, dv)
    return out, s_fin


def causal_conv(u, state, w):
    L = u.shape[1]
    ext = jnp.concatenate([state.astype(u.dtype), u], axis=1)
    out = sum(w[j] * ext[:, j:j + L] for j in range(CONV_W))
    return out, ext[:, -(CONV_W - 1):]


def layer(x, p, s_ret, s_conv, pos, chunk, g_mix, w_in, w_ret_out, conv_w, w_conv_out, w_o,
          g_ffn, w_up, w_down, g_ple, w_ple_gate, w_ple_proj):
    B, L, _ = x.shape
    h = rmsnorm(x, g_mix)
    proj = h @ w_in
    q, k, v, gr, cx, cb, cc, ga, gb = jnp.split(proj, SPLIT_OFFSETS, axis=-1)
    q = rope(q.reshape(B, L, N_RET_HEADS, RET_DK).astype(jnp.float32), pos) * (RET_DK ** -0.5)
    k = rope(k.reshape(B, L, N_RET_HEADS, RET_DK).astype(jnp.float32), pos)
    v = v.reshape(B, L, N_RET_HEADS, RET_DV).astype(jnp.float32)
    o, s_ret_new = retention(q, k, v, s_ret.astype(jnp.float32), chunk)
    o = o * lax.rsqrt(jnp.mean(o * o, axis=-1, keepdims=True) + EPS)
    o = o.reshape(B, L, D_RET_V).astype(x.dtype) * jax.nn.silu(gr)
    y_ret = o @ w_ret_out
    u = cc * cx
    cv, s_conv_new = causal_conv(u, s_conv, conv_w)
    y_conv = (cb * cv) @ w_conv_out
    x = x + (jax.nn.sigmoid(ga) * y_ret + jax.nn.sigmoid(gb) * y_conv) @ w_o
    hf = rmsnorm(x, g_ffn)
    x = x + jnp.square(jax.nn.relu(hf @ w_up)) @ w_down
    gate = jax.nn.sigmoid(rmsnorm(x, g_ple) @ w_ple_gate)
    x = x + gate * (p @ w_ple_proj)
    return x, s_ret_new.astype(s_ret.dtype), s_conv_new.astype(s_conv.dtype)


def setup_inputs(seed: int = 0) -> dict:
    key = jax.random.key(seed)
    ks = jax.random.split(key, 24)
    f32 = jnp.float32
    nrm = lambda k, shape, scale: jax.random.normal(k, shape, f32) * scale
    return {
        "x_prompt": nrm(ks[0], (BATCH, SEQ, D_MODEL), 1.0),
        "x_sample": nrm(ks[1], (DEC_BATCH, DEC_SEQ, D_MODEL), 1.0),
        "state_ret": nrm(ks[2], (DEPTH, DEC_BATCH, N_RET_HEADS, RET_DK, RET_DV), 1.0),
        "state_conv": nrm(ks[3], (DEPTH, DEC_BATCH, CONV_W - 1, D_CONV), 1.0),
        "p_prompt": nrm(ks[4], (DEPTH, BATCH, SEQ, D_PLE), 1.0),
        "p_sample": nrm(ks[5], (DEPTH, DEC_BATCH, DEC_SEQ, D_PLE), 1.0),
        "g_mix": 1.0 + nrm(ks[6], (DEPTH, D_MODEL), 0.01),
        "w_in": nrm(ks[7], (DEPTH, D_MODEL, D_IN_TOTAL), D_MODEL ** -0.5),
        "w_ret_out": nrm(ks[8], (DEPTH, D_RET_V, D_MODEL), D_RET_V ** -0.5),
        "conv_w": nrm(ks[9], (DEPTH, CONV_W, D_CONV), CONV_W ** -0.5),
        "w_conv_out": nrm(ks[10], (DEPTH, D_CONV, D_MODEL), D_CONV ** -0.5),
        "w_o": nrm(ks[11], (DEPTH, D_MODEL, D_MODEL), D_MODEL ** -0.5),
        "g_ffn": 1.0 + nrm(ks[12], (DEPTH, D_MODEL), 0.01),
        "w_up": nrm(ks[13], (DEPTH, D_MODEL, D_FF), D_MODEL ** -0.5),
        "w_down": nrm(ks[14], (DEPTH, D_FF, D_MODEL), D_FF ** -0.5),
        "g_ple": 1.0 + nrm(ks[15], (DEPTH, D_MODEL), 0.01),
        "w_ple_gate": nrm(ks[16], (DEPTH, D_MODEL, D_MODEL), D_MODEL ** -0.5),
        "w_ple_proj": nrm(ks[17], (DEPTH, D_PLE, D_MODEL), D_PLE ** -0.5),
        "g_final": 1.0 + nrm(ks[18], (D_MODEL,), 0.01),
    }


def reference(x_prompt, x_sample, state_ret, state_conv, p_prompt, p_sample, g_mix, w_in, w_ret_out,
              conv_w, w_conv_out, w_o, g_ffn, w_up, w_down, g_ple, w_ple_gate, w_ple_proj, g_final):
    pos_prompt = jnp.arange(SEQ, dtype=jnp.float32)
    pos_sample = PAST_LEN + jnp.arange(DEC_SEQ, dtype=jnp.float32)
    chunk_prompt = min(RET_CHUNK, SEQ)
    hp, hs = x_prompt, x_sample
    rp_list, cp_list, rs_list, cs_list = [], [], [], []
    for i in range(DEPTH):
        w = (g_mix[i], w_in[i], w_ret_out[i], conv_w[i], w_conv_out[i], w_o[i],
             g_ffn[i], w_up[i], w_down[i], g_ple[i], w_ple_gate[i], w_ple_proj[i])
        s_ret0 = jnp.zeros((BATCH, N_RET_HEADS, RET_DK, RET_DV), state_ret.dtype)
        s_conv0 = jnp.zeros((BATCH, CONV_W - 1, D_CONV), state_conv.dtype)
        hp, rp, cp = layer(hp, p_prompt[i], s_ret0, s_conv0, pos_prompt, chunk_prompt, *w)
        hs, rs, cs = layer(hs, p_sample[i], state_ret[i], state_conv[i], pos_sample, DEC_SEQ, *w)
        rp_list.append(rp); cp_list.append(cp); rs_list.append(rs); cs_list.append(cs)
    y_prompt = rmsnorm(hp, g_final)
    y_sample = rmsnorm(hs, g_final)
    ret_state_prompt = jnp.stack(rp_list)
    conv_state_prompt = jnp.stack(cp_list)
    ret_state_sample = jnp.stack(rs_list)
    conv_state_sample = jnp.stack(cs_list)
    return (y_prompt, y_sample, ret_state_prompt, conv_state_prompt, ret_state_sample, conv_state_sample)
```

```python
import functools

import jax
import jax.numpy as jnp
import numpy as np
from jax import lax
from jax.experimental import pallas as pl
from jax.experimental.pallas import tpu as pltpu

F32 = jnp.float32
BF16 = jnp.bfloat16

N_HEADS = 8
DK = 128
DV = 256
CONV_W = 3
RET_CHUNK = 128
PAST_LEN = 16384
ROPE_BASE = 10000.0
EPS = 1e-6

V7X_SUBLANES = 8
V7X_VMEM_LIMIT_BYTES = 56 * 1024 * 1024

COL_TILE = 1024


def _rms(x, g):
    return x * lax.rsqrt(jnp.mean(x * x, axis=-1, keepdims=True) + EPS) * g


def _dot(a, b):
    return jnp.dot(a, b, preferred_element_type=F32)


def _params(semantics):
    return pltpu.CompilerParams(dimension_semantics=semantics,
                                vmem_limit_bytes=V7X_VMEM_LIMIT_BYTES)


def _resident(shape):
    zeros = (0,) * len(shape)
    return pl.BlockSpec(shape, lambda *_: zeros, pipeline_mode=pl.Buffered(1))


def _inproj_kernel(x_ref, g_ref, w_ref, o_ref, h_ref):
    @pl.when(pl.program_id(1) == 0)
    def _():
        h_ref[...] = _rms(x_ref[...], g_ref[...]).astype(BF16)

    o_ref[...] = _dot(h_ref[...], w_ref[...])


def _inproj(x2d, g, w, tm):
    T, D = x2d.shape
    n_out = w.shape[1]
    return pl.pallas_call(
        _inproj_kernel,
        grid=(T // tm, n_out // COL_TILE),
        in_specs=[pl.BlockSpec((tm, D), lambda i, j: (i, 0)),
                  pl.BlockSpec((1, D), lambda i, j: (0, 0)),
                  pl.BlockSpec((D, COL_TILE), lambda i, j: (0, j))],
        out_specs=pl.BlockSpec((tm, COL_TILE), lambda i, j: (i, j)),
        out_shape=jax.ShapeDtypeStruct((T, n_out), F32),
        scratch_shapes=[pltpu.VMEM((tm, D), BF16)],
        compiler_params=_params(("parallel", "arbitrary")),
        name="inproj",
    )(x2d, g, w)


def _retention_tables(pos, chunk):
    inv_freq = ROPE_BASE ** (-jnp.arange(0, DK, 2, dtype=F32) / DK)
    ang = pos.astype(F32)[:, None] * inv_freq[None, :]
    c, s = jnp.cos(ang), jnp.sin(ang)
    cos2 = jnp.concatenate([c, c], axis=-1)
    sin2 = jnp.concatenate([-s, s], axis=-1)
    log_g = jnp.log(1.0 - 2.0 ** (-5.0 - jnp.arange(N_HEADS, dtype=F32)))
    idx = jnp.arange(chunk, dtype=F32)
    diff = idx[:, None] - idx[None, :]
    decay = jnp.where(diff >= 0, jnp.exp(jnp.maximum(diff, 0.0)[None] * log_g[:, None, None]), 0.0)
    q_decay = jnp.exp((idx + 1.0)[None, :] * log_g[:, None])
    k_decay = jnp.exp((chunk - 1.0 - idx)[None, :] * log_g[:, None])
    chunk_decay = jnp.exp(chunk * log_g)
    qd = jnp.broadcast_to(q_decay[:, :, None], (N_HEADS, chunk, DV))
    kd = jnp.broadcast_to(k_decay[:, :, None], (N_HEADS, chunk, DK))
    return cos2, sin2, decay, qd, kd, chunk_decay


def _retention_kernel(*refs, chunk, n_chunks, n_seq, has_init):
    if has_init:
        (q_ref, k_ref, v_ref, gr_ref, cos_ref, sin_ref, dec_ref, qd_ref, kd_ref, cd_ref,
         s0_ref, o_ref, s_ref) = refs
    else:
        (q_ref, k_ref, v_ref, gr_ref, cos_ref, sin_ref, dec_ref, qd_ref, kd_ref, cd_ref,
         o_ref, s_ref) = refs

        @pl.when(pl.program_id(1) == 0)
        def _():
            s_ref[...] = jnp.zeros_like(s_ref)

    scale = DK ** -0.5

    def rope(t, cos, sin):
        return t * cos + pltpu.roll(t, DK // 2, 1) * sin

    for seq in range(n_seq):
        for c in range(n_chunks):
            rows = pl.ds((seq * n_chunks + c) * chunk, chunk)
            trows = pl.ds(c * chunk, chunk)
            cos, sin = cos_ref[trows, :], sin_ref[trows, :]
            for h in range(N_HEADS):
                kcols = pl.ds(h * DK, DK)
                vcols = pl.ds(h * DV, DV)
                q = (rope(q_ref[rows, kcols], cos, sin) * scale).astype(BF16)
                k = rope(k_ref[rows, kcols], cos, sin)
                v = v_ref[rows, vcols].astype(BF16)
                if has_init and c == 0:
                    S = s0_ref[seq, h]
                else:
                    S = s_ref[seq, h]
                scores = lax.dot_general(q, k.astype(BF16), (((1,), (1,)), ((), ())),
                                         preferred_element_type=F32) * dec_ref[h]
                inner = _dot(scores.astype(BF16), v)
                cross = _dot(q, S.astype(BF16)) * qd_ref[h]
                kdt = jnp.transpose(k * kd_ref[h]).astype(BF16)
                s_ref[seq, h] = S * cd_ref[h] + _dot(kdt, v)
                o = inner + cross
                o = o * lax.rsqrt(jnp.mean(o * o, axis=-1, keepdims=True) + EPS)
                o_ref[rows, vcols] = (o * jax.nn.silu(gr_ref[rows, vcols])).astype(BF16)


def _retention(proj, s0, pos, *, n_batch, seq_len, chunk, n_chunks, n_seq):
    T = proj.shape[0]
    rows = n_seq * n_chunks * chunk
    steps = seq_len // (n_chunks * chunk) if n_seq == 1 else 1
    has_init = s0 is not None
    cos2, sin2, decay, qd, kd, cd = _retention_tables(pos, chunk)
    grid = (n_batch // n_seq, steps)

    def row_blk(b, t):
        return b * steps + t

    in_specs = [pl.BlockSpec((rows, N_HEADS * DK), lambda b, t: (row_blk(b, t), 0)),
                pl.BlockSpec((rows, N_HEADS * DK), lambda b, t: (row_blk(b, t), 1)),
                pl.BlockSpec((rows, N_HEADS * DV), lambda b, t: (row_blk(b, t), 1)),
                pl.BlockSpec((rows, N_HEADS * DV), lambda b, t: (row_blk(b, t), 2)),
                pl.BlockSpec((n_chunks * chunk, DK), lambda b, t: (t, 0)),
                pl.BlockSpec((n_chunks * chunk, DK), lambda b, t: (t, 0)),
                _resident(decay.shape), _resident(qd.shape), _resident(kd.shape),
                pl.BlockSpec(memory_space=pltpu.SMEM)]
    args = [proj, proj, proj, proj, cos2, sin2, decay, qd, kd, cd]
    state_spec = pl.BlockSpec((n_seq, N_HEADS, DK, DV), lambda b, t: (b, 0, 0, 0))
    if has_init:
        in_specs.append(state_spec)
        args.append(s0)
    kernel = functools.partial(_retention_kernel, chunk=chunk, n_chunks=n_chunks, n_seq=n_seq,
                               has_init=has_init)
    return pl.pallas_call(
        kernel,
        grid=grid,
        in_specs=in_specs,
        out_specs=[pl.BlockSpec((rows, N_HEADS * DV), lambda b, t: (row_blk(b, t), 0)), state_spec],
        out_shape=[jax.ShapeDtypeStruct((T, N_HEADS * DV), BF16),
                   jax.ShapeDtypeStruct((n_batch, N_HEADS, DK, DV), F32)],
        compiler_params=_params(("parallel", "arbitrary")),
        name="retention_init" if has_init else "retention",
    )(*args)


def _mix_kernel(*refs, tm, seq_len, has_init):
    (x_ref, og_ref, cx_ref, cb_ref, cc_ref, ga_ref, gb_ref, a_ref, b_ref,
     cw_ref, wro_ref, wco_ref, wo_ref, x1_ref, cs_ref) = refs
    u = cc_ref[...] * cx_ref[...]
    row = lax.broadcasted_iota(jnp.int32, u.shape, 0)
    if has_init:
        n = tm // seq_len
        prev2 = jnp.broadcast_to(a_ref[:, 0:1, :], (n, seq_len, u.shape[1])).reshape(u.shape)
        prev1 = jnp.broadcast_to(a_ref[:, 1:2, :], (n, seq_len, u.shape[1])).reshape(u.shape)
        pos = row % seq_len
    else:
        halo = a_ref[...] * b_ref[...]
        halo = jnp.where(pl.program_id(1) == 0, 0.0, halo)
        prev2 = jnp.broadcast_to(halo[V7X_SUBLANES - 2:V7X_SUBLANES - 1, :], u.shape)
        prev1 = jnp.broadcast_to(halo[V7X_SUBLANES - 1:V7X_SUBLANES, :], u.shape)
        pos = row
    u1 = jnp.where(pos == 0, prev1, pltpu.roll(u, 1, 0))
    u2 = jnp.where(pos == 0, prev2, jnp.where(pos == 1, prev1, pltpu.roll(u, 2, 0)))
    cv = cw_ref[0:1, :] * u2 + cw_ref[1:2, :] * u1 + cw_ref[2:3, :] * u
    y_conv = _dot((cb_ref[...] * cv).astype(BF16), wco_ref[...])
    y_ret = _dot(og_ref[...], wro_ref[...])
    merged = jax.nn.sigmoid(ga_ref[...]) * y_ret + jax.nn.sigmoid(gb_ref[...]) * y_conv
    x1_ref[...] = x_ref[...] + _dot(merged.astype(BF16), wo_ref[...])
    if has_init:
        cs_ref[...] = u.reshape(tm // seq_len, seq_len, u.shape[1])[:, seq_len - (CONV_W - 1):, :]
    else:
        cs_ref[0] = u[tm - (CONV_W - 1):, :]


def _mix(x2d, og, proj, conv_state, cw, wro, wco, wo, *, n_batch, seq_len, tm):
    T, D = x2d.shape
    has_init = conv_state is not None
    if has_init:
        n = tm // seq_len
        grid = (T // tm, 1)
        halo_specs = [pl.BlockSpec((n, CONV_W - 1, D), lambda i, t: (i, 0, 0))]
        halo_args = [conv_state]
        cs_spec = pl.BlockSpec((n, CONV_W - 1, D), lambda i, t: (i, 0, 0))
        steps = 1
    else:
        steps = seq_len // tm
        grid = (n_batch, steps)
        sub = tm // V7X_SUBLANES

        def halo_blk(b, t):
            return jnp.maximum((b * steps + t) * sub - 1, 0)

        halo_specs = [pl.BlockSpec((V7X_SUBLANES, D), lambda b, t: (halo_blk(b, t), 6)),
                      pl.BlockSpec((V7X_SUBLANES, D), lambda b, t: (halo_blk(b, t), 8))]
        halo_args = [proj, proj]
        cs_spec = pl.BlockSpec((1, CONV_W - 1, D), lambda b, t: (b, 0, 0))

    def tile(col):
        return pl.BlockSpec((tm, D), lambda b, t: (b * steps + t, col))

    in_specs = ([tile(0), pl.BlockSpec((tm, N_HEADS * DV), lambda b, t: (b * steps + t, 0)),
                 tile(6), tile(7), tile(8), tile(9), tile(10)] + halo_specs)
    args = [x2d, og, proj, proj, proj, proj, proj] + halo_args
    if has_init:
        in_specs.append(pl.BlockSpec((1, 1), lambda b, t: (0, 0)))
        args.append(jnp.zeros((1, 1), F32))
    in_specs += [_resident(cw.shape), _resident(wro.shape), _resident(wco.shape), _resident(wo.shape)]
    args += [cw, wro, wco, wo]
    kernel = functools.partial(_mix_kernel, tm=tm, seq_len=seq_len, has_init=has_init)
    return pl.pallas_call(
        kernel,
        grid=grid,
        in_specs=in_specs,
        out_specs=[tile(0), cs_spec],
        out_shape=[jax.ShapeDtypeStruct((T, D), F32),
                   jax.ShapeDtypeStruct((n_batch, CONV_W - 1, D), F32)],
        compiler_params=_params(("parallel", "arbitrary")),
        name="mix_init" if has_init else "mix",
    )(*args)


def _ffn_kernel(x_ref, p_ref, gf_ref, wu_ref, wd_ref, gp_ref, wpg_ref, wpp_ref, gl_ref, o_ref,
                *, final_norm):
    x = x_ref[...]
    hf = _rms(x, gf_ref[...]).astype(BF16)
    d_ff = wu_ref.shape[1]
    acc = x
    for c in range(d_ff // COL_TILE):
        cols = pl.ds(c * COL_TILE, COL_TILE)
        hid = jnp.square(jnp.maximum(_dot(hf, wu_ref[:, cols]), 0.0)).astype(BF16)
        acc = acc + _dot(hid, wd_ref[cols, :])
    gate = jax.nn.sigmoid(_dot(_rms(acc, gp_ref[...]).astype(BF16), wpg_ref[...]))
    y = acc + gate * _dot(p_ref[...].astype(BF16), wpp_ref[...])
    if final_norm:
        y = _rms(y, gl_ref[...])
    o_ref[...] = y


def _ffn(x2d, p2d, gf, wu, wd, gp, wpg, wpp, gl, *, tm, final_norm):
    T, D = x2d.shape
    kernel = functools.partial(_ffn_kernel, final_norm=final_norm)
    return pl.pallas_call(
        kernel,
        grid=(T // tm,),
        in_specs=[pl.BlockSpec((tm, D), lambda i: (i, 0)),
                  pl.BlockSpec((tm, p2d.shape[1]), lambda i: (i, 0)),
                  _resident(gf.shape), _resident(wu.shape), _resident(wd.shape),
                  _resident(gp.shape), _resident(wpg.shape), _resident(wpp.shape),
                  _resident(gl.shape)],
        out_specs=pl.BlockSpec((tm, D), lambda i: (i, 0)),
        out_shape=jax.ShapeDtypeStruct((T, D), F32),
        compiler_params=_params(("parallel",)),
        name="ffn",
    )(x2d, p2d, gf, wu, wd, gp, wpg, wpp, gl)


def _layer(x2d, p2d, s_ret, s_conv, pos, w, *, n_batch, seq_len, chunk, n_chunks, n_seq,
           tm_in, tm_mix, tm_ffn, final_norm):
    proj = _inproj(x2d, w["g_mix"], w["w_in"], tm_in)
    og, s_ret_new = _retention(proj, s_ret, pos, n_batch=n_batch, seq_len=seq_len, chunk=chunk,
                               n_chunks=n_chunks, n_seq=n_seq)
    x1, s_conv_new = _mix(x2d, og, proj, s_conv, w["conv_w"], w["w_ret_out"], w["w_conv_out"],
                          w["w_o"], n_batch=n_batch, seq_len=seq_len, tm=tm_mix)
    y = _ffn(x1, p2d, w["g_ffn"], w["w_up"], w["w_down"], w["g_ple"], w["w_ple_gate"],
             w["w_ple_proj"], w["g_final"], tm=tm_ffn, final_norm=final_norm)
    return y, s_ret_new, s_conv_new


def kernel(x_prompt, x_sample, state_ret, state_conv, p_prompt, p_sample, g_mix, w_in, w_ret_out,
           conv_w, w_conv_out, w_o, g_ffn, w_up, w_down, g_ple, w_ple_gate, w_ple_proj, g_final):
    depth = w_in.shape[0]
    B, L, D = x_prompt.shape
    Bs, Ls, _ = x_sample.shape
    pos_prompt = jnp.arange(L, dtype=F32)
    pos_sample = PAST_LEN + jnp.arange(Ls, dtype=F32)
    chunk_prompt = min(RET_CHUNK, L)

    hp = x_prompt.reshape(B * L, D)
    hs = x_sample.reshape(Bs * Ls, D)
    rp, cp, rs, cs = [], [], [], []
    for i in range(depth):
        w = {"g_mix": g_mix[i][None], "w_in": w_in[i].astype(BF16),
             "w_ret_out": w_ret_out[i].astype(BF16), "conv_w": conv_w[i],
             "w_conv_out": w_conv_out[i].astype(BF16), "w_o": w_o[i].astype(BF16),
             "g_ffn": g_ffn[i][None], "w_up": w_up[i].astype(BF16), "w_down": w_down[i].astype(BF16),
             "g_ple": g_ple[i][None], "w_ple_gate": w_ple_gate[i].astype(BF16),
             "w_ple_proj": w_ple_proj[i].astype(BF16), "g_final": g_final[None]}
        last = i == depth - 1
        hp, r, c = _layer(hp, p_prompt[i].reshape(B * L, -1), None, None, pos_prompt, w,
                          n_batch=B, seq_len=L, chunk=chunk_prompt, n_chunks=2, n_seq=1,
                          tm_in=1024, tm_mix=512, tm_ffn=512, final_norm=last)
        rp.append(r)
        cp.append(c)
        hs, r, c = _layer(hs, p_sample[i].reshape(Bs * Ls, -1), state_ret[i], state_conv[i],
                          pos_sample, w, n_batch=Bs, seq_len=Ls, chunk=Ls, n_chunks=1, n_seq=4,
                          tm_in=1024, tm_mix=512, tm_ffn=512, final_norm=last)
        rs.append(r)
        cs.append(c)
    return (hp.reshape(B, L, D), hs.reshape(Bs, Ls, D), jnp.stack(rp), jnp.stack(cp),
            jnp.stack(rs), jnp.stack(cs))
```

```python
import functools

import jax
import jax.numpy as jnp
from jax import lax
from jax.experimental import pallas as pl
from jax.experimental.pallas import tpu as pltpu

F32 = jnp.float32
BF16 = jnp.bfloat16

N_HEADS = 8
DK = 128
DV = 256
CONV_W = 3
RET_CHUNK = 128
PAST_LEN = 16384
ROPE_BASE = 10000.0
EPS = 1e-6

V7X_SUBLANES = 8
V7X_VMEM_LIMIT_BYTES = 56 * 1024 * 1024

COL_TILE = 1024

J_Q, J_K, J_V, J_GR, J_CX, J_CB, J_CC, J_GA = 0, 1, 2, 4, 6, 7, 8, 9


def _rms(x, g):
    return x * lax.rsqrt(jnp.mean(x * x, axis=-1, keepdims=True) + EPS) * g


def _dot(a, b):
    return jnp.dot(a, b, preferred_element_type=F32)


def _params(semantics):
    return pltpu.CompilerParams(dimension_semantics=semantics,
                                vmem_limit_bytes=V7X_VMEM_LIMIT_BYTES)


def _resident(shape):
    zeros = (0,) * len(shape)
    return pl.BlockSpec(shape, lambda *_: zeros, pipeline_mode=pl.Buffered(1))


def _tables(pos, chunk):
    inv_freq = ROPE_BASE ** (-jnp.arange(0, DK, 2, dtype=F32) / DK)
    ang = pos.astype(F32)[:, None] * inv_freq[None, :]
    c, s = jnp.cos(ang), jnp.sin(ang)
    cos2 = jnp.concatenate([c, c], axis=-1)
    sin2 = jnp.concatenate([-s, s], axis=-1)
    log_g = jnp.log(1.0 - 2.0 ** (-5.0 - jnp.arange(N_HEADS, dtype=F32)))
    idx = jnp.arange(chunk, dtype=F32)
    diff = idx[:, None] - idx[None, :]
    decay = jnp.where(diff >= 0, jnp.exp(jnp.maximum(diff, 0.0)[None] * log_g[:, None, None]), 0.0)
    q_decay = jnp.exp((idx + 1.0)[None, :] * log_g[:, None])
    k_decay = jnp.exp((chunk - 1.0 - idx)[None, :] * log_g[:, None])
    chunk_decay = jnp.exp(chunk * log_g)
    qd = jnp.repeat(q_decay.T, DK, axis=1)
    kd = jnp.repeat(k_decay.T, DK, axis=1)
    return cos2, sin2, decay, qd, kd, chunk_decay


def _inproj_kernel(*refs, tm, chunk, seq_len, has_init):
    if has_init:
        (x_ref, g_ref, w_ref, cos_ref, sin_ref, qd_ref, kd_ref, cw_ref, st_ref,
         a_ref, b_ref, f_ref, cs_ref, h_ref, cx_ref, cb_ref, carry_ref) = refs
    else:
        (x_ref, g_ref, w_ref, cos_ref, sin_ref, qd_ref, kd_ref, cw_ref,
         a_ref, b_ref, f_ref, cs_ref, h_ref, cx_ref, cb_ref, carry_ref) = refs
    i = pl.program_id(0)
    j = pl.program_id(1)
    n_pos = cos_ref.shape[0]

    @pl.when(j == 0)
    def _():
        h_ref[...] = _rms(x_ref[...], g_ref[...]).astype(BF16)

    def proj():
        return _dot(h_ref[...], w_ref[...])

    def rotary_heads(acc, dec_ref, post_scale):
        plain, scaled = [], []
        for h in range(N_HEADS):
            cols = slice(h * DK, (h + 1) * DK)
            t = acc[:, cols]
            t3 = t.reshape(tm // n_pos, n_pos, DK)
            r3 = pltpu.roll(t, DK // 2, 1).reshape(tm // n_pos, n_pos, DK)
            r = (t3 * cos_ref[...][None] + r3 * sin_ref[...][None]).reshape(tm, DK)
            if post_scale != 1.0:
                r = r * post_scale
            plain.append(r.astype(BF16))
            d = r.reshape(tm // chunk, chunk, DK) * dec_ref[:, cols][None]
            scaled.append(d.reshape(tm, DK).astype(BF16))
        return jnp.concatenate(plain + scaled, axis=1)

    @pl.when(j == J_Q)
    def _():
        a_ref[...] = rotary_heads(proj(), qd_ref, DK ** -0.5)

    @pl.when(j == J_K)
    def _():
        a_ref[...] = rotary_heads(proj(), kd_ref, 1.0)

    @pl.when((j == J_V) | (j == J_V + 1))
    def _():
        b_ref[...] = proj().astype(BF16)

    @pl.when((j == J_GR) | (j == J_GR + 1))
    def _():
        f_ref[...] = jax.nn.silu(proj())

    @pl.when(j == J_CX)
    def _():
        cx_ref[...] = proj()

    @pl.when(j == J_CB)
    def _():
        cb_ref[...] = proj()

    @pl.when(j == J_CC)
    def _():
        u = proj() * cx_ref[...]
        row = lax.broadcasted_iota(jnp.int32, u.shape, 0)
        if has_init:
            n = tm // seq_len
            prev2 = jnp.broadcast_to(st_ref[:, 0:1, :], (n, seq_len, u.shape[1])).reshape(u.shape)
            prev1 = jnp.broadcast_to(st_ref[:, 1:2, :], (n, seq_len, u.shape[1])).reshape(u.shape)
            pos = row % seq_len
        else:
            first = (i % (seq_len // tm)) == 0
            halo = jnp.where(first, 0.0, carry_ref[...])
            prev2 = jnp.broadcast_to(halo[V7X_SUBLANES - 2:V7X_SUBLANES - 1, :], u.shape)
            prev1 = jnp.broadcast_to(halo[V7X_SUBLANES - 1:V7X_SUBLANES, :], u.shape)
            pos = row
        u1 = jnp.where(pos == 0, prev1, pltpu.roll(u, 1, 0))
        u2 = jnp.where(pos == 0, prev2, jnp.where(pos == 1, prev1, pltpu.roll(u, 2, 0)))
        cv = cw_ref[0:1, :] * u2 + cw_ref[1:2, :] * u1 + cw_ref[2:3, :] * u
        b_ref[...] = (cb_ref[...] * cv).astype(BF16)
        if has_init:
            cs_ref[...] = u.reshape(tm // seq_len, seq_len, u.shape[1])[:, seq_len - (CONV_W - 1):, :]
        else:
            carry_ref[...] = u[tm - V7X_SUBLANES:, :]
            cs_ref[0] = u[tm - (CONV_W - 1):, :]

    @pl.when(j >= J_GA)
    def _():
        f_ref[...] = jax.nn.sigmoid(proj())


def _inproj(x2d, g, w, cw, conv_state, cos2, sin2, qd, kd, *, n_batch, seq_len, chunk, tm):
    T, D = x2d.shape
    has_init = conv_state is not None
    n_col = w.shape[1] // COL_TILE
    assert n_col == 11 and D == COL_TILE
    if has_init:
        assert tm % seq_len == 0
        n = tm // seq_len
        pos_spec = pl.BlockSpec((seq_len, DK), lambda i, j: (0, 0))
        st_specs = [pl.BlockSpec((n, CONV_W - 1, D), lambda i, j: (i, 0, 0))]
        st_args = [conv_state]
        cs_spec = pl.BlockSpec((n, CONV_W - 1, D), lambda i, j: (i, 0, 0))
    else:
        assert seq_len % tm == 0 and tm % chunk == 0
        tiles = seq_len // tm
        pos_spec = pl.BlockSpec((tm, DK), lambda i, j: (i % tiles, 0))
        st_specs, st_args = [], []
        cs_spec = pl.BlockSpec((1, CONV_W - 1, D), lambda i, j: (i // tiles, 0, 0))

    def b_col(j):
        return jnp.where(j <= J_V, 0, jnp.where(j < J_CC, 1, 2))

    def f_col(j):
        return jnp.where(j <= J_GR, 0, jnp.where(j < J_GA, 1, jnp.where(j == J_GA, 2, 3)))

    kernel = functools.partial(_inproj_kernel, tm=tm, chunk=chunk, seq_len=seq_len, has_init=has_init)
    return pl.pallas_call(
        kernel,
        grid=(T // tm, n_col),
        in_specs=[pl.BlockSpec((tm, D), lambda i, j: (i, 0)),
                  pl.BlockSpec((1, D), lambda i, j: (0, 0)),
                  pl.BlockSpec((D, COL_TILE), lambda i, j: (0, j)),
                  pos_spec, pos_spec,
                  _resident(qd.shape), _resident(kd.shape), _resident(cw.shape)] + st_specs,
        out_specs=[pl.BlockSpec((tm, 2 * COL_TILE), lambda i, j: (i, jnp.minimum(j, 1))),
                   pl.BlockSpec((tm, COL_TILE), lambda i, j: (i, b_col(j))),
                   pl.BlockSpec((tm, COL_TILE), lambda i, j: (i, f_col(j))),
                   cs_spec],
        out_shape=[jax.ShapeDtypeStruct((T, 4 * COL_TILE), BF16),
                   jax.ShapeDtypeStruct((T, 3 * COL_TILE), BF16),
                   jax.ShapeDtypeStruct((T, 4 * COL_TILE), F32),
                   jax.ShapeDtypeStruct((n_batch, CONV_W - 1, D), F32)],
        scratch_shapes=[pltpu.VMEM((tm, D), BF16), pltpu.VMEM((tm, D), F32), pltpu.VMEM((tm, D), F32),
                        pltpu.VMEM((V7X_SUBLANES, D), F32)],
        compiler_params=_params(("arbitrary", "arbitrary")),
        name="inproj_init" if has_init else "inproj",
    )(x2d, g, w, cos2, sin2, qd, kd, cw, *st_args)


def _retention_kernel(*refs, chunk, n_chunks, n_seq, has_init):
    if has_init:
        (q_ref, qd_ref, k_ref, kd_ref, v0_ref, v1_ref, sg_ref, dec_ref, cd_ref,
         s0_ref, o_ref, s_ref) = refs
    else:
        (q_ref, qd_ref, k_ref, kd_ref, v0_ref, v1_ref, sg_ref, dec_ref, cd_ref,
         o_ref, s_ref) = refs

        @pl.when(pl.program_id(1) == 0)
        def _():
            s_ref[...] = jnp.zeros_like(s_ref)

    half = N_HEADS // 2
    fuse = chunk % DK == 0

    def v_at(rows, h):
        v_ref = v0_ref if h < half else v1_ref
        return v_ref[rows, pl.ds((h % half) * DV, DV)]

    for seq in range(n_seq):
        row_slices = [pl.ds((seq * n_chunks + c) * chunk, chunk) for c in range(n_chunks)]
        probs, kvs = {}, {}
        for c, rows in enumerate(row_slices):
            for h in range(N_HEADS):
                kcols = pl.ds(h * DK, DK)
                scores = lax.dot_general(q_ref[rows, kcols], k_ref[rows, kcols], (((1,), (1,)), ((), ())),
                                         preferred_element_type=F32) * dec_ref[h]
                probs[c, h] = scores.astype(BF16)
                kdt = jnp.transpose(kd_ref[rows, kcols].astype(F32)).astype(BF16)
                kvs[c, h] = _dot(kdt, v_at(rows, h))
        if has_init:
            states = [s0_ref[seq, h] for h in range(N_HEADS)]
        else:
            states = [s_ref[seq, h] for h in range(N_HEADS)]
        for c, rows in enumerate(row_slices):
            for h in range(N_HEADS):
                kcols = pl.ds(h * DK, DK)
                ocols = pl.ds(h * DV, DV)
                S = states[h]
                if fuse:
                    o = _dot(jnp.concatenate([probs[c, h], qd_ref[rows, kcols]], axis=1),
                             jnp.concatenate([v_at(rows, h), S.astype(BF16)], axis=0))
                else:
                    o = _dot(probs[c, h], v_at(rows, h)) + _dot(qd_ref[rows, kcols], S.astype(BF16))
                states[h] = S * cd_ref[h] + kvs[c, h]
                o = o * lax.rsqrt(jnp.mean(o * o, axis=-1, keepdims=True) + EPS)
                o_ref[rows, ocols] = (o * sg_ref[rows, ocols]).astype(BF16)
        for h in range(N_HEADS):
            s_ref[seq, h] = states[h]


def _retention(a, b, f, s0, decay, cd, *, n_batch, seq_len, chunk, n_chunks, n_seq):
    T = a.shape[0]
    rows = n_seq * n_chunks * chunk
    steps = seq_len // (n_chunks * chunk) if n_seq == 1 else 1
    has_init = s0 is not None

    def tile(width, col):
        return pl.BlockSpec((rows, width), lambda b_, t: (b_ * steps + t, col))

    in_specs = [tile(COL_TILE, 0), tile(COL_TILE, 1), tile(COL_TILE, 2), tile(COL_TILE, 3),
                tile(COL_TILE, 0), tile(COL_TILE, 1),
                tile(N_HEADS * DV, 0),
                _resident(decay.shape), pl.BlockSpec(memory_space=pltpu.SMEM)]
    args = [a, a, a, a, b, b, f, decay, cd]
    state_spec = pl.BlockSpec((n_seq, N_HEADS, DK, DV), lambda b_, t: (b_, 0, 0, 0))
    if has_init:
        in_specs.append(state_spec)
        args.append(s0)
    kernel = functools.partial(_retention_kernel, chunk=chunk, n_chunks=n_chunks, n_seq=n_seq,
                               has_init=has_init)
    return pl.pallas_call(
        kernel,
        grid=(n_batch // n_seq, steps),
        in_specs=in_specs,
        out_specs=[tile(N_HEADS * DV, 0), state_spec],
        out_shape=[jax.ShapeDtypeStruct((T, N_HEADS * DV), BF16),
                   jax.ShapeDtypeStruct((n_batch, N_HEADS, DK, DV), F32)],
        compiler_params=_params(("parallel", "arbitrary")),
        name="retention_init" if has_init else "retention",
    )(*args)


def _mix_kernel(x_ref, og_ref, cbv_ref, sa_ref, sb_ref, wro_ref, wco_ref, wo_ref, x1_ref):
    y_ret = _dot(og_ref[...], wro_ref[...])
    y_conv = _dot(cbv_ref[...], wco_ref[...])
    merged = sa_ref[...] * y_ret + sb_ref[...] * y_conv
    x1_ref[...] = x_ref[...] + _dot(merged.astype(BF16), wo_ref[...])


def _mix(x2d, og, b, f, wro, wco, wo, *, tm):
    T, D = x2d.shape

    def tile(width, col):
        return pl.BlockSpec((tm, width), lambda i: (i, col))

    return pl.pallas_call(
        _mix_kernel,
        grid=(T // tm,),
        in_specs=[tile(D, 0), tile(N_HEADS * DV, 0), tile(D, 2), tile(D, 2), tile(D, 3),
                  _resident(wro.shape), _resident(wco.shape), _resident(wo.shape)],
        out_specs=tile(D, 0),
        out_shape=jax.ShapeDtypeStruct((T, D), F32),
        compiler_params=_params(("parallel",)),
        name="mix",
    )(x2d, og, b, f, f, wro, wco, wo)


def _ffn_kernel(x_ref, p_ref, gf_ref, wu_ref, wd_ref, gp_ref, wpg_ref, wpp_ref, gl_ref, o_ref,
                *, final_norm):
    x = x_ref[...]
    hf = _rms(x, gf_ref[...]).astype(BF16)
    d_ff = wu_ref.shape[1]
    acc = x
    for c in range(d_ff // COL_TILE):
        cols = pl.ds(c * COL_TILE, COL_TILE)
        hid = jnp.square(jnp.maximum(_dot(hf, wu_ref[:, cols]), 0.0)).astype(BF16)
        acc = acc + _dot(hid, wd_ref[cols, :])
    gate = jax.nn.sigmoid(_dot(_rms(acc, gp_ref[...]).astype(BF16), wpg_ref[...]))
    y = acc + gate * _dot(p_ref[...].astype(BF16), wpp_ref[...])
    if final_norm:
        y = _rms(y, gl_ref[...])
    o_ref[...] = y


def _ffn(x2d, p2d, gf, wu, wd, gp, wpg, wpp, gl, *, tm, final_norm):
    T, D = x2d.shape
    kernel = functools.partial(_ffn_kernel, final_norm=final_norm)
    return pl.pallas_call(
        kernel,
        grid=(T // tm,),
        in_specs=[pl.BlockSpec((tm, D), lambda i: (i, 0)),
                  pl.BlockSpec((tm, p2d.shape[1]), lambda i: (i, 0)),
                  _resident(gf.shape), _resident(wu.shape), _resident(wd.shape),
                  _resident(gp.shape), _resident(wpg.shape), _resident(wpp.shape),
                  _resident(gl.shape)],
        out_specs=pl.BlockSpec((tm, D), lambda i: (i, 0)),
        out_shape=jax.ShapeDtypeStruct((T, D), F32),
        compiler_params=_params(("parallel",)),
        name="ffn",
    )(x2d, p2d, gf, wu, wd, gp, wpg, wpp, gl)


def _layer(x2d, p2d, s_ret, s_conv, pos, w, *, n_batch, seq_len, chunk, n_chunks, n_seq,
           tm_in, tm_mix, tm_ffn, final_norm):
    cos2, sin2, decay, qd, kd, cd = _tables(pos, chunk)
    a, b, f, s_conv_new = _inproj(x2d, w["g_mix"], w["w_in"], w["conv_w"], s_conv, cos2, sin2, qd, kd,
                                  n_batch=n_batch, seq_len=seq_len, chunk=chunk, tm=tm_in)
    og, s_ret_new = _retention(a, b, f, s_ret, decay, cd, n_batch=n_batch, seq_len=seq_len,
                               chunk=chunk, n_chunks=n_chunks, n_seq=n_seq)
    x1 = _mix(x2d, og, b, f, w["w_ret_out"], w["w_conv_out"], w["w_o"], tm=tm_mix)
    y = _ffn(x1, p2d, w["g_ffn"], w["w_up"], w["w_down"], w["g_ple"], w["w_ple_gate"],
             w["w_ple_proj"], w["g_final"], tm=tm_ffn, final_norm=final_norm)
    return y, s_ret_new, s_conv_new


def kernel(x_prompt, x_sample, state_ret, state_conv, p_prompt, p_sample, g_mix, w_in, w_ret_out,
           conv_w, w_conv_out, w_o, g_ffn, w_up, w_down, g_ple, w_ple_gate, w_ple_proj, g_final):
    depth = w_in.shape[0]
    B, L, D = x_prompt.shape
    Bs, Ls, _ = x_sample.shape
    pos_prompt = jnp.arange(L, dtype=F32)
    pos_sample = PAST_LEN + jnp.arange(Ls, dtype=F32)
    chunk_prompt = min(RET_CHUNK, L)

    hp = x_prompt.reshape(B * L, D)
    hs = x_sample.reshape(Bs * Ls, D)
    rp, cp, rs, cs = [], [], [], []
    for i in range(depth):
        w = {"g_mix": g_mix[i][None], "w_in": w_in[i].astype(BF16),
             "w_ret_out": w_ret_out[i].astype(BF16), "conv_w": conv_w[i],
             "w_conv_out": w_conv_out[i].astype(BF16), "w_o": w_o[i].astype(BF16),
             "g_ffn": g_ffn[i][None], "w_up": w_up[i].astype(BF16), "w_down": w_down[i].astype(BF16),
             "g_ple": g_ple[i][None], "w_ple_gate": w_ple_gate[i].astype(BF16),
             "w_ple_proj": w_ple_proj[i].astype(BF16), "g_final": g_final[None]}
        last = i == depth - 1
        hp, r, c = _layer(hp, p_prompt[i].reshape(B * L, -1), None, None, pos_prompt, w,
                          n_batch=B, seq_len=L, chunk=chunk_prompt, n_chunks=4, n_seq=1,
                          tm_in=1024, tm_mix=512, tm_ffn=512, final_norm=last)
        rp.append(r)
        cp.append(c)
        hs, r, c = _layer(hs, p_sample[i].reshape(Bs * Ls, -1), state_ret[i], state_conv[i],
                          pos_sample, w, n_batch=Bs, seq_len=Ls, chunk=Ls, n_chunks=1, n_seq=4,
                          tm_in=1024, tm_mix=512, tm_ffn=512, final_norm=last)
        rs.append(r)
        cs.append(c)
    return (hp.reshape(B, L, D), hs.reshape(Bs, Ls, D), jnp.stack(rp), jnp.stack(cp),
            jnp.stack(rs), jnp.stack(cs))
```

```python
import functools

import jax
import jax.numpy as jnp
from jax import lax
from jax.experimental import pallas as pl
from jax.experimental.pallas import tpu as pltpu

F32 = jnp.float32
BF16 = jnp.bfloat16

N_HEADS = 8
DK = 128
DV = 256
CONV_W = 3
RET_CHUNK = 128
PAST_LEN = 16384
ROPE_BASE = 10000.0
EPS = 1e-6

V7X_SUBLANES = 8
V7X_VMEM_BYTES = 64 * 1024 * 1024
V7X_VMEM_LIMIT_BYTES = V7X_VMEM_BYTES - 4 * 1024 * 1024

COL_TILE = 1024

J_Q, J_K, J_V, J_GR, J_CX, J_CB, J_CC, J_GA = 0, 1, 2, 4, 6, 7, 8, 9


def _rms(x, g):
    return x * lax.rsqrt(jnp.mean(x * x, axis=-1, keepdims=True) + EPS) * g


def _dot(a, b):
    return jnp.dot(a, b, preferred_element_type=F32)


def _params(semantics):
    return pltpu.CompilerParams(dimension_semantics=semantics,
                                vmem_limit_bytes=V7X_VMEM_LIMIT_BYTES)


def _resident(shape):
    zeros = (0,) * len(shape)
    return pl.BlockSpec(shape, lambda *_: zeros, pipeline_mode=pl.Buffered(1))


def _tables(pos, chunk):
    inv_freq = ROPE_BASE ** (-jnp.arange(0, DK, 2, dtype=F32) / DK)
    ang = pos.astype(F32)[:, None] * inv_freq[None, :]
    c, s = jnp.cos(ang), jnp.sin(ang)
    cos2 = jnp.concatenate([c, c], axis=-1)
    sin2 = jnp.concatenate([-s, s], axis=-1)
    log_g = jnp.log(1.0 - 2.0 ** (-5.0 - jnp.arange(N_HEADS, dtype=F32)))
    idx = jnp.arange(chunk, dtype=F32)
    diff = idx[:, None] - idx[None, :]
    decay = jnp.where(diff >= 0, jnp.exp(jnp.maximum(diff, 0.0)[None] * log_g[:, None, None]), 0.0)
    q_decay = jnp.exp((idx + 1.0)[None, :] * log_g[:, None])
    k_decay = jnp.exp((chunk - 1.0 - idx)[None, :] * log_g[:, None])
    chunk_decay = jnp.exp(chunk * log_g)
    qd = jnp.repeat(q_decay.T, DK, axis=1)
    kd = jnp.repeat(k_decay.T, DK, axis=1)
    return cos2, sin2, decay, qd, kd, chunk_decay


def _inproj_kernel(*refs, tm, chunk, seq_len, has_init):
    if has_init:
        (x_ref, g_ref, w_ref, cos_ref, sin_ref, qd_ref, kd_ref, cw_ref, st_ref,
         a_ref, b_ref, f_ref, cs_ref, h_ref, cx_ref, cb_ref, carry_ref) = refs
    else:
        (x_ref, g_ref, w_ref, cos_ref, sin_ref, qd_ref, kd_ref, cw_ref,
         a_ref, b_ref, f_ref, cs_ref, h_ref, cx_ref, cb_ref, carry_ref) = refs
    i = pl.program_id(0)
    j = pl.program_id(1)
    n_pos = cos_ref.shape[0]

    @pl.when(j == 0)
    def _():
        h_ref[...] = _rms(x_ref[...], g_ref[...]).astype(BF16)

    def proj():
        return _dot(h_ref[...], w_ref[...])

    def rotary_heads(acc, dec_ref, post_scale):
        plain, scaled = [], []
        for h in range(N_HEADS):
            cols = slice(h * DK, (h + 1) * DK)
            t = acc[:, cols]
            t3 = t.reshape(tm // n_pos, n_pos, DK)
            r3 = pltpu.roll(t, DK // 2, 1).reshape(tm // n_pos, n_pos, DK)
            r = (t3 * cos_ref[...][None] + r3 * sin_ref[...][None]).reshape(tm, DK)
            if post_scale != 1.0:
                r = r * post_scale
            plain.append(r.astype(BF16))
            d = r.reshape(tm // chunk, chunk, DK) * dec_ref[:, cols][None]
            scaled.append(d.reshape(tm, DK).astype(BF16))
        return jnp.concatenate(plain + scaled, axis=1)

    @pl.when(j == J_Q)
    def _():
        a_ref[...] = rotary_heads(proj(), qd_ref, DK ** -0.5)

    @pl.when(j == J_K)
    def _():
        a_ref[...] = rotary_heads(proj(), kd_ref, 1.0)

    @pl.when((j == J_V) | (j == J_V + 1))
    def _():
        b_ref[...] = proj().astype(BF16)

    @pl.when((j == J_GR) | (j == J_GR + 1))
    def _():
        f_ref[...] = jax.nn.silu(proj())

    @pl.when(j == J_CX)
    def _():
        cx_ref[...] = proj()

    @pl.when(j == J_CB)
    def _():
        cb_ref[...] = proj()

    @pl.when(j == J_CC)
    def _():
        u = proj() * cx_ref[...]
        row = lax.broadcasted_iota(jnp.int32, u.shape, 0)
        if has_init:
            n = tm // seq_len
            prev2 = jnp.broadcast_to(st_ref[:, 0:1, :], (n, seq_len, u.shape[1])).reshape(u.shape)
            prev1 = jnp.broadcast_to(st_ref[:, 1:2, :], (n, seq_len, u.shape[1])).reshape(u.shape)
            pos = row % seq_len
        else:
            first = (i % (seq_len // tm)) == 0
            halo = jnp.where(first, 0.0, carry_ref[...])
            prev2 = jnp.broadcast_to(halo[V7X_SUBLANES - 2:V7X_SUBLANES - 1, :], u.shape)
            prev1 = jnp.broadcast_to(halo[V7X_SUBLANES - 1:V7X_SUBLANES, :], u.shape)
            pos = row
        u1 = jnp.where(pos == 0, prev1, pltpu.roll(u, 1, 0))
        u2 = jnp.where(pos == 0, prev2, jnp.where(pos == 1, prev1, pltpu.roll(u, 2, 0)))
        cv = cw_ref[0:1, :] * u2 + cw_ref[1:2, :] * u1 + cw_ref[2:3, :] * u
        b_ref[...] = (cb_ref[...] * cv).astype(BF16)
        if has_init:
            cs_ref[...] = u.reshape(tm // seq_len, seq_len, u.shape[1])[:, seq_len - (CONV_W - 1):, :]
        else:
            carry_ref[...] = u[tm - V7X_SUBLANES:, :]
            cs_ref[0] = u[tm - (CONV_W - 1):, :]

    @pl.when(j >= J_GA)
    def _():
        f_ref[...] = jax.nn.sigmoid(proj())


def _inproj(x2d, g, w, cw, conv_state, cos2, sin2, qd, kd, *, n_batch, seq_len, chunk, tm):
    T, D = x2d.shape
    has_init = conv_state is not None
    n_col = w.shape[1] // COL_TILE
    assert n_col == 11 and D == COL_TILE
    if has_init:
        assert tm % seq_len == 0
        n = tm // seq_len
        pos_spec = pl.BlockSpec((seq_len, DK), lambda i, j: (0, 0))
        st_specs = [pl.BlockSpec((n, CONV_W - 1, D), lambda i, j: (i, 0, 0))]
        st_args = [conv_state]
        cs_spec = pl.BlockSpec((n, CONV_W - 1, D), lambda i, j: (i, 0, 0))
    else:
        assert seq_len % tm == 0 and tm % chunk == 0
        tiles = seq_len // tm
        pos_spec = pl.BlockSpec((tm, DK), lambda i, j: (i % tiles, 0))
        st_specs, st_args = [], []
        cs_spec = pl.BlockSpec((1, CONV_W - 1, D), lambda i, j: (i // tiles, 0, 0))

    def b_col(j):
        return jnp.where(j <= J_V, 0, jnp.where(j < J_CC, 1, 2))

    def f_col(j):
        return jnp.where(j <= J_GR, 0, jnp.where(j < J_GA, 1, jnp.where(j == J_GA, 2, 3)))

    kernel = functools.partial(_inproj_kernel, tm=tm, chunk=chunk, seq_len=seq_len, has_init=has_init)
    return pl.pallas_call(
        kernel,
        grid=(T // tm, n_col),
        in_specs=[pl.BlockSpec((tm, D), lambda i, j: (i, 0)),
                  pl.BlockSpec((1, D), lambda i, j: (0, 0)),
                  pl.BlockSpec((D, COL_TILE), lambda i, j: (0, j)),
                  pos_spec, pos_spec,
                  _resident(qd.shape), _resident(kd.shape), _resident(cw.shape)] + st_specs,
        out_specs=[pl.BlockSpec((tm, 2 * COL_TILE), lambda i, j: (i, jnp.minimum(j, 1))),
                   pl.BlockSpec((tm, COL_TILE), lambda i, j: (i, b_col(j))),
                   pl.BlockSpec((tm, COL_TILE), lambda i, j: (i, f_col(j))),
                   cs_spec],
        out_shape=[jax.ShapeDtypeStruct((T, 4 * COL_TILE), BF16),
                   jax.ShapeDtypeStruct((T, 3 * COL_TILE), BF16),
                   jax.ShapeDtypeStruct((T, 4 * COL_TILE), F32),
                   jax.ShapeDtypeStruct((n_batch, CONV_W - 1, D), F32)],
        scratch_shapes=[pltpu.VMEM((tm, D), BF16), pltpu.VMEM((tm, D), F32), pltpu.VMEM((tm, D), F32),
                        pltpu.VMEM((V7X_SUBLANES, D), F32)],
        compiler_params=_params(("arbitrary", "arbitrary")),
        name="inproj_init" if has_init else "inproj",
    )(x2d, g, w, cos2, sin2, qd, kd, cw, *st_args)


def _retention_kernel(*refs, chunk, n_chunks, n_seq, has_init, with_mix):
    refs = list(refs)
    q_ref, qd_ref, k_ref, kd_ref, v0_ref, v1_ref, sg_ref, dec_ref, cd_ref = refs[:9]
    del refs[:9]
    s0_ref = refs.pop(0) if has_init else None
    if with_mix:
        x_ref, cbv_ref, sa_ref, sb_ref, wro_ref, wco_ref, wo_ref, x1_ref, s_ref, o_ref = refs
    else:
        o_ref, s_ref = refs
    if not has_init:
        @pl.when(pl.program_id(1) == 0)
        def _():
            s_ref[...] = jnp.zeros_like(s_ref)

    half = N_HEADS // 2
    fuse = chunk % DK == 0

    def v_at(rows, h):
        v_ref = v0_ref if h < half else v1_ref
        return v_ref[rows, pl.ds((h % half) * DV, DV)]

    for seq in range(n_seq):
        row_slices = [pl.ds((seq * n_chunks + c) * chunk, chunk) for c in range(n_chunks)]
        probs, kvs = {}, {}
        for c, rows in enumerate(row_slices):
            for h in range(N_HEADS):
                kcols = pl.ds(h * DK, DK)
                scores = lax.dot_general(q_ref[rows, kcols], k_ref[rows, kcols], (((1,), (1,)), ((), ())),
                                         preferred_element_type=F32) * dec_ref[h]
                probs[c, h] = scores.astype(BF16)
                kdt = jnp.transpose(kd_ref[rows, kcols].astype(F32)).astype(BF16)
                kvs[c, h] = _dot(kdt, v_at(rows, h))
        if has_init:
            states = [s0_ref[seq, h] for h in range(N_HEADS)]
        else:
            states = [s_ref[seq, h] for h in range(N_HEADS)]
        for c, rows in enumerate(row_slices):
            for h in range(N_HEADS):
                kcols = pl.ds(h * DK, DK)
                ocols = pl.ds(h * DV, DV)
                S = states[h]
                if fuse:
                    o = _dot(jnp.concatenate([probs[c, h], qd_ref[rows, kcols]], axis=1),
                             jnp.concatenate([v_at(rows, h), S.astype(BF16)], axis=0))
                else:
                    o = _dot(probs[c, h], v_at(rows, h)) + _dot(qd_ref[rows, kcols], S.astype(BF16))
                states[h] = S * cd_ref[h] + kvs[c, h]
                o = o * lax.rsqrt(jnp.mean(o * o, axis=-1, keepdims=True) + EPS)
                o_ref[rows, ocols] = (o * sg_ref[rows, ocols]).astype(BF16)
        for h in range(N_HEADS):
            s_ref[seq, h] = states[h]

    if with_mix:
        y_conv = _dot(cbv_ref[...], wco_ref[...])
        merged = sa_ref[...] * _dot(o_ref[...], wro_ref[...]) + sb_ref[...] * y_conv
        x1_ref[...] = x_ref[...] + _dot(merged.astype(BF16), wo_ref[...])


def _retention(a, b, f, s0, decay, cd, mix_args, *, n_batch, seq_len, chunk, n_chunks, n_seq):
    T = a.shape[0]
    rows = n_seq * n_chunks * chunk
    steps = seq_len // (n_chunks * chunk) if n_seq == 1 else 1
    has_init = s0 is not None
    with_mix = mix_args is not None

    def tile(width, col):
        return pl.BlockSpec((rows, width), lambda b_, t: (b_ * steps + t, col))

    in_specs = [tile(COL_TILE, 0), tile(COL_TILE, 1), tile(COL_TILE, 2), tile(COL_TILE, 3),
                tile(COL_TILE, 0), tile(COL_TILE, 1),
                tile(N_HEADS * DV, 0),
                _resident(decay.shape), pl.BlockSpec(memory_space=pltpu.SMEM)]
    args = [a, a, a, a, b, b, f, decay, cd]
    state_spec = pl.BlockSpec((n_seq, N_HEADS, DK, DV), lambda b_, t: (b_, 0, 0, 0))
    state_shape = jax.ShapeDtypeStruct((n_batch, N_HEADS, DK, DV), F32)
    if has_init:
        in_specs.append(state_spec)
        args.append(s0)
    if with_mix:
        x2d, wro, wco, wo = mix_args
        D = x2d.shape[1]
        in_specs += [tile(D, 0), tile(D, 2), tile(D, 2), tile(D, 3),
                     _resident(wro.shape), _resident(wco.shape), _resident(wo.shape)]
        args += [x2d, b, f, f, wro, wco, wo]
        out_specs = [tile(D, 0), state_spec]
        out_shape = [jax.ShapeDtypeStruct((T, D), F32), state_shape]
        scratch = [pltpu.VMEM((rows, N_HEADS * DV), BF16)]
    else:
        out_specs = [tile(N_HEADS * DV, 0), state_spec]
        out_shape = [jax.ShapeDtypeStruct((T, N_HEADS * DV), BF16), state_shape]
        scratch = []
    kernel = functools.partial(_retention_kernel, chunk=chunk, n_chunks=n_chunks, n_seq=n_seq,
                               has_init=has_init, with_mix=with_mix)
    return pl.pallas_call(
        kernel,
        grid=(n_batch // n_seq, steps),
        in_specs=in_specs,
        out_specs=out_specs,
        out_shape=out_shape,
        scratch_shapes=scratch,
        compiler_params=_params(("parallel", "arbitrary")),
        name="retention_mix" if with_mix else "retention",
    )(*args)


def _mix_kernel(x_ref, og_ref, cbv_ref, sa_ref, sb_ref, wro_ref, wco_ref, wo_ref, x1_ref):
    y_ret = _dot(og_ref[...], wro_ref[...])
    y_conv = _dot(cbv_ref[...], wco_ref[...])
    merged = sa_ref[...] * y_ret + sb_ref[...] * y_conv
    x1_ref[...] = x_ref[...] + _dot(merged.astype(BF16), wo_ref[...])


def _mix(x2d, og, b, f, wro, wco, wo, *, tm):
    T, D = x2d.shape

    def tile(width, col):
        return pl.BlockSpec((tm, width), lambda i: (i, col))

    return pl.pallas_call(
        _mix_kernel,
        grid=(T // tm,),
        in_specs=[tile(D, 0), tile(N_HEADS * DV, 0), tile(D, 2), tile(D, 2), tile(D, 3),
                  _resident(wro.shape), _resident(wco.shape), _resident(wo.shape)],
        out_specs=tile(D, 0),
        out_shape=jax.ShapeDtypeStruct((T, D), F32),
        compiler_params=_params(("parallel",)),
        name="mix",
    )(x2d, og, b, f, f, wro, wco, wo)


def _ffn_kernel(x_ref, p_ref, gf_ref, wu_ref, wd_ref, gp_ref, wpg_ref, wpp_ref, gl_ref, o_ref,
                *, final_norm):
    x = x_ref[...]
    hf = _rms(x, gf_ref[...]).astype(BF16)
    d_ff = wu_ref.shape[1]
    acc = x
    for c in range(d_ff // COL_TILE):
        cols = pl.ds(c * COL_TILE, COL_TILE)
        hid = jnp.square(jnp.maximum(_dot(hf, wu_ref[:, cols]), 0.0)).astype(BF16)
        acc = acc + _dot(hid, wd_ref[cols, :])
    gate = jax.nn.sigmoid(_dot(_rms(acc, gp_ref[...]).astype(BF16), wpg_ref[...]))
    y = acc + gate * _dot(p_ref[...].astype(BF16), wpp_ref[...])
    if final_norm:
        y = _rms(y, gl_ref[...])
    o_ref[...] = y


def _ffn(x2d, p2d, gf, wu, wd, gp, wpg, wpp, gl, *, tm, final_norm):
    T, D = x2d.shape
    kernel = functools.partial(_ffn_kernel, final_norm=final_norm)
    return pl.pallas_call(
        kernel,
        grid=(T // tm,),
        in_specs=[pl.BlockSpec((tm, D), lambda i: (i, 0)),
                  pl.BlockSpec((tm, p2d.shape[1]), lambda i: (i, 0)),
                  _resident(gf.shape), _resident(wu.shape), _resident(wd.shape),
                  _resident(gp.shape), _resident(wpg.shape), _resident(wpp.shape),
                  _resident(gl.shape)],
        out_specs=pl.BlockSpec((tm, D), lambda i: (i, 0)),
        out_shape=jax.ShapeDtypeStruct((T, D), F32),
        compiler_params=_params(("parallel",)),
        name="ffn",
    )(x2d, p2d, gf, wu, wd, gp, wpg, wpp, gl)


def _layer(x2d, p2d, s_ret, s_conv, pos, w, *, n_batch, seq_len, chunk, n_chunks, n_seq,
           tm_in, tm_mix, tm_ffn, final_norm):
    cos2, sin2, decay, qd, kd, cd = _tables(pos, chunk)
    a, b, f, s_conv_new = _inproj(x2d, w["g_mix"], w["w_in"], w["conv_w"], s_conv, cos2, sin2, qd, kd,
                                  n_batch=n_batch, seq_len=seq_len, chunk=chunk, tm=tm_in)
    ret = functools.partial(_retention, a, b, f, s_ret, decay, cd, n_batch=n_batch, seq_len=seq_len,
                            chunk=chunk, n_chunks=n_chunks, n_seq=n_seq)
    if tm_mix is None:
        x1, s_ret_new = ret((x2d, w["w_ret_out"], w["w_conv_out"], w["w_o"]))
    else:
        og, s_ret_new = ret(None)
        x1 = _mix(x2d, og, b, f, w["w_ret_out"], w["w_conv_out"], w["w_o"], tm=tm_mix)
    y = _ffn(x1, p2d, w["g_ffn"], w["w_up"], w["w_down"], w["g_ple"], w["w_ple_gate"],
             w["w_ple_proj"], w["g_final"], tm=tm_ffn, final_norm=final_norm)
    return y, s_ret_new, s_conv_new


def kernel(x_prompt, x_sample, state_ret, state_conv, p_prompt, p_sample, g_mix, w_in, w_ret_out,
           conv_w, w_conv_out, w_o, g_ffn, w_up, w_down, g_ple, w_ple_gate, w_ple_proj, g_final):
    depth = w_in.shape[0]
    B, L, D = x_prompt.shape
    Bs, Ls, _ = x_sample.shape
    pos_prompt = jnp.arange(L, dtype=F32)
    pos_sample = PAST_LEN + jnp.arange(Ls, dtype=F32)
    chunk_prompt = min(RET_CHUNK, L)

    hp = x_prompt.reshape(B * L, D)
    hs = x_sample.reshape(Bs * Ls, D)
    rp, cp, rs, cs = [], [], [], []
    for i in range(depth):
        w = {"g_mix": g_mix[i][None], "w_in": w_in[i].astype(BF16),
             "w_ret_out": w_ret_out[i].astype(BF16), "conv_w": conv_w[i],
             "w_conv_out": w_conv_out[i].astype(BF16), "w_o": w_o[i].astype(BF16),
             "g_ffn": g_ffn[i][None], "w_up": w_up[i].astype(BF16), "w_down": w_down[i].astype(BF16),
             "g_ple": g_ple[i][None], "w_ple_gate": w_ple_gate[i].astype(BF16),
             "w_ple_proj": w_ple_proj[i].astype(BF16), "g_final": g_final[None]}
        last = i == depth - 1
        hp, r, c = _layer(hp, p_prompt[i].reshape(B * L, -1), None, None, pos_prompt, w,
                          n_batch=B, seq_len=L, chunk=chunk_prompt, n_chunks=4, n_seq=1,
                          tm_in=1024, tm_mix=None, tm_ffn=1024, final_norm=last)
        rp.append(r)
        cp.append(c)
        hs, r, c = _layer(hs, p_sample[i].reshape(Bs * Ls, -1), state_ret[i], state_conv[i],
                          pos_sample, w, n_batch=Bs, seq_len=Ls, chunk=Ls, n_chunks=1, n_seq=4,
                          tm_in=1024, tm_mix=512, tm_ffn=512, final_norm=last)
        rs.append(r)
        cs.append(c)
    return (hp.reshape(B, L, D), hs.reshape(Bs, Ls, D), jnp.stack(rp), jnp.stack(cp),
            jnp.stack(rs), jnp.stack(cs))
```

```python
import functools

import jax
import jax.numpy as jnp
from jax import lax
from jax.experimental import pallas as pl
from jax.experimental.pallas import tpu as pltpu

F32 = jnp.float32
BF16 = jnp.bfloat16

N_HEADS = 8
DK = 128
DV = 256
CONV_W = 3
RET_CHUNK = 128
PAST_LEN = 16384
ROPE_BASE = 10000.0
EPS = 1e-6

V7X_SUBLANES = 8
V7X_VMEM_BYTES = 64 * 1024 * 1024
V7X_VMEM_LIMIT_BYTES = V7X_VMEM_BYTES - 4 * 1024 * 1024

COL_TILE = 1024

IN_GROUPS = 3
IN_GROUP_TILES = 4


def _rms(x, g):
    return x * lax.rsqrt(jnp.mean(x * x, axis=-1, keepdims=True) + EPS) * g


def _dot(a, b):
    return jnp.dot(a, b, preferred_element_type=F32)


def _params(semantics):
    return pltpu.CompilerParams(dimension_semantics=semantics,
                                vmem_limit_bytes=V7X_VMEM_LIMIT_BYTES)


def _resident(shape):
    zeros = (0,) * len(shape)
    return pl.BlockSpec(shape, lambda *_: zeros, pipeline_mode=pl.Buffered(1))


def _tables(pos, chunk):
    inv_freq = ROPE_BASE ** (-jnp.arange(0, DK, 2, dtype=F32) / DK)
    ang = pos.astype(F32)[:, None] * inv_freq[None, :]
    c, s = jnp.cos(ang), jnp.sin(ang)
    cos2 = jnp.concatenate([c, c], axis=-1)
    sin2 = jnp.concatenate([-s, s], axis=-1)
    log_g = jnp.log(1.0 - 2.0 ** (-5.0 - jnp.arange(N_HEADS, dtype=F32)))
    idx = jnp.arange(chunk, dtype=F32)
    diff = idx[:, None] - idx[None, :]
    decay = jnp.where(diff >= 0, jnp.exp(jnp.maximum(diff, 0.0)[None] * log_g[:, None, None]), 0.0)
    q_decay = jnp.exp((idx + 1.0)[None, :] * log_g[:, None])
    k_decay = jnp.exp((chunk - 1.0 - idx)[None, :] * log_g[:, None])
    chunk_decay = jnp.exp(chunk * log_g)
    qd = jnp.repeat(q_decay.T, DK, axis=1)
    kd = jnp.repeat(k_decay.T, DK, axis=1)
    return cos2, sin2, decay, qd, kd, chunk_decay


def _inproj_kernel(*refs, tm, chunk, seq_len, has_init):
    if has_init:
        (x_ref, g_ref, w_ref, cos_ref, sin_ref, qd_ref, kd_ref, cw_ref, st_ref,
         a_ref, b_ref, f_ref, cs_ref, h_ref, cx_ref, cb_ref, carry_ref) = refs
    else:
        (x_ref, g_ref, w_ref, cos_ref, sin_ref, qd_ref, kd_ref, cw_ref,
         a_ref, b_ref, f_ref, cs_ref, h_ref, cx_ref, cb_ref, carry_ref) = refs
    i = pl.program_id(0)
    j = pl.program_id(1)
    n_pos = cos_ref.shape[0]

    def proj(t, h=None):
        h = h_ref[...] if h is None else h
        return _dot(h, w_ref[:, t * COL_TILE:(t + 1) * COL_TILE])

    def rotary_heads(acc, dec_ref, post_scale):
        plain, scaled = [], []
        for h in range(N_HEADS):
            cols = slice(h * DK, (h + 1) * DK)
            t = acc[:, cols]
            t3 = t.reshape(tm // n_pos, n_pos, DK)
            r3 = pltpu.roll(t, DK // 2, 1).reshape(tm // n_pos, n_pos, DK)
            r = (t3 * cos_ref[...][None] + r3 * sin_ref[...][None]).reshape(tm, DK)
            if post_scale != 1.0:
                r = r * post_scale
            plain.append(r.astype(BF16))
            d = r.reshape(tm // chunk, chunk, DK) * dec_ref[:, cols][None]
            scaled.append(d.reshape(tm, DK).astype(BF16))
        return jnp.concatenate(plain + scaled, axis=1)

    @pl.when(j == 0)
    def _():
        h = _rms(x_ref[...], g_ref[...]).astype(BF16)
        h_ref[...] = h
        a_ref[:, :2 * COL_TILE] = rotary_heads(proj(0, h), qd_ref, DK ** -0.5)
        a_ref[:, 2 * COL_TILE:] = rotary_heads(proj(1, h), kd_ref, 1.0)
        b_ref[:, :COL_TILE] = proj(2, h).astype(BF16)
        b_ref[:, COL_TILE:] = proj(3, h).astype(BF16)

    @pl.when(j == 1)
    def _():
        gr0, gr1 = proj(0), proj(1)
        cx_ref[...] = proj(2)
        cb_ref[...] = proj(3)
        f_ref[:, :COL_TILE] = jax.nn.silu(gr0)
        f_ref[:, COL_TILE:] = jax.nn.silu(gr1)

    @pl.when(j == 2)
    def _():
        cc, ga, gb = proj(0), proj(1), proj(2)
        f_ref[:, :COL_TILE] = jax.nn.sigmoid(ga)
        f_ref[:, COL_TILE:] = jax.nn.sigmoid(gb)
        u = cc * cx_ref[...]
        row = lax.broadcasted_iota(jnp.int32, u.shape, 0)
        if has_init:
            n = tm // seq_len
            prev2 = jnp.broadcast_to(st_ref[:, 0:1, :], (n, seq_len, u.shape[1])).reshape(u.shape)
            prev1 = jnp.broadcast_to(st_ref[:, 1:2, :], (n, seq_len, u.shape[1])).reshape(u.shape)
            pos = row % seq_len
        else:
            first = (i % (seq_len // tm)) == 0
            halo = jnp.where(first, 0.0, carry_ref[...])
            prev2 = jnp.broadcast_to(halo[V7X_SUBLANES - 2:V7X_SUBLANES - 1, :], u.shape)
            prev1 = jnp.broadcast_to(halo[V7X_SUBLANES - 1:V7X_SUBLANES, :], u.shape)
            pos = row
        u1 = jnp.where(pos == 0, prev1, pltpu.roll(u, 1, 0))
        u2 = jnp.where(pos == 0, prev2, jnp.where(pos == 1, prev1, pltpu.roll(u, 2, 0)))
        cv = cw_ref[0:1, :] * u2 + cw_ref[1:2, :] * u1 + cw_ref[2:3, :] * u
        b_ref[:, :COL_TILE] = (cb_ref[...] * cv).astype(BF16)
        b_ref[:, COL_TILE:] = jnp.zeros((tm, COL_TILE), BF16)
        if has_init:
            cs_ref[...] = u.reshape(tm // seq_len, seq_len, u.shape[1])[:, seq_len - (CONV_W - 1):, :]
        else:
            carry_ref[...] = u[tm - V7X_SUBLANES:, :]
            cs_ref[0] = u[tm - (CONV_W - 1):, :]


def _inproj(x2d, g, w, cw, conv_state, cos2, sin2, qd, kd, *, n_batch, seq_len, chunk, tm):
    T, D = x2d.shape
    has_init = conv_state is not None
    assert w.shape[1] == IN_GROUPS * IN_GROUP_TILES * COL_TILE and D == COL_TILE
    if has_init:
        assert tm % seq_len == 0
        n = tm // seq_len
        pos_spec = pl.BlockSpec((seq_len, DK), lambda i, j: (0, 0))
        st_specs = [pl.BlockSpec((n, CONV_W - 1, D), lambda i, j: (i, 0, 0))]
        st_args = [conv_state]
        cs_spec = pl.BlockSpec((n, CONV_W - 1, D), lambda i, j: (i, 0, 0))
    else:
        assert seq_len % tm == 0 and tm % chunk == 0
        tiles = seq_len // tm
        pos_spec = pl.BlockSpec((tm, DK), lambda i, j: (i % tiles, 0))
        st_specs, st_args = [], []
        cs_spec = pl.BlockSpec((1, CONV_W - 1, D), lambda i, j: (i // tiles, 0, 0))

    kernel = functools.partial(_inproj_kernel, tm=tm, chunk=chunk, seq_len=seq_len, has_init=has_init)
    return pl.pallas_call(
        kernel,
        grid=(T // tm, IN_GROUPS),
        in_specs=[pl.BlockSpec((tm, D), lambda i, j: (i, 0)),
                  pl.BlockSpec((1, D), lambda i, j: (0, 0)),
                  pl.BlockSpec((D, IN_GROUP_TILES * COL_TILE), lambda i, j: (0, j)),
                  pos_spec, pos_spec,
                  _resident(qd.shape), _resident(kd.shape), _resident(cw.shape)] + st_specs,
        out_specs=[pl.BlockSpec((tm, 4 * COL_TILE), lambda i, j: (i, 0)),
                   pl.BlockSpec((tm, 2 * COL_TILE), lambda i, j: (i, j // 2)),
                   pl.BlockSpec((tm, 2 * COL_TILE), lambda i, j: (i, jnp.maximum(j - 1, 0))),
                   cs_spec],
        out_shape=[jax.ShapeDtypeStruct((T, 4 * COL_TILE), BF16),
                   jax.ShapeDtypeStruct((T, 4 * COL_TILE), BF16),
                   jax.ShapeDtypeStruct((T, 4 * COL_TILE), F32),
                   jax.ShapeDtypeStruct((n_batch, CONV_W - 1, D), F32)],
        scratch_shapes=[pltpu.VMEM((tm, D), BF16), pltpu.VMEM((tm, D), F32), pltpu.VMEM((tm, D), F32),
                        pltpu.VMEM((V7X_SUBLANES, D), F32)],
        compiler_params=_params(("arbitrary", "arbitrary")),
        name="inproj_init" if has_init else "inproj",
    )(x2d, g, w, cos2, sin2, qd, kd, cw, *st_args)


def _retention_kernel(*refs, chunk, n_chunks, n_seq, has_init, with_mix):
    refs = list(refs)
    q_ref, qd_ref, k_ref, kd_ref, v0_ref, v1_ref, sg_ref, dec_ref, cd_ref = refs[:9]
    del refs[:9]
    s0_ref = refs.pop(0) if has_init else None
    if with_mix:
        x_ref, cbv_ref, sa_ref, sb_ref, wro_ref, wco_ref, wo_ref, x1_ref, s_ref, o_ref = refs
    else:
        o_ref, s_ref = refs
    if not has_init:
        @pl.when(pl.program_id(1) == 0)
        def _():
            s_ref[...] = jnp.zeros_like(s_ref)

    half = N_HEADS // 2
    fuse = chunk % DK == 0

    def v_at(rows, h):
        v_ref = v0_ref if h < half else v1_ref
        return v_ref[rows, pl.ds((h % half) * DV, DV)]

    for seq in range(n_seq):
        row_slices = [pl.ds((seq * n_chunks + c) * chunk, chunk) for c in range(n_chunks)]
        probs, kvs = {}, {}
        for c, rows in enumerate(row_slices):
            for h in range(N_HEADS):
                kcols = pl.ds(h * DK, DK)
                scores = lax.dot_general(q_ref[rows, kcols], k_ref[rows, kcols], (((1,), (1,)), ((), ())),
                                         preferred_element_type=F32) * dec_ref[h]
                probs[c, h] = scores.astype(BF16)
                kdt = jnp.transpose(kd_ref[rows, kcols].astype(F32)).astype(BF16)
                kvs[c, h] = _dot(kdt, v_at(rows, h))
        if has_init:
            states = [s0_ref[seq, h] for h in range(N_HEADS)]
        else:
            states = [s_ref[seq, h] for h in range(N_HEADS)]
        for c, rows in enumerate(row_slices):
            for h in range(N_HEADS):
                kcols = pl.ds(h * DK, DK)
                ocols = pl.ds(h * DV, DV)
                S = states[h]
                if fuse:
                    o = _dot(jnp.concatenate([probs[c, h], qd_ref[rows, kcols]], axis=1),
                             jnp.concatenate([v_at(rows, h), S.astype(BF16)], axis=0))
                else:
                    o = _dot(probs[c, h], v_at(rows, h)) + _dot(qd_ref[rows, kcols], S.astype(BF16))
                states[h] = S * cd_ref[h] + kvs[c, h]
                o = o * lax.rsqrt(jnp.mean(o * o, axis=-1, keepdims=True) + EPS)
                o_ref[rows, ocols] = (o * sg_ref[rows, ocols]).astype(BF16)
        for h in range(N_HEADS):
            s_ref[seq, h] = states[h]

    if with_mix:
        y_conv = _dot(cbv_ref[...], wco_ref[...])
        merged = sa_ref[...] * _dot(o_ref[...], wro_ref[...]) + sb_ref[...] * y_conv
        x1_ref[...] = x_ref[...] + _dot(merged.astype(BF16), wo_ref[...])


def _retention(a, b, f, s0, decay, cd, mix_args, *, n_batch, seq_len, chunk, n_chunks, n_seq):
    T = a.shape[0]
    rows = n_seq * n_chunks * chunk
    steps = seq_len // (n_chunks * chunk) if n_seq == 1 else 1
    has_init = s0 is not None
    with_mix = mix_args is not None

    def tile(width, col):
        return pl.BlockSpec((rows, width), lambda b_, t: (b_ * steps + t, col))

    in_specs = [tile(COL_TILE, 0), tile(COL_TILE, 1), tile(COL_TILE, 2), tile(COL_TILE, 3),
                tile(COL_TILE, 0), tile(COL_TILE, 1),
                tile(N_HEADS * DV, 0),
                _resident(decay.shape), pl.BlockSpec(memory_space=pltpu.SMEM)]
    args = [a, a, a, a, b, b, f, decay, cd]
    state_spec = pl.BlockSpec((n_seq, N_HEADS, DK, DV), lambda b_, t: (b_, 0, 0, 0))
    state_shape = jax.ShapeDtypeStruct((n_batch, N_HEADS, DK, DV), F32)
    if has_init:
        in_specs.append(state_spec)
        args.append(s0)
    if with_mix:
        x2d, wro, wco, wo = mix_args
        D = x2d.shape[1]
        in_specs += [tile(D, 0), tile(D, 2), tile(D, 2), tile(D, 3),
                     _resident(wro.shape), _resident(wco.shape), _resident(wo.shape)]
        args += [x2d, b, f, f, wro, wco, wo]
        out_specs = [tile(D, 0), state_spec]
        out_shape = [jax.ShapeDtypeStruct((T, D), F32), state_shape]
        scratch = [pltpu.VMEM((rows, N_HEADS * DV), BF16)]
    else:
        out_specs = [tile(N_HEADS * DV, 0), state_spec]
        out_shape = [jax.ShapeDtypeStruct((T, N_HEADS * DV), BF16), state_shape]
        scratch = []
    kernel = functools.partial(_retention_kernel, chunk=chunk, n_chunks=n_chunks, n_seq=n_seq,
                               has_init=has_init, with_mix=with_mix)
    return pl.pallas_call(
        kernel,
        grid=(n_batch // n_seq, steps),
        in_specs=in_specs,
        out_specs=out_specs,
        out_shape=out_shape,
        scratch_shapes=scratch,
        compiler_params=_params(("parallel", "arbitrary")),
        name="retention_mix" if with_mix else "retention",
    )(*args)


def _mix_kernel(x_ref, og_ref, cbv_ref, sa_ref, sb_ref, wro_ref, wco_ref, wo_ref, x1_ref):
    y_ret = _dot(og_ref[...], wro_ref[...])
    y_conv = _dot(cbv_ref[...], wco_ref[...])
    merged = sa_ref[...] * y_ret + sb_ref[...] * y_conv
    x1_ref[...] = x_ref[...] + _dot(merged.astype(BF16), wo_ref[...])


def _mix(x2d, og, b, f, wro, wco, wo, *, tm):
    T, D = x2d.shape

    def tile(width, col):
        return pl.BlockSpec((tm, width), lambda i: (i, col))

    return pl.pallas_call(
        _mix_kernel,
        grid=(T // tm,),
        in_specs=[tile(D, 0), tile(N_HEADS * DV, 0), tile(D, 2), tile(D, 2), tile(D, 3),
                  _resident(wro.shape), _resident(wco.shape), _resident(wo.shape)],
        out_specs=tile(D, 0),
        out_shape=jax.ShapeDtypeStruct((T, D), F32),
        compiler_params=_params(("parallel",)),
        name="mix",
    )(x2d, og, b, f, f, wro, wco, wo)


def _ffn_kernel(x_ref, p_ref, gf_ref, wu_ref, wd_ref, gp_ref, wpg_ref, wpp_ref, gl_ref, o_ref,
                *, final_norm):
    x = x_ref[...]
    hf = _rms(x, gf_ref[...]).astype(BF16)
    d_ff = wu_ref.shape[1]
    acc = x
    for c in range(d_ff // COL_TILE):
        cols = pl.ds(c * COL_TILE, COL_TILE)
        hid = jnp.square(jnp.maximum(_dot(hf, wu_ref[:, cols]), 0.0)).astype(BF16)
        acc = acc + _dot(hid, wd_ref[cols, :])
    gate = jax.nn.sigmoid(_dot(_rms(acc, gp_ref[...]).astype(BF16), wpg_ref[...]))
    y = acc + gate * _dot(p_ref[...].astype(BF16), wpp_ref[...])
    if final_norm:
        y = _rms(y, gl_ref[...])
    o_ref[...] = y


def _ffn(x2d, p2d, gf, wu, wd, gp, wpg, wpp, gl, *, tm, final_norm):
    T, D = x2d.shape
    kernel = functools.partial(_ffn_kernel, final_norm=final_norm)
    return pl.pallas_call(
        kernel,
        grid=(T // tm,),
        in_specs=[pl.BlockSpec((tm, D), lambda i: (i, 0)),
                  pl.BlockSpec((tm, p2d.shape[1]), lambda i: (i, 0)),
                  _resident(gf.shape), _resident(wu.shape), _resident(wd.shape),
                  _resident(gp.shape), _resident(wpg.shape), _resident(wpp.shape),
                  _resident(gl.shape)],
        out_specs=pl.BlockSpec((tm, D), lambda i: (i, 0)),
        out_shape=jax.ShapeDtypeStruct((T, D), F32),
        compiler_params=_params(("parallel",)),
        name="ffn",
    )(x2d, p2d, gf, wu, wd, gp, wpg, wpp, gl)


def _layer(x2d, p2d, s_ret, s_conv, pos, w, *, n_batch, seq_len, chunk, n_chunks, n_seq,
           tm_in, tm_mix, tm_ffn, final_norm):
    cos2, sin2, decay, qd, kd, cd = _tables(pos, chunk)
    a, b, f, s_conv_new = _inproj(x2d, w["g_mix"], w["w_in"], w["conv_w"], s_conv, cos2, sin2, qd, kd,
                                  n_batch=n_batch, seq_len=seq_len, chunk=chunk, tm=tm_in)
    ret = functools.partial(_retention, a, b, f, s_ret, decay, cd, n_batch=n_batch, seq_len=seq_len,
                            chunk=chunk, n_chunks=n_chunks, n_seq=n_seq)
    if tm_mix is None:
        x1, s_ret_new = ret((x2d, w["w_ret_out"], w["w_conv_out"], w["w_o"]))
    else:
        og, s_ret_new = ret(None)
        x1 = _mix(x2d, og, b, f, w["w_ret_out"], w["w_conv_out"], w["w_o"], tm=tm_mix)
    y = _ffn(x1, p2d, w["g_ffn"], w["w_up"], w["w_down"], w["g_ple"], w["w_ple_gate"],
             w["w_ple_proj"], w["g_final"], tm=tm_ffn, final_norm=final_norm)
    return y, s_ret_new, s_conv_new


def kernel(x_prompt, x_sample, state_ret, state_conv, p_prompt, p_sample, g_mix, w_in, w_ret_out,
           conv_w, w_conv_out, w_o, g_ffn, w_up, w_down, g_ple, w_ple_gate, w_ple_proj, g_final):
    depth = w_in.shape[0]
    B, L, D = x_prompt.shape
    Bs, Ls, _ = x_sample.shape
    pos_prompt = jnp.arange(L, dtype=F32)
    pos_sample = PAST_LEN + jnp.arange(Ls, dtype=F32)
    chunk_prompt = min(RET_CHUNK, L)

    hp = x_prompt.reshape(B * L, D)
    hs = x_sample.reshape(Bs * Ls, D)
    rp, cp, rs, cs = [], [], [], []
    for i in range(depth):
        pad = IN_GROUPS * IN_GROUP_TILES * COL_TILE - w_in.shape[2]
        w = {"g_mix": g_mix[i][None], "w_in": jnp.pad(w_in[i].astype(BF16), ((0, 0), (0, pad))),
             "w_ret_out": w_ret_out[i].astype(BF16), "conv_w": conv_w[i],
             "w_conv_out": w_conv_out[i].astype(BF16), "w_o": w_o[i].astype(BF16),
             "g_ffn": g_ffn[i][None], "w_up": w_up[i].astype(BF16), "w_down": w_down[i].astype(BF16),
             "g_ple": g_ple[i][None], "w_ple_gate": w_ple_gate[i].astype(BF16),
             "w_ple_proj": w_ple_proj[i].astype(BF16), "g_final": g_final[None]}
        last = i == depth - 1
        hp, r, c = _layer(hp, p_prompt[i].reshape(B * L, -1), None, None, pos_prompt, w,
                          n_batch=B, seq_len=L, chunk=chunk_prompt, n_chunks=4, n_seq=1,
                          tm_in=512, tm_mix=None, tm_ffn=1024, final_norm=last)
        rp.append(r)
        cp.append(c)
        hs, r, c = _layer(hs, p_sample[i].reshape(Bs * Ls, -1), state_ret[i], state_conv[i],
                          pos_sample, w, n_batch=Bs, seq_len=Ls, chunk=Ls, n_chunks=1, n_seq=4,
                          tm_in=512, tm_mix=512, tm_ffn=512, final_norm=last)
        rs.append(r)
        cs.append(c)
    return (hp.reshape(B, L, D), hs.reshape(Bs, Ls, D), jnp.stack(rp), jnp.stack(cp),
            jnp.stack(rs), jnp.stack(cs))
```

```python
import functools

import jax
import jax.numpy as jnp
from jax import lax
from jax.experimental import pallas as pl
from jax.experimental.pallas import tpu as pltpu

F32 = jnp.float32
BF16 = jnp.bfloat16

N_HEADS = 8
DK = 128
DV = 256
CONV_W = 3
RET_CHUNK = 128
PAST_LEN = 16384
ROPE_BASE = 10000.0
EPS = 1e-6

V7X_SUBLANES = 8
V7X_VMEM_BYTES = 64 * 1024 * 1024
V7X_VMEM_LIMIT_BYTES = V7X_VMEM_BYTES - 4 * 1024 * 1024

COL_TILE = 1024

IN_TILE_ORDER = (0, 2, 1, 3, 4, 6, 5, 7, 8, 9, 10)
IN_GROUP_TILES = 2
IN_GROUPS = 6


def _rms(x, g):
    return x * lax.rsqrt(jnp.mean(x * x, axis=-1, keepdims=True) + EPS) * g


def _dot(a, b):
    return jnp.dot(a, b, preferred_element_type=F32)


def _params(semantics):
    return pltpu.CompilerParams(dimension_semantics=semantics,
                                vmem_limit_bytes=V7X_VMEM_LIMIT_BYTES)


def _resident(shape):
    zeros = (0,) * len(shape)
    return pl.BlockSpec(shape, lambda *_: zeros, pipeline_mode=pl.Buffered(1))


def _tables(pos, chunk):
    inv_freq = ROPE_BASE ** (-jnp.arange(0, DK, 2, dtype=F32) / DK)
    ang = pos.astype(F32)[:, None] * inv_freq[None, :]
    c, s = jnp.cos(ang), jnp.sin(ang)
    cos2 = jnp.concatenate([c, c], axis=-1)
    sin2 = jnp.concatenate([-s, s], axis=-1)
    log_g = jnp.log(1.0 - 2.0 ** (-5.0 - jnp.arange(N_HEADS, dtype=F32)))
    idx = jnp.arange(chunk, dtype=F32)
    diff = idx[:, None] - idx[None, :]
    decay = jnp.where(diff >= 0, jnp.exp(jnp.maximum(diff, 0.0)[None] * log_g[:, None, None]), 0.0)
    q_decay = jnp.exp((idx + 1.0)[None, :] * log_g[:, None])
    k_decay = jnp.exp((chunk - 1.0 - idx)[None, :] * log_g[:, None])
    chunk_decay = jnp.exp(chunk * log_g)
    qd = jnp.repeat(q_decay.T, DK, axis=1)
    kd = jnp.repeat(k_decay.T, DK, axis=1)
    return cos2, sin2, decay, qd, kd, chunk_decay


def _inproj_kernel(*refs, tm, chunk, seq_len, has_init):
    if has_init:
        (x_ref, g_ref, w_ref, cos_ref, sin_ref, qd_ref, kd_ref, cw_ref, st_ref,
         a_ref, b_ref, f_ref, cs_ref, h_ref, cx_ref, cb_ref, carry_ref) = refs
    else:
        (x_ref, g_ref, w_ref, cos_ref, sin_ref, qd_ref, kd_ref, cw_ref,
         a_ref, b_ref, f_ref, cs_ref, h_ref, cx_ref, cb_ref, carry_ref) = refs
    i = pl.program_id(0)
    j = pl.program_id(1)
    n_pos = cos_ref.shape[0]

    def proj(t, h=None):
        h = h_ref[...] if h is None else h
        return _dot(h, w_ref[:, t * COL_TILE:(t + 1) * COL_TILE])

    def rotary_heads(acc, dec_ref, post_scale):
        plain, scaled = [], []
        for h in range(N_HEADS):
            cols = slice(h * DK, (h + 1) * DK)
            t = acc[:, cols]
            t3 = t.reshape(tm // n_pos, n_pos, DK)
            r3 = pltpu.roll(t, DK // 2, 1).reshape(tm // n_pos, n_pos, DK)
            r = (t3 * cos_ref[...][None] + r3 * sin_ref[...][None]).reshape(tm, DK)
            if post_scale != 1.0:
                r = r * post_scale
            plain.append(r.astype(BF16))
            d = r.reshape(tm // chunk, chunk, DK) * dec_ref[:, cols][None]
            scaled.append(d.reshape(tm, DK).astype(BF16))
        return jnp.concatenate(plain + scaled, axis=1)

    @pl.when(j == 0)
    def _():
        h = _rms(x_ref[...], g_ref[...]).astype(BF16)
        h_ref[...] = h
        a_ref[...] = rotary_heads(proj(0, h), qd_ref, DK ** -0.5)
        b_ref[...] = proj(1, h).astype(BF16)

    @pl.when(j == 1)
    def _():
        a_ref[...] = rotary_heads(proj(0), kd_ref, 1.0)
        b_ref[...] = proj(1).astype(BF16)

    @pl.when(j == 2)
    def _():
        f_ref[...] = jax.nn.silu(proj(0))
        cx_ref[...] = proj(1)

    @pl.when(j == 3)
    def _():
        f_ref[...] = jax.nn.silu(proj(0))
        cb_ref[...] = proj(1)

    @pl.when(j == 5)
    def _():
        f_ref[...] = jax.nn.sigmoid(proj(0))

    @pl.when(j == 4)
    def _():
        f_ref[...] = jax.nn.sigmoid(proj(1))
        u = proj(0) * cx_ref[...]
        row = lax.broadcasted_iota(jnp.int32, u.shape, 0)
        if has_init:
            n = tm // seq_len
            prev2 = jnp.broadcast_to(st_ref[:, 0:1, :], (n, seq_len, u.shape[1])).reshape(u.shape)
            prev1 = jnp.broadcast_to(st_ref[:, 1:2, :], (n, seq_len, u.shape[1])).reshape(u.shape)
            pos = row % seq_len
        else:
            first = (i % (seq_len // tm)) == 0
            halo = jnp.where(first, 0.0, carry_ref[...])
            prev2 = jnp.broadcast_to(halo[V7X_SUBLANES - 2:V7X_SUBLANES - 1, :], u.shape)
            prev1 = jnp.broadcast_to(halo[V7X_SUBLANES - 1:V7X_SUBLANES, :], u.shape)
            pos = row
        u1 = jnp.where(pos == 0, prev1, pltpu.roll(u, 1, 0))
        u2 = jnp.where(pos == 0, prev2, jnp.where(pos == 1, prev1, pltpu.roll(u, 2, 0)))
        cv = cw_ref[0:1, :] * u2 + cw_ref[1:2, :] * u1 + cw_ref[2:3, :] * u
        b_ref[...] = (cb_ref[...] * cv).astype(BF16)
        if has_init:
            cs_ref[...] = u.reshape(tm // seq_len, seq_len, u.shape[1])[:, seq_len - (CONV_W - 1):, :]
        else:
            carry_ref[...] = u[tm - V7X_SUBLANES:, :]
            cs_ref[0] = u[tm - (CONV_W - 1):, :]


def _inproj(x2d, g, w, cw, conv_state, cos2, sin2, qd, kd, *, n_batch, seq_len, chunk, tm):
    T, D = x2d.shape
    has_init = conv_state is not None
    assert w.shape[1] == len(IN_TILE_ORDER) * COL_TILE and D == COL_TILE
    if has_init:
        assert tm % seq_len == 0
        n = tm // seq_len
        pos_spec = pl.BlockSpec((seq_len, DK), lambda i, j: (0, 0))
        st_specs = [pl.BlockSpec((n, CONV_W - 1, D), lambda i, j: (i, 0, 0))]
        st_args = [conv_state]
        cs_spec = pl.BlockSpec((n, CONV_W - 1, D), lambda i, j: (i, 0, 0))
    else:
        assert seq_len % tm == 0 and tm % chunk == 0
        tiles = seq_len // tm
        pos_spec = pl.BlockSpec((tm, DK), lambda i, j: (i % tiles, 0))
        st_specs, st_args = [], []
        cs_spec = pl.BlockSpec((1, CONV_W - 1, D), lambda i, j: (i // tiles, 0, 0))

    def b_col(j):
        return jnp.where(j == 0, 0, jnp.where(j < 4, 1, 2))

    kernel = functools.partial(_inproj_kernel, tm=tm, chunk=chunk, seq_len=seq_len, has_init=has_init)
    return pl.pallas_call(
        kernel,
        grid=(T // tm, IN_GROUPS),
        in_specs=[pl.BlockSpec((tm, D), lambda i, j: (i, 0)),
                  pl.BlockSpec((1, D), lambda i, j: (0, 0)),
                  pl.BlockSpec((D, IN_GROUP_TILES * COL_TILE), lambda i, j: (0, j)),
                  pos_spec, pos_spec,
                  _resident(qd.shape), _resident(kd.shape), _resident(cw.shape)] + st_specs,
        out_specs=[pl.BlockSpec((tm, 2 * COL_TILE), lambda i, j: (i, jnp.minimum(j, 1))),
                   pl.BlockSpec((tm, COL_TILE), lambda i, j: (i, b_col(j))),
                   pl.BlockSpec((tm, COL_TILE), lambda i, j: (i, jnp.maximum(j - 2, 0))),
                   cs_spec],
        out_shape=[jax.ShapeDtypeStruct((T, 4 * COL_TILE), BF16),
                   jax.ShapeDtypeStruct((T, 3 * COL_TILE), BF16),
                   jax.ShapeDtypeStruct((T, 4 * COL_TILE), F32),
                   jax.ShapeDtypeStruct((n_batch, CONV_W - 1, D), F32)],
        scratch_shapes=[pltpu.VMEM((tm, D), BF16), pltpu.VMEM((tm, D), F32), pltpu.VMEM((tm, D), F32),
                        pltpu.VMEM((V7X_SUBLANES, D), F32)],
        compiler_params=_params(("arbitrary", "arbitrary")),
        name="inproj_init" if has_init else "inproj",
    )(x2d, g, w, cos2, sin2, qd, kd, cw, *st_args)


def _retention_kernel(*refs, chunk, n_chunks, n_seq, has_init, with_mix):
    refs = list(refs)
    q_ref, qd_ref, k_ref, kd_ref, v0_ref, v1_ref, sg_ref, dec_ref, cd_ref = refs[:9]
    del refs[:9]
    s0_ref = refs.pop(0) if has_init else None
    if with_mix:
        x_ref, cbv_ref, sa_ref, sb_ref, wro_ref, wco_ref, wo_ref, x1_ref, s_ref, o_ref = refs
    else:
        o_ref, s_ref = refs
    if not has_init:
        @pl.when(pl.program_id(1) == 0)
        def _():
            s_ref[...] = jnp.zeros_like(s_ref)

    half = N_HEADS // 2
    fuse = chunk % DK == 0

    def v_at(rows, h):
        v_ref = v0_ref if h < half else v1_ref
        return v_ref[rows, pl.ds((h % half) * DV, DV)]

    for seq in range(n_seq):
        row_slices = [pl.ds((seq * n_chunks + c) * chunk, chunk) for c in range(n_chunks)]
        probs, kvs = {}, {}
        for c, rows in enumerate(row_slices):
            for h in range(N_HEADS):
                kcols = pl.ds(h * DK, DK)
                scores = lax.dot_general(q_ref[rows, kcols], k_ref[rows, kcols], (((1,), (1,)), ((), ())),
                                         preferred_element_type=F32) * dec_ref[h]
                probs[c, h] = scores.astype(BF16)
                kdt = jnp.transpose(kd_ref[rows, kcols].astype(F32)).astype(BF16)
                kvs[c, h] = _dot(kdt, v_at(rows, h))
        if has_init:
            states = [s0_ref[seq, h] for h in range(N_HEADS)]
        else:
            states = [s_ref[seq, h] for h in range(N_HEADS)]
        for c, rows in enumerate(row_slices):
            for h in range(N_HEADS):
                kcols = pl.ds(h * DK, DK)
                ocols = pl.ds(h * DV, DV)
                S = states[h]
                if fuse:
                    o = _dot(jnp.concatenate([probs[c, h], qd_ref[rows, kcols]], axis=1),
                             jnp.concatenate([v_at(rows, h), S.astype(BF16)], axis=0))
                else:
                    o = _dot(probs[c, h], v_at(rows, h)) + _dot(qd_ref[rows, kcols], S.astype(BF16))
                states[h] = S * cd_ref[h] + kvs[c, h]
                o = o * lax.rsqrt(jnp.mean(o * o, axis=-1, keepdims=True) + EPS)
                o_ref[rows, ocols] = (o * sg_ref[rows, ocols]).astype(BF16)
        for h in range(N_HEADS):
            s_ref[seq, h] = states[h]

    if with_mix:
        y_conv = _dot(cbv_ref[...], wco_ref[...])
        merged = sa_ref[...] * _dot(o_ref[...], wro_ref[...]) + sb_ref[...] * y_conv
        x1_ref[...] = x_ref[...] + _dot(merged.astype(BF16), wo_ref[...])


def _retention(a, b, f, s0, decay, cd, mix_args, *, n_batch, seq_len, chunk, n_chunks, n_seq):
    T = a.shape[0]
    rows = n_seq * n_chunks * chunk
    steps = seq_len // (n_chunks * chunk) if n_seq == 1 else 1
    has_init = s0 is not None
    with_mix = mix_args is not None

    def tile(width, col):
        return pl.BlockSpec((rows, width), lambda b_, t: (b_ * steps + t, col))

    in_specs = [tile(COL_TILE, 0), tile(COL_TILE, 1), tile(COL_TILE, 2), tile(COL_TILE, 3),
                tile(COL_TILE, 0), tile(COL_TILE, 1),
                tile(N_HEADS * DV, 0),
                _resident(decay.shape), pl.BlockSpec(memory_space=pltpu.SMEM)]
    args = [a, a, a, a, b, b, f, decay, cd]
    state_spec = pl.BlockSpec((n_seq, N_HEADS, DK, DV), lambda b_, t: (b_, 0, 0, 0))
    state_shape = jax.ShapeDtypeStruct((n_batch, N_HEADS, DK, DV), F32)
    if has_init:
        in_specs.append(state_spec)
        args.append(s0)
    if with_mix:
        x2d, wro, wco, wo = mix_args
        D = x2d.shape[1]
        in_specs += [tile(D, 0), tile(D, 2), tile(D, 2), tile(D, 3),
                     _resident(wro.shape), _resident(wco.shape), _resident(wo.shape)]
        args += [x2d, b, f, f, wro, wco, wo]
        out_specs = [tile(D, 0), state_spec]
        out_shape = [jax.ShapeDtypeStruct((T, D), F32), state_shape]
        scratch = [pltpu.VMEM((rows, N_HEADS * DV), BF16)]
    else:
        out_specs = [tile(N_HEADS * DV, 0), state_spec]
        out_shape = [jax.ShapeDtypeStruct((T, N_HEADS * DV), BF16), state_shape]
        scratch = []
    kernel = functools.partial(_retention_kernel, chunk=chunk, n_chunks=n_chunks, n_seq=n_seq,
                               has_init=has_init, with_mix=with_mix)
    return pl.pallas_call(
        kernel,
        grid=(n_batch // n_seq, steps),
        in_specs=in_specs,
        out_specs=out_specs,
        out_shape=out_shape,
        scratch_shapes=scratch,
        compiler_params=_params(("parallel", "arbitrary")),
        name="retention_mix" if with_mix else "retention",
    )(*args)


def _mix_kernel(x_ref, og_ref, cbv_ref, sa_ref, sb_ref, wro_ref, wco_ref, wo_ref, x1_ref):
    y_ret = _dot(og_ref[...], wro_ref[...])
    y_conv = _dot(cbv_ref[...], wco_ref[...])
    merged = sa_ref[...] * y_ret + sb_ref[...] * y_conv
    x1_ref[...] = x_ref[...] + _dot(merged.astype(BF16), wo_ref[...])


def _mix(x2d, og, b, f, wro, wco, wo, *, tm):
    T, D = x2d.shape

    def tile(width, col):
        return pl.BlockSpec((tm, width), lambda i: (i, col))

    return pl.pallas_call(
        _mix_kernel,
        grid=(T // tm,),
        in_specs=[tile(D, 0), tile(N_HEADS * DV, 0), tile(D, 2), tile(D, 2), tile(D, 3),
                  _resident(wro.shape), _resident(wco.shape), _resident(wo.shape)],
        out_specs=tile(D, 0),
        out_shape=jax.ShapeDtypeStruct((T, D), F32),
        compiler_params=_params(("parallel",)),
        name="mix",
    )(x2d, og, b, f, f, wro, wco, wo)


def _ffn_kernel(x_ref, p_ref, gf_ref, wu_ref, wd_ref, gp_ref, wpg_ref, wpp_ref, gl_ref, o_ref,
                *, final_norm):
    x = x_ref[...]
    hf = _rms(x, gf_ref[...]).astype(BF16)
    d_ff = wu_ref.shape[1]
    acc = x
    for c in range(d_ff // COL_TILE):
        cols = pl.ds(c * COL_TILE, COL_TILE)
        hid = jnp.square(jnp.maximum(_dot(hf, wu_ref[:, cols]), 0.0)).astype(BF16)
        acc = acc + _dot(hid, wd_ref[cols, :])
    gate = jax.nn.sigmoid(_dot(_rms(acc, gp_ref[...]).astype(BF16), wpg_ref[...]))
    y = acc + gate * _dot(p_ref[...].astype(BF16), wpp_ref[...])
    if final_norm:
        y = _rms(y, gl_ref[...])
    o_ref[...] = y


def _ffn(x2d, p2d, gf, wu, wd, gp, wpg, wpp, gl, *, tm, final_norm):
    T, D = x2d.shape
    kernel = functools.partial(_ffn_kernel, final_norm=final_norm)
    return pl.pallas_call(
        kernel,
        grid=(T // tm,),
        in_specs=[pl.BlockSpec((tm, D), lambda i: (i, 0)),
                  pl.BlockSpec((tm, p2d.shape[1]), lambda i: (i, 0)),
                  _resident(gf.shape), _resident(wu.shape), _resident(wd.shape),
                  _resident(gp.shape), _resident(wpg.shape), _resident(wpp.shape),
                  _resident(gl.shape)],
        out_specs=pl.BlockSpec((tm, D), lambda i: (i, 0)),
        out_shape=jax.ShapeDtypeStruct((T, D), F32),
        compiler_params=_params(("parallel",)),
        name="ffn",
    )(x2d, p2d, gf, wu, wd, gp, wpg, wpp, gl)


def _layer(x2d, p2d, s_ret, s_conv, pos, w, *, n_batch, seq_len, chunk, n_chunks, n_seq,
           tm_in, tm_mix, tm_ffn, final_norm):
    cos2, sin2, decay, qd, kd, cd = _tables(pos, chunk)
    a, b, f, s_conv_new = _inproj(x2d, w["g_mix"], w["w_in"], w["conv_w"], s_conv, cos2, sin2, qd, kd,
                                  n_batch=n_batch, seq_len=seq_len, chunk=chunk, tm=tm_in)
    ret = functools.partial(_retention, a, b, f, s_ret, decay, cd, n_batch=n_batch, seq_len=seq_len,
                            chunk=chunk, n_chunks=n_chunks, n_seq=n_seq)
    if tm_mix is None:
        x1, s_ret_new = ret((x2d, w["w_ret_out"], w["w_conv_out"], w["w_o"]))
    else:
        og, s_ret_new = ret(None)
        x1 = _mix(x2d, og, b, f, w["w_ret_out"], w["w_conv_out"], w["w_o"], tm=tm_mix)
    y = _ffn(x1, p2d, w["g_ffn"], w["w_up"], w["w_down"], w["g_ple"], w["w_ple_gate"],
             w["w_ple_proj"], w["g_final"], tm=tm_ffn, final_norm=final_norm)
    return y, s_ret_new, s_conv_new


def kernel(x_prompt, x_sample, state_ret, state_conv, p_prompt, p_sample, g_mix, w_in, w_ret_out,
           conv_w, w_conv_out, w_o, g_ffn, w_up, w_down, g_ple, w_ple_gate, w_ple_proj, g_final):
    depth = w_in.shape[0]
    B, L, D = x_prompt.shape
    Bs, Ls, _ = x_sample.shape
    pos_prompt = jnp.arange(L, dtype=F32)
    pos_sample = PAST_LEN + jnp.arange(Ls, dtype=F32)
    chunk_prompt = min(RET_CHUNK, L)

    hp = x_prompt.reshape(B * L, D)
    hs = x_sample.reshape(Bs * Ls, D)
    rp, cp, rs, cs = [], [], [], []
    for i in range(depth):
        w_in_tiles = [w_in[i][:, t * COL_TILE:(t + 1) * COL_TILE] for t in IN_TILE_ORDER]
        w = {"g_mix": g_mix[i][None], "w_in": jnp.concatenate(w_in_tiles, axis=1).astype(BF16),
             "w_ret_out": w_ret_out[i].astype(BF16), "conv_w": conv_w[i],
             "w_conv_out": w_conv_out[i].astype(BF16), "w_o": w_o[i].astype(BF16),
             "g_ffn": g_ffn[i][None], "w_up": w_up[i].astype(BF16), "w_down": w_down[i].astype(BF16),
             "g_ple": g_ple[i][None], "w_ple_gate": w_ple_gate[i].astype(BF16),
             "w_ple_proj": w_ple_proj[i].astype(BF16), "g_final": g_final[None]}
        last = i == depth - 1
        hp, r, c = _layer(hp, p_prompt[i].reshape(B * L, -1), None, None, pos_prompt, w,
                          n_batch=B, seq_len=L, chunk=chunk_prompt, n_chunks=4, n_seq=1,
                          tm_in=1024, tm_mix=None, tm_ffn=1024, final_norm=last)
        rp.append(r)
        cp.append(c)
        hs, r, c = _layer(hs, p_sample[i].reshape(Bs * Ls, -1), state_ret[i], state_conv[i],
                          pos_sample, w, n_batch=Bs, seq_len=Ls, chunk=Ls, n_chunks=1, n_seq=4,
                          tm_in=1024, tm_mix=512, tm_ffn=512, final_norm=last)
        rs.append(r)
        cs.append(c)
    return (hp.reshape(B, L, D), hs.reshape(Bs, Ls, D), jnp.stack(rp), jnp.stack(cp),
            jnp.stack(rs), jnp.stack(cs))
```

```python
import functools

import jax
import jax.numpy as jnp
from jax import lax
from jax.experimental import pallas as pl
from jax.experimental.pallas import tpu as pltpu

F32 = jnp.float32
BF16 = jnp.bfloat16

N_HEADS = 8
DK = 128
DV = 256
CONV_W = 3
RET_CHUNK = 128
PAST_LEN = 16384
ROPE_BASE = 10000.0
EPS = 1e-6

V7X_SUBLANES = 8
V7X_VMEM_BYTES = 64 * 1024 * 1024
V7X_VMEM_LIMIT_BYTES = V7X_VMEM_BYTES - 4 * 1024 * 1024

COL_TILE = 1024

IN_TILE_ORDER = (0, 2, 1, 3, 4, 6, 10, None, 5, 7, 8, 9)
IN_GROUP_TILES = 2
IN_GROUPS = 6
W_SLOTS = 3


def _rms(x, g):
    return x * lax.rsqrt(jnp.mean(x * x, axis=-1, keepdims=True) + EPS) * g


def _dot(a, b):
    return jnp.dot(a, b, preferred_element_type=F32)


def _params(semantics):
    return pltpu.CompilerParams(dimension_semantics=semantics,
                                vmem_limit_bytes=V7X_VMEM_LIMIT_BYTES)


def _resident(shape):
    zeros = (0,) * len(shape)
    return pl.BlockSpec(shape, lambda *_: zeros, pipeline_mode=pl.Buffered(1))


def _tables(pos, chunk):
    inv_freq = ROPE_BASE ** (-jnp.arange(0, DK, 2, dtype=F32) / DK)
    ang = pos.astype(F32)[:, None] * inv_freq[None, :]
    c, s = jnp.cos(ang), jnp.sin(ang)
    cos2 = jnp.concatenate([c, c], axis=-1)
    sin2 = jnp.concatenate([-s, s], axis=-1)
    log_g = jnp.log(1.0 - 2.0 ** (-5.0 - jnp.arange(N_HEADS, dtype=F32)))
    idx = jnp.arange(chunk, dtype=F32)
    diff = idx[:, None] - idx[None, :]
    decay = jnp.where(diff >= 0, jnp.exp(jnp.maximum(diff, 0.0)[None] * log_g[:, None, None]), 0.0)
    q_decay = jnp.exp((idx + 1.0)[None, :] * log_g[:, None])
    k_decay = jnp.exp((chunk - 1.0 - idx)[None, :] * log_g[:, None])
    chunk_decay = jnp.exp(chunk * log_g)
    qd = jnp.repeat(q_decay.T, DK, axis=1)
    kd = jnp.repeat(k_decay.T, DK, axis=1)
    return cos2, sin2, decay, qd, kd, chunk_decay


def _inproj_kernel(*refs, tm, chunk, seq_len, has_init):
    if has_init:
        (x_ref, g_ref, w_hbm, cos_ref, sin_ref, qd_ref, kd_ref, cw_ref, st_ref,
         a_ref, b_ref, f_ref, cs_ref, h_ref, cx_ref, cb_ref, carry_ref, wbuf, wsem) = refs
    else:
        (x_ref, g_ref, w_hbm, cos_ref, sin_ref, qd_ref, kd_ref, cw_ref,
         a_ref, b_ref, f_ref, cs_ref, h_ref, cx_ref, cb_ref, carry_ref, wbuf, wsem) = refs
    i = pl.program_id(0)
    j = pl.program_id(1)
    n_pos = cos_ref.shape[0]

    step = i * IN_GROUPS + j
    n_steps = pl.num_programs(0) * IN_GROUPS
    group_cols = IN_GROUP_TILES * COL_TILE

    def w_copy(group, slot):
        return pltpu.make_async_copy(w_hbm.at[:, pl.ds(group * group_cols, group_cols)],
                                     wbuf.at[slot], wsem.at[slot])

    @pl.when(step == 0)
    def _():
        for s in range(W_SLOTS - 1):
            w_copy(s, s).start()

    for jj in range(IN_GROUPS):
        ahead = (jj + W_SLOTS - 1) % IN_GROUPS

        @pl.when((j == jj) & (step + W_SLOTS - 1 < n_steps))
        def _():
            w_copy(ahead, ahead % W_SLOTS).start()

        @pl.when(j == jj)
        def _():
            w_copy(jj, jj % W_SLOTS).wait()

    def proj(jj, t, h=None):
        h = h_ref[...] if h is None else h
        return _dot(h, wbuf[jj % W_SLOTS, :, t * COL_TILE:(t + 1) * COL_TILE])

    def rotary_heads(acc, dec_ref, post_scale):
        plain, scaled = [], []
        for h in range(N_HEADS):
            cols = slice(h * DK, (h + 1) * DK)
            t = acc[:, cols]
            t3 = t.reshape(tm // n_pos, n_pos, DK)
            r3 = pltpu.roll(t, DK // 2, 1).reshape(tm // n_pos, n_pos, DK)
            r = (t3 * cos_ref[...][None] + r3 * sin_ref[...][None]).reshape(tm, DK)
            if post_scale != 1.0:
                r = r * post_scale
            plain.append(r.astype(BF16))
            d = r.reshape(tm // chunk, chunk, DK) * dec_ref[:, cols][None]
            scaled.append(d.reshape(tm, DK).astype(BF16))
        return jnp.concatenate(plain + scaled, axis=1)

    @pl.when(j == 0)
    def _():
        h = _rms(x_ref[...], g_ref[...]).astype(BF16)
        h_ref[...] = h
        a_ref[...] = rotary_heads(proj(0, 0, h), qd_ref, DK ** -0.5)
        b_ref[...] = proj(0, 1, h).astype(BF16)

    @pl.when(j == 1)
    def _():
        a_ref[...] = rotary_heads(proj(1, 0), kd_ref, 1.0)
        b_ref[...] = proj(1, 1).astype(BF16)

    @pl.when(j == 2)
    def _():
        f_ref[...] = jax.nn.silu(proj(2, 0))
        cx_ref[...] = proj(2, 1)

    @pl.when(j == 3)
    def _():
        f_ref[...] = jax.nn.sigmoid(proj(3, 0))

    @pl.when(j == 4)
    def _():
        f_ref[...] = jax.nn.silu(proj(4, 0))
        cb_ref[...] = proj(4, 1)

    @pl.when(j == 5)
    def _():
        f_ref[...] = jax.nn.sigmoid(proj(5, 1))
        u = proj(5, 0) * cx_ref[...]
        row = lax.broadcasted_iota(jnp.int32, u.shape, 0)
        if has_init:
            n = tm // seq_len
            prev2 = jnp.broadcast_to(st_ref[:, 0:1, :], (n, seq_len, u.shape[1])).reshape(u.shape)
            prev1 = jnp.broadcast_to(st_ref[:, 1:2, :], (n, seq_len, u.shape[1])).reshape(u.shape)
            pos = row % seq_len
        else:
            first = (i % (seq_len // tm)) == 0
            halo = jnp.where(first, 0.0, carry_ref[...])
            prev2 = jnp.broadcast_to(halo[V7X_SUBLANES - 2:V7X_SUBLANES - 1, :], u.shape)
            prev1 = jnp.broadcast_to(halo[V7X_SUBLANES - 1:V7X_SUBLANES, :], u.shape)
            pos = row
        u1 = jnp.where(pos == 0, prev1, pltpu.roll(u, 1, 0))
        u2 = jnp.where(pos == 0, prev2, jnp.where(pos == 1, prev1, pltpu.roll(u, 2, 0)))
        cv = cw_ref[0:1, :] * u2 + cw_ref[1:2, :] * u1 + cw_ref[2:3, :] * u
        b_ref[...] = (cb_ref[...] * cv).astype(BF16)
        if has_init:
            cs_ref[...] = u.reshape(tm // seq_len, seq_len, u.shape[1])[:, seq_len - (CONV_W - 1):, :]
        else:
            carry_ref[...] = u[tm - V7X_SUBLANES:, :]
            cs_ref[0] = u[tm - (CONV_W - 1):, :]


def _inproj(x2d, g, w, cw, conv_state, cos2, sin2, qd, kd, *, n_batch, seq_len, chunk, tm):
    T, D = x2d.shape
    has_init = conv_state is not None
    assert w.shape[1] == len(IN_TILE_ORDER) * COL_TILE and D == COL_TILE
    if has_init:
        assert tm % seq_len == 0
        n = tm // seq_len
        pos_spec = pl.BlockSpec((seq_len, DK), lambda i, j: (0, 0))
        st_specs = [pl.BlockSpec((n, CONV_W - 1, D), lambda i, j: (i, 0, 0))]
        st_args = [conv_state]
        cs_spec = pl.BlockSpec((n, CONV_W - 1, D), lambda i, j: (i, 0, 0))
    else:
        assert seq_len % tm == 0 and tm % chunk == 0
        tiles = seq_len // tm
        pos_spec = pl.BlockSpec((tm, DK), lambda i, j: (i % tiles, 0))
        st_specs, st_args = [], []
        cs_spec = pl.BlockSpec((1, CONV_W - 1, D), lambda i, j: (i // tiles, 0, 0))

    def b_col(j):
        return jnp.where(j == 0, 0, jnp.where(j < 5, 1, 2))

    def f_col(j):
        return jnp.where(j <= 2, 0, jnp.where(j == 3, 3, j - 3))

    kernel = functools.partial(_inproj_kernel, tm=tm, chunk=chunk, seq_len=seq_len, has_init=has_init)
    return pl.pallas_call(
        kernel,
        grid=(T // tm, IN_GROUPS),
        in_specs=[pl.BlockSpec((tm, D), lambda i, j: (i, 0)),
                  pl.BlockSpec((1, D), lambda i, j: (0, 0)),
                  pl.BlockSpec(memory_space=pl.ANY),
                  pos_spec, pos_spec,
                  _resident(qd.shape), _resident(kd.shape), _resident(cw.shape)] + st_specs,
        out_specs=[pl.BlockSpec((tm, 2 * COL_TILE), lambda i, j: (i, jnp.minimum(j, 1))),
                   pl.BlockSpec((tm, COL_TILE), lambda i, j: (i, b_col(j))),
                   pl.BlockSpec((tm, COL_TILE), lambda i, j: (i, f_col(j))),
                   cs_spec],
        out_shape=[jax.ShapeDtypeStruct((T, 4 * COL_TILE), BF16),
                   jax.ShapeDtypeStruct((T, 3 * COL_TILE), BF16),
                   jax.ShapeDtypeStruct((T, 4 * COL_TILE), F32),
                   jax.ShapeDtypeStruct((n_batch, CONV_W - 1, D), F32)],
        scratch_shapes=[pltpu.VMEM((tm, D), BF16), pltpu.VMEM((tm, D), F32), pltpu.VMEM((tm, D), F32),
                        pltpu.VMEM((V7X_SUBLANES, D), F32),
                        pltpu.VMEM((W_SLOTS, D, IN_GROUP_TILES * COL_TILE), BF16),
                        pltpu.SemaphoreType.DMA((W_SLOTS,))],
        compiler_params=_params(("arbitrary", "arbitrary")),
        name="inproj_init" if has_init else "inproj",
    )(x2d, g, w, cos2, sin2, qd, kd, cw, *st_args)


def _retention_kernel(*refs, chunk, n_chunks, n_seq, has_init, with_mix):
    refs = list(refs)
    q_ref, qd_ref, k_ref, kd_ref, v0_ref, v1_ref, sg_ref, dec_ref, cd_ref = refs[:9]
    del refs[:9]
    s0_ref = refs.pop(0) if has_init else None
    if with_mix:
        x_ref, cbv_ref, sa_ref, sb_ref, wro_ref, wco_ref, wo_ref, x1_ref, s_ref, o_ref = refs
    else:
        o_ref, s_ref = refs
    if not has_init:
        @pl.when(pl.program_id(1) == 0)
        def _():
            s_ref[...] = jnp.zeros_like(s_ref)

    half = N_HEADS // 2
    fuse = chunk % DK == 0

    def v_at(rows, h):
        v_ref = v0_ref if h < half else v1_ref
        return v_ref[rows, pl.ds((h % half) * DV, DV)]

    for seq in range(n_seq):
        row_slices = [pl.ds((seq * n_chunks + c) * chunk, chunk) for c in range(n_chunks)]
        probs, kvs = {}, {}
        for c, rows in enumerate(row_slices):
            for h in range(N_HEADS):
                kcols = pl.ds(h * DK, DK)
                scores = lax.dot_general(q_ref[rows, kcols], k_ref[rows, kcols], (((1,), (1,)), ((), ())),
                                         preferred_element_type=F32) * dec_ref[h]
                probs[c, h] = scores.astype(BF16)
                kdt = jnp.transpose(kd_ref[rows, kcols].astype(F32)).astype(BF16)
                kvs[c, h] = _dot(kdt, v_at(rows, h))
        if has_init:
            states = [s0_ref[seq, h] for h in range(N_HEADS)]
        else:
            states = [s_ref[seq, h] for h in range(N_HEADS)]
        for c, rows in enumerate(row_slices):
            for h in range(N_HEADS):
                kcols = pl.ds(h * DK, DK)
                ocols = pl.ds(h * DV, DV)
                S = states[h]
                if fuse:
                    o = _dot(jnp.concatenate([probs[c, h], qd_ref[rows, kcols]], axis=1),
                             jnp.concatenate([v_at(rows, h), S.astype(BF16)], axis=0))
                else:
                    o = _dot(probs[c, h], v_at(rows, h)) + _dot(qd_ref[rows, kcols], S.astype(BF16))
                states[h] = S * cd_ref[h] + kvs[c, h]
                o = o * lax.rsqrt(jnp.mean(o * o, axis=-1, keepdims=True) + EPS)
                o_ref[rows, ocols] = (o * sg_ref[rows, ocols]).astype(BF16)
        for h in range(N_HEADS):
            s_ref[seq, h] = states[h]

    if with_mix:
        y_conv = _dot(cbv_ref[...], wco_ref[...])
        merged = sa_ref[...] * _dot(o_ref[...], wro_ref[...]) + sb_ref[...] * y_conv
        x1_ref[...] = x_ref[...] + _dot(merged.astype(BF16), wo_ref[...])


def _retention(a, b, f, s0, decay, cd, mix_args, *, n_batch, seq_len, chunk, n_chunks, n_seq):
    T = a.shape[0]
    rows = n_seq * n_chunks * chunk
    steps = seq_len // (n_chunks * chunk) if n_seq == 1 else 1
    has_init = s0 is not None
    with_mix = mix_args is not None

    def tile(width, col):
        return pl.BlockSpec((rows, width), lambda b_, t: (b_ * steps + t, col))

    in_specs = [tile(COL_TILE, 0), tile(COL_TILE, 1), tile(COL_TILE, 2), tile(COL_TILE, 3),
                tile(COL_TILE, 0), tile(COL_TILE, 1),
                tile(N_HEADS * DV, 0),
                _resident(decay.shape), pl.BlockSpec(memory_space=pltpu.SMEM)]
    args = [a, a, a, a, b, b, f, decay, cd]
    state_spec = pl.BlockSpec((n_seq, N_HEADS, DK, DV), lambda b_, t: (b_, 0, 0, 0))
    state_shape = jax.ShapeDtypeStruct((n_batch, N_HEADS, DK, DV), F32)
    if has_init:
        in_specs.append(state_spec)
        args.append(s0)
    if with_mix:
        x2d, wro, wco, wo = mix_args
        D = x2d.shape[1]
        in_specs += [tile(D, 0), tile(D, 2), tile(D, 2), tile(D, 3),
                     _resident(wro.shape), _resident(wco.shape), _resident(wo.shape)]
        args += [x2d, b, f, f, wro, wco, wo]
        out_specs = [tile(D, 0), state_spec]
        out_shape = [jax.ShapeDtypeStruct((T, D), F32), state_shape]
        scratch = [pltpu.VMEM((rows, N_HEADS * DV), BF16)]
    else:
        out_specs = [tile(N_HEADS * DV, 0), state_spec]
        out_shape = [jax.ShapeDtypeStruct((T, N_HEADS * DV), BF16), state_shape]
        scratch = []
    kernel = functools.partial(_retention_kernel, chunk=chunk, n_chunks=n_chunks, n_seq=n_seq,
                               has_init=has_init, with_mix=with_mix)
    return pl.pallas_call(
        kernel,
        grid=(n_batch // n_seq, steps),
        in_specs=in_specs,
        out_specs=out_specs,
        out_shape=out_shape,
        scratch_shapes=scratch,
        compiler_params=_params(("parallel", "arbitrary")),
        name="retention_mix" if with_mix else "retention",
    )(*args)


def _mix_kernel(x_ref, og_ref, cbv_ref, sa_ref, sb_ref, wro_ref, wco_ref, wo_ref, x1_ref):
    y_ret = _dot(og_ref[...], wro_ref[...])
    y_conv = _dot(cbv_ref[...], wco_ref[...])
    merged = sa_ref[...] * y_ret + sb_ref[...] * y_conv
    x1_ref[...] = x_ref[...] + _dot(merged.astype(BF16), wo_ref[...])


def _mix(x2d, og, b, f, wro, wco, wo, *, tm):
    T, D = x2d.shape

    def tile(width, col):
        return pl.BlockSpec((tm, width), lambda i: (i, col))

    return pl.pallas_call(
        _mix_kernel,
        grid=(T // tm,),
        in_specs=[tile(D, 0), tile(N_HEADS * DV, 0), tile(D, 2), tile(D, 2), tile(D, 3),
                  _resident(wro.shape), _resident(wco.shape), _resident(wo.shape)],
        out_specs=tile(D, 0),
        out_shape=jax.ShapeDtypeStruct((T, D), F32),
        compiler_params=_params(("parallel",)),
        name="mix",
    )(x2d, og, b, f, f, wro, wco, wo)


def _ffn_kernel(x_ref, p_ref, gf_ref, wu_ref, wd_ref, gp_ref, wpg_ref, wpp_ref, gl_ref, o_ref,
                *, final_norm):
    x = x_ref[...]
    hf = _rms(x, gf_ref[...]).astype(BF16)
    d_ff = wu_ref.shape[1]
    acc = x
    for c in range(d_ff // COL_TILE):
        cols = pl.ds(c * COL_TILE, COL_TILE)
        hid = jnp.square(jnp.maximum(_dot(hf, wu_ref[:, cols]), 0.0)).astype(BF16)
        acc = acc + _dot(hid, wd_ref[cols, :])
    gate = jax.nn.sigmoid(_dot(_rms(acc, gp_ref[...]).astype(BF16), wpg_ref[...]))
    y = acc + gate * _dot(p_ref[...].astype(BF16), wpp_ref[...])
    if final_norm:
        y = _rms(y, gl_ref[...])
    o_ref[...] = y


def _ffn(x2d, p2d, gf, wu, wd, gp, wpg, wpp, gl, *, tm, final_norm):
    T, D = x2d.shape
    kernel = functools.partial(_ffn_kernel, final_norm=final_norm)
    return pl.pallas_call(
        kernel,
        grid=(T // tm,),
        in_specs=[pl.BlockSpec((tm, D), lambda i: (i, 0)),
                  pl.BlockSpec((tm, p2d.shape[1]), lambda i: (i, 0)),
                  _resident(gf.shape), _resident(wu.shape), _resident(wd.shape),
                  _resident(gp.shape), _resident(wpg.shape), _resident(wpp.shape),
                  _resident(gl.shape)],
        out_specs=pl.BlockSpec((tm, D), lambda i: (i, 0)),
        out_shape=jax.ShapeDtypeStruct((T, D), F32),
        compiler_params=_params(("parallel",)),
        name="ffn",
    )(x2d, p2d, gf, wu, wd, gp, wpg, wpp, gl)


def _layer(x2d, p2d, s_ret, s_conv, pos, w, *, n_batch, seq_len, chunk, n_chunks, n_seq,
           tm_in, tm_mix, tm_ffn, final_norm):
    cos2, sin2, decay, qd, kd, cd = _tables(pos, chunk)
    a, b, f, s_conv_new = _inproj(x2d, w["g_mix"], w["w_in"], w["conv_w"], s_conv, cos2, sin2, qd, kd,
                                  n_batch=n_batch, seq_len=seq_len, chunk=chunk, tm=tm_in)
    ret = functools.partial(_retention, a, b, f, s_ret, decay, cd, n_batch=n_batch, seq_len=seq_len,
                            chunk=chunk, n_chunks=n_chunks, n_seq=n_seq)
    if tm_mix is None:
        x1, s_ret_new = ret((x2d, w["w_ret_out"], w["w_conv_out"], w["w_o"]))
    else:
        og, s_ret_new = ret(None)
        x1 = _mix(x2d, og, b, f, w["w_ret_out"], w["w_conv_out"], w["w_o"], tm=tm_mix)
    y = _ffn(x1, p2d, w["g_ffn"], w["w_up"], w["w_down"], w["g_ple"], w["w_ple_gate"],
             w["w_ple_proj"], w["g_final"], tm=tm_ffn, final_norm=final_norm)
    return y, s_ret_new, s_conv_new


def kernel(x_prompt, x_sample, state_ret, state_conv, p_prompt, p_sample, g_mix, w_in, w_ret_out,
           conv_w, w_conv_out, w_o, g_ffn, w_up, w_down, g_ple, w_ple_gate, w_ple_proj, g_final):
    depth = w_in.shape[0]
    B, L, D = x_prompt.shape
    Bs, Ls, _ = x_sample.shape
    pos_prompt = jnp.arange(L, dtype=F32)
    pos_sample = PAST_LEN + jnp.arange(Ls, dtype=F32)
    chunk_prompt = min(RET_CHUNK, L)

    hp = x_prompt.reshape(B * L, D)
    hs = x_sample.reshape(Bs * Ls, D)
    rp, cp, rs, cs = [], [], [], []
    for i in range(depth):
        w_in_tiles = [jnp.zeros((D, COL_TILE), w_in.dtype) if t is None
                      else w_in[i][:, t * COL_TILE:(t + 1) * COL_TILE] for t in IN_TILE_ORDER]
        w = {"g_mix": g_mix[i][None], "w_in": jnp.concatenate(w_in_tiles, axis=1).astype(BF16),
             "w_ret_out": w_ret_out[i].astype(BF16), "conv_w": conv_w[i],
             "w_conv_out": w_conv_out[i].astype(BF16), "w_o": w_o[i].astype(BF16),
             "g_ffn": g_ffn[i][None], "w_up": w_up[i].astype(BF16), "w_down": w_down[i].astype(BF16),
             "g_ple": g_ple[i][None], "w_ple_gate": w_ple_gate[i].astype(BF16),
             "w_ple_proj": w_ple_proj[i].astype(BF16), "g_final": g_final[None]}
        last = i == depth - 1
        hp, r, c = _layer(hp, p_prompt[i].reshape(B * L, -1), None, None, pos_prompt, w,
                          n_batch=B, seq_len=L, chunk=chunk_prompt, n_chunks=4, n_seq=1,
                          tm_in=1024, tm_mix=None, tm_ffn=1024, final_norm=last)
        rp.append(r)
        cp.append(c)
        hs, r, c = _layer(hs, p_sample[i].reshape(Bs * Ls, -1), state_ret[i], state_conv[i],
                          pos_sample, w, n_batch=Bs, seq_len=Ls, chunk=Ls, n_chunks=1, n_seq=4,
                          tm_in=1024, tm_mix=512, tm_ffn=512, final_norm=last)
        rs.append(r)
        cs.append(c)
    return (hp.reshape(B, L, D), hs.reshape(Bs, Ls, D), jnp.stack(rp), jnp.stack(cp),
            jnp.stack(rs), jnp.stack(cs))
```

```python
import functools

import jax
import jax.numpy as jnp
from jax import lax
from jax.experimental import pallas as pl
from jax.experimental.pallas import tpu as pltpu

F32 = jnp.float32
BF16 = jnp.bfloat16

N_HEADS = 8
DK = 128
DV = 256
CONV_W = 3
RET_CHUNK = 128
PAST_LEN = 16384
ROPE_BASE = 10000.0
EPS = 1e-6

V7X_SUBLANES = 8
V7X_VMEM_BYTES = 64 * 1024 * 1024
V7X_VMEM_LIMIT_BYTES = V7X_VMEM_BYTES - 4 * 1024 * 1024

COL_TILE = 1024

IN_TILE_ORDER = (0, 2, 1, 3, 4, 6, 10, None, 5, 7, 8, 9)
IN_GROUP_TILES = 2
IN_GROUPS = 6
W_SLOTS = 3


def _rms(x, g):
    return x * lax.rsqrt(jnp.mean(x * x, axis=-1, keepdims=True) + EPS) * g


def _dot(a, b):
    return jnp.dot(a, b, preferred_element_type=F32)


def _params(semantics):
    return pltpu.CompilerParams(dimension_semantics=semantics,
                                vmem_limit_bytes=V7X_VMEM_LIMIT_BYTES)


def _resident(shape):
    zeros = (0,) * len(shape)
    return pl.BlockSpec(shape, lambda *_: zeros, pipeline_mode=pl.Buffered(1))


def _tables(pos, chunk):
    inv_freq = ROPE_BASE ** (-jnp.arange(0, DK, 2, dtype=F32) / DK)
    ang = pos.astype(F32)[:, None] * inv_freq[None, :]
    c, s = jnp.cos(ang), jnp.sin(ang)
    cos2 = jnp.concatenate([c, c], axis=-1)
    sin2 = jnp.concatenate([-s, s], axis=-1)
    log_g = jnp.log(1.0 - 2.0 ** (-5.0 - jnp.arange(N_HEADS, dtype=F32)))
    idx = jnp.arange(chunk, dtype=F32)
    diff = idx[:, None] - idx[None, :]
    decay = jnp.where(diff >= 0, jnp.exp(jnp.maximum(diff, 0.0)[None] * log_g[:, None, None]), 0.0)
    q_decay = jnp.exp((idx + 1.0)[None, :] * log_g[:, None])
    k_decay = jnp.exp((chunk - 1.0 - idx)[None, :] * log_g[:, None])
    chunk_decay = jnp.exp(chunk * log_g)
    qd = jnp.repeat(q_decay.T, DK, axis=1)
    kd = jnp.repeat(k_decay.T, DK, axis=1)
    return cos2, sin2, decay, qd, kd, chunk_decay


def _inproj_kernel(*refs, tm, chunk, seq_len, has_init):
    if has_init:
        (x_ref, g_ref, w_hbm, cos_ref, sin_ref, qd_ref, kd_ref, cw_ref, st_ref,
         a_ref, b_ref, f_ref, cs_ref, h_ref, cx_ref, cb_ref, carry_ref, wbuf, wsem) = refs
    else:
        (x_ref, g_ref, w_hbm, cos_ref, sin_ref, qd_ref, kd_ref, cw_ref,
         a_ref, b_ref, f_ref, cs_ref, h_ref, cx_ref, cb_ref, carry_ref, wbuf, wsem) = refs
    i = pl.program_id(0)
    j = pl.program_id(1)
    n_pos = cos_ref.shape[0]

    step = i * IN_GROUPS + j
    n_steps = pl.num_programs(0) * IN_GROUPS
    group_cols = IN_GROUP_TILES * COL_TILE

    def w_copy(group, slot):
        return pltpu.make_async_copy(w_hbm.at[:, pl.ds(group * group_cols, group_cols)],
                                     wbuf.at[slot], wsem.at[slot])

    @pl.when(step == 0)
    def _():
        for s in range(W_SLOTS - 1):
            w_copy(s, s).start()

    for jj in range(IN_GROUPS):
        ahead = (jj + W_SLOTS - 1) % IN_GROUPS

        @pl.when((j == jj) & (step + W_SLOTS - 1 < n_steps))
        def _():
            w_copy(ahead, ahead % W_SLOTS).start()

        @pl.when(j == jj)
        def _():
            w_copy(jj, jj % W_SLOTS).wait()

    def proj(jj, t, h=None):
        h = h_ref[...] if h is None else h
        return _dot(h, wbuf[jj % W_SLOTS, :, t * COL_TILE:(t + 1) * COL_TILE])

    def rotary_heads(acc, dec_ref, post_scale):
        plain, scaled = [], []
        for h in range(N_HEADS):
            cols = slice(h * DK, (h + 1) * DK)
            t = acc[:, cols]
            t3 = t.reshape(tm // n_pos, n_pos, DK)
            r3 = pltpu.roll(t, DK // 2, 1).reshape(tm // n_pos, n_pos, DK)
            r = (t3 * cos_ref[...][None] + r3 * sin_ref[...][None]).reshape(tm, DK)
            if post_scale != 1.0:
                r = r * post_scale
            plain.append(r.astype(BF16))
            d = r.reshape(tm // chunk, chunk, DK) * dec_ref[:, cols][None]
            scaled.append(d.reshape(tm, DK).astype(BF16))
        return jnp.concatenate(plain + scaled, axis=1)

    @pl.when(j == 0)
    def _():
        h = _rms(x_ref[...], g_ref[...]).astype(BF16)
        h_ref[...] = h
        a_ref[...] = rotary_heads(proj(0, 0, h), qd_ref, DK ** -0.5)
        b_ref[...] = proj(0, 1, h).astype(BF16)

    @pl.when(j == 1)
    def _():
        a_ref[...] = rotary_heads(proj(1, 0), kd_ref, 1.0)
        b_ref[...] = proj(1, 1).astype(BF16)

    @pl.when(j == 2)
    def _():
        f_ref[...] = jax.nn.silu(proj(2, 0))
        cx_ref[...] = proj(2, 1)

    @pl.when(j == 3)
    def _():
        f_ref[...] = jax.nn.sigmoid(proj(3, 0))

    @pl.when(j == 4)
    def _():
        f_ref[...] = jax.nn.silu(proj(4, 0))
        cb_ref[...] = proj(4, 1)

    @pl.when(j == 5)
    def _():
        f_ref[...] = jax.nn.sigmoid(proj(5, 1))
        u = proj(5, 0) * cx_ref[...]
        row = lax.broadcasted_iota(jnp.int32, u.shape, 0)
        if has_init:
            n = tm // seq_len
            prev2 = jnp.broadcast_to(st_ref[:, 0:1, :], (n, seq_len, u.shape[1])).reshape(u.shape)
            prev1 = jnp.broadcast_to(st_ref[:, 1:2, :], (n, seq_len, u.shape[1])).reshape(u.shape)
            pos = row % seq_len
        else:
            first = (i % (seq_len // tm)) == 0
            halo = jnp.where(first, 0.0, carry_ref[...])
            prev2 = jnp.broadcast_to(halo[V7X_SUBLANES - 2:V7X_SUBLANES - 1, :], u.shape)
            prev1 = jnp.broadcast_to(halo[V7X_SUBLANES - 1:V7X_SUBLANES, :], u.shape)
            pos = row
        u1 = jnp.where(pos == 0, prev1, pltpu.roll(u, 1, 0))
        u2 = jnp.where(pos == 0, prev2, jnp.where(pos == 1, prev1, pltpu.roll(u, 2, 0)))
        cv = cw_ref[0:1, :] * u2 + cw_ref[1:2, :] * u1 + cw_ref[2:3, :] * u
        b_ref[...] = (cb_ref[...] * cv).astype(BF16)
        if has_init:
            cs_ref[...] = u.reshape(tm // seq_len, seq_len, u.shape[1])[:, seq_len - (CONV_W - 1):, :]
        else:
            carry_ref[...] = u[tm - V7X_SUBLANES:, :]
            cs_ref[0] = u[tm - (CONV_W - 1):, :]


def _inproj(x2d, g, w, cw, conv_state, cos2, sin2, qd, kd, *, n_batch, seq_len, chunk, tm):
    T, D = x2d.shape
    has_init = conv_state is not None
    assert w.shape[1] == len(IN_TILE_ORDER) * COL_TILE and D == COL_TILE
    if has_init:
        assert tm % seq_len == 0
        n = tm // seq_len
        pos_spec = pl.BlockSpec((seq_len, DK), lambda i, j: (0, 0))
        st_specs = [pl.BlockSpec((n, CONV_W - 1, D), lambda i, j: (i, 0, 0))]
        st_args = [conv_state]
        cs_spec = pl.BlockSpec((n, CONV_W - 1, D), lambda i, j: (i, 0, 0))
    else:
        assert seq_len % tm == 0 and tm % chunk == 0
        tiles = seq_len // tm
        pos_spec = pl.BlockSpec((tm, DK), lambda i, j: (i % tiles, 0))
        st_specs, st_args = [], []
        cs_spec = pl.BlockSpec((1, CONV_W - 1, D), lambda i, j: (i // tiles, 0, 0))

    def b_col(j):
        return jnp.where(j == 0, 0, jnp.where(j < 5, 1, 2))

    def f_col(j):
        return jnp.where(j <= 2, 0, jnp.where(j == 3, 3, j - 3))

    kernel = functools.partial(_inproj_kernel, tm=tm, chunk=chunk, seq_len=seq_len, has_init=has_init)
    return pl.pallas_call(
        kernel,
        grid=(T // tm, IN_GROUPS),
        in_specs=[pl.BlockSpec((tm, D), lambda i, j: (i, 0)),
                  pl.BlockSpec((1, D), lambda i, j: (0, 0)),
                  pl.BlockSpec(memory_space=pl.ANY),
                  pos_spec, pos_spec,
                  _resident(qd.shape), _resident(kd.shape), _resident(cw.shape)] + st_specs,
        out_specs=[pl.BlockSpec((tm, 2 * COL_TILE), lambda i, j: (i, jnp.minimum(j, 1))),
                   pl.BlockSpec((tm, COL_TILE), lambda i, j: (i, b_col(j))),
                   pl.BlockSpec((tm, COL_TILE), lambda i, j: (i, f_col(j))),
                   cs_spec],
        out_shape=[jax.ShapeDtypeStruct((T, 4 * COL_TILE), BF16),
                   jax.ShapeDtypeStruct((T, 3 * COL_TILE), BF16),
                   jax.ShapeDtypeStruct((T, 4 * COL_TILE), F32),
                   jax.ShapeDtypeStruct((n_batch, CONV_W - 1, D), F32)],
        scratch_shapes=[pltpu.VMEM((tm, D), BF16), pltpu.VMEM((tm, D), F32), pltpu.VMEM((tm, D), F32),
                        pltpu.VMEM((V7X_SUBLANES, D), F32),
                        pltpu.VMEM((W_SLOTS, D, IN_GROUP_TILES * COL_TILE), BF16),
                        pltpu.SemaphoreType.DMA((W_SLOTS,))],
        compiler_params=_params(("arbitrary", "arbitrary")),
        name="inproj_init" if has_init else "inproj",
    )(x2d, g, w, cos2, sin2, qd, kd, cw, *st_args)


N_RET_INPUTS = 9


def _retention_body(q_ref, qd_ref, k_ref, kd_ref, v0_ref, v1_ref, sg_ref, dec_ref, cd_ref,
                    s0_ref, s_ref, o_ref, *, chunk, n_chunks, n_seq):
    has_init = s0_ref is not None
    half = N_HEADS // 2
    fuse = chunk % DK == 0

    def v_at(rows, h):
        v_ref = v0_ref if h < half else v1_ref
        return v_ref[rows, pl.ds((h % half) * DV, DV)]

    for seq in range(n_seq):
        row_slices = [pl.ds((seq * n_chunks + c) * chunk, chunk) for c in range(n_chunks)]
        probs, kvs = {}, {}
        for c, rows in enumerate(row_slices):
            for h in range(N_HEADS):
                kcols = pl.ds(h * DK, DK)
                scores = lax.dot_general(q_ref[rows, kcols], k_ref[rows, kcols], (((1,), (1,)), ((), ())),
                                         preferred_element_type=F32) * dec_ref[h]
                probs[c, h] = scores.astype(BF16)
                kdt = jnp.transpose(kd_ref[rows, kcols].astype(F32)).astype(BF16)
                kvs[c, h] = _dot(kdt, v_at(rows, h))
        if has_init:
            states = [s0_ref[seq, h] for h in range(N_HEADS)]
        else:
            states = [s_ref[seq, h] for h in range(N_HEADS)]
        for c, rows in enumerate(row_slices):
            for h in range(N_HEADS):
                kcols = pl.ds(h * DK, DK)
                ocols = pl.ds(h * DV, DV)
                S = states[h]
                if fuse:
                    o = _dot(jnp.concatenate([probs[c, h], qd_ref[rows, kcols]], axis=1),
                             jnp.concatenate([v_at(rows, h), S.astype(BF16)], axis=0))
                else:
                    o = _dot(probs[c, h], v_at(rows, h)) + _dot(qd_ref[rows, kcols], S.astype(BF16))
                states[h] = S * cd_ref[h] + kvs[c, h]
                o = o * lax.rsqrt(jnp.mean(o * o, axis=-1, keepdims=True) + EPS)
                o_ref[rows, ocols] = (o * sg_ref[rows, ocols]).astype(BF16)
        for h in range(N_HEADS):
            s_ref[seq, h] = states[h]


def _retention_specs(a, b, f, decay, cd, rows, row_block):
    def tile(width, col):
        return pl.BlockSpec((rows, width), lambda *g: (row_block(*g), col))

    specs = [tile(COL_TILE, 0), tile(COL_TILE, 1), tile(COL_TILE, 2), tile(COL_TILE, 3),
             tile(COL_TILE, 0), tile(COL_TILE, 1),
             tile(N_HEADS * DV, 0),
             _resident(decay.shape), pl.BlockSpec(memory_space=pltpu.SMEM)]
    return specs, [a, a, a, a, b, b, f, decay, cd]


def _retention_mix_kernel(*refs, chunk, n_chunks):
    ret_refs = refs[:N_RET_INPUTS]
    x_ref, cbv_ref, sa_ref, sb_ref, wro_ref, wco_ref, wo_ref, x1_ref, s_ref, o_ref = refs[N_RET_INPUTS:]

    @pl.when(pl.program_id(1) == 0)
    def _():
        s_ref[...] = jnp.zeros_like(s_ref)

    _retention_body(*ret_refs, None, s_ref, o_ref, chunk=chunk, n_chunks=n_chunks, n_seq=1)
    y_conv = _dot(cbv_ref[...], wco_ref[...])
    merged = sa_ref[...] * _dot(o_ref[...], wro_ref[...]) + sb_ref[...] * y_conv
    x1_ref[...] = x_ref[...] + _dot(merged.astype(BF16), wo_ref[...])


def _retention_mix(a, b, f, decay, cd, x2d, wro, wco, wo, *, n_batch, seq_len, chunk, n_chunks):
    T, D = x2d.shape
    rows = n_chunks * chunk
    steps = seq_len // rows

    def row_block(b_, t):
        return b_ * steps + t

    def tile(width, col):
        return pl.BlockSpec((rows, width), lambda b_, t: (row_block(b_, t), col))

    in_specs, args = _retention_specs(a, b, f, decay, cd, rows, row_block)
    in_specs += [tile(D, 0), tile(D, 2), tile(D, 2), tile(D, 3),
                 _resident(wro.shape), _resident(wco.shape), _resident(wo.shape)]
    args += [x2d, b, f, f, wro, wco, wo]
    kernel = functools.partial(_retention_mix_kernel, chunk=chunk, n_chunks=n_chunks)
    return pl.pallas_call(
        kernel,
        grid=(n_batch, steps),
        in_specs=in_specs,
        out_specs=[tile(D, 0), pl.BlockSpec((1, N_HEADS, DK, DV), lambda b_, t: (b_, 0, 0, 0))],
        out_shape=[jax.ShapeDtypeStruct((T, D), F32),
                   jax.ShapeDtypeStruct((n_batch, N_HEADS, DK, DV), F32)],
        scratch_shapes=[pltpu.VMEM((rows, N_HEADS * DV), BF16)],
        compiler_params=_params(("parallel", "arbitrary")),
        name="retention_mix",
    )(*args)


def _mix_kernel(x_ref, og_ref, cbv_ref, sa_ref, sb_ref, wro_ref, wco_ref, wo_ref, x1_ref):
    y_ret = _dot(og_ref[...], wro_ref[...])
    y_conv = _dot(cbv_ref[...], wco_ref[...])
    merged = sa_ref[...] * y_ret + sb_ref[...] * y_conv
    x1_ref[...] = x_ref[...] + _dot(merged.astype(BF16), wo_ref[...])


def _mix(x2d, og, b, f, wro, wco, wo, *, tm):
    T, D = x2d.shape

    def tile(width, col):
        return pl.BlockSpec((tm, width), lambda i: (i, col))

    return pl.pallas_call(
        _mix_kernel,
        grid=(T // tm,),
        in_specs=[tile(D, 0), tile(N_HEADS * DV, 0), tile(D, 2), tile(D, 2), tile(D, 3),
                  _resident(wro.shape), _resident(wco.shape), _resident(wo.shape)],
        out_specs=tile(D, 0),
        out_shape=jax.ShapeDtypeStruct((T, D), F32),
        compiler_params=_params(("parallel",)),
        name="mix",
    )(x2d, og, b, f, f, wro, wco, wo)


N_FFN_INPUTS = 9


def _ffn_kernel(*refs, final_norm, side):
    x_ref, p_ref, gf_ref, wu_ref, wd_ref, gp_ref, wpg_ref, wpp_ref, gl_ref = refs[:N_FFN_INPUTS]
    if side is None:
        o_ref, = refs[N_FFN_INPUTS:]
    else:
        ret_refs = refs[N_FFN_INPUTS:N_FFN_INPUTS + N_RET_INPUTS]
        s0_ref, o_ref, og_ref, s_ref = refs[N_FFN_INPUTS + N_RET_INPUTS:]
        chunk, n_seq = side
        _retention_body(*ret_refs, s0_ref, s_ref, og_ref, chunk=chunk, n_chunks=1, n_seq=n_seq)
    x = x_ref[...]
    hf = _rms(x, gf_ref[...]).astype(BF16)
    d_ff = wu_ref.shape[1]
    acc = x
    for c in range(d_ff // COL_TILE):
        cols = pl.ds(c * COL_TILE, COL_TILE)
        hid = jnp.square(jnp.maximum(_dot(hf, wu_ref[:, cols]), 0.0)).astype(BF16)
        acc = acc + _dot(hid, wd_ref[cols, :])
    gate = jax.nn.sigmoid(_dot(_rms(acc, gp_ref[...]).astype(BF16), wpg_ref[...]))
    y = acc + gate * _dot(p_ref[...].astype(BF16), wpp_ref[...])
    if final_norm:
        y = _rms(y, gl_ref[...])
    o_ref[...] = y


def _ffn(x2d, p2d, gf, wu, wd, gp, wpg, wpp, gl, *, tm, final_norm, side=None):
    T, D = x2d.shape
    steps = T // tm
    in_specs = [pl.BlockSpec((tm, D), lambda i: (i, 0)),
                pl.BlockSpec((tm, p2d.shape[1]), lambda i: (i, 0)),
                _resident(gf.shape), _resident(wu.shape), _resident(wd.shape),
                _resident(gp.shape), _resident(wpg.shape), _resident(wpp.shape),
                _resident(gl.shape)]
    args = [x2d, p2d, gf, wu, wd, gp, wpg, wpp, gl]
    out_specs = [pl.BlockSpec((tm, D), lambda i: (i, 0))]
    out_shape = [jax.ShapeDtypeStruct((T, D), F32)]
    kernel_side = None
    if side is not None:
        a, b, f, s0, decay, cd, chunk = side
        n_all = s0.shape[0]
        assert n_all % steps == 0 and a.shape[0] == n_all * chunk
        n_seq = n_all // steps
        ret_specs, ret_args = _retention_specs(a, b, f, decay, cd, n_seq * chunk, lambda i: i)
        state_spec = pl.BlockSpec((n_seq,) + s0.shape[1:], lambda i: (i, 0, 0, 0))
        in_specs += ret_specs + [state_spec]
        args += ret_args + [s0]
        out_specs += [pl.BlockSpec((n_seq * chunk, N_HEADS * DV), lambda i: (i, 0)), state_spec]
        out_shape += [jax.ShapeDtypeStruct((a.shape[0], N_HEADS * DV), BF16),
                      jax.ShapeDtypeStruct(s0.shape, F32)]
        kernel_side = (chunk, n_seq)
    kernel = functools.partial(_ffn_kernel, final_norm=final_norm, side=kernel_side)
    out = pl.pallas_call(
        kernel,
        grid=(steps,),
        in_specs=in_specs,
        out_specs=out_specs,
        out_shape=out_shape,
        compiler_params=_params(("parallel",)),
        name="ffn" if side is None else "ffn_retention",
    )(*args)
    return out[0] if side is None else out


def _layer(xp, xs, pp, ps, s_ret, s_conv, pos_p, pos_s, w, *, n_prompt, len_prompt, chunk_prompt,
           n_sample, len_sample, final_norm):
    ffn_w = (w["g_ffn"], w["w_up"], w["w_down"], w["g_ple"], w["w_ple_gate"], w["w_ple_proj"],
             w["g_final"])
    mix_w = (w["w_ret_out"], w["w_conv_out"], w["w_o"])
    cos_p, sin_p, decay_p, qd_p, kd_p, cd_p = _tables(pos_p, chunk_prompt)
    cos_s, sin_s, decay_s, qd_s, kd_s, cd_s = _tables(pos_s, len_sample)
    ap, bp, fp, conv_p = _inproj(xp, w["g_mix"], w["w_in"], w["conv_w"], None, cos_p, sin_p, qd_p, kd_p,
                                 n_batch=n_prompt, seq_len=len_prompt, chunk=chunk_prompt, tm=1024)
    as_, bs, fs, conv_s = _inproj(xs, w["g_mix"], w["w_in"], w["conv_w"], s_conv, cos_s, sin_s, qd_s, kd_s,
                                  n_batch=n_sample, seq_len=len_sample, chunk=len_sample, tm=512)
    x1p, ret_p = _retention_mix(ap, bp, fp, decay_p, cd_p, xp, *mix_w, n_batch=n_prompt,
                                seq_len=len_prompt, chunk=chunk_prompt, n_chunks=4)
    yp, og_s, ret_s = _ffn(x1p, pp, *ffn_w, tm=512, final_norm=final_norm,
                           side=(as_, bs, fs, s_ret, decay_s, cd_s, len_sample))
    x1s = _mix(xs, og_s, bs, fs, *mix_w, tm=512)
    ys = _ffn(x1s, ps, *ffn_w, tm=512, final_norm=final_norm)
    return yp, ys, ret_p, conv_p, ret_s, conv_s


def kernel(x_prompt, x_sample, state_ret, state_conv, p_prompt, p_sample, g_mix, w_in, w_ret_out,
           conv_w, w_conv_out, w_o, g_ffn, w_up, w_down, g_ple, w_ple_gate, w_ple_proj, g_final):
    depth = w_in.shape[0]
    B, L, D = x_prompt.shape
    Bs, Ls, _ = x_sample.shape
    pos_prompt = jnp.arange(L, dtype=F32)
    pos_sample = PAST_LEN + jnp.arange(Ls, dtype=F32)
    chunk_prompt = min(RET_CHUNK, L)

    hp = x_prompt.reshape(B * L, D)
    hs = x_sample.reshape(Bs * Ls, D)
    rp, cp, rs, cs = [], [], [], []
    for i in range(depth):
        w_in_tiles = [jnp.zeros((D, COL_TILE), w_in.dtype) if t is None
                      else w_in[i][:, t * COL_TILE:(t + 1) * COL_TILE] for t in IN_TILE_ORDER]
        w = {"g_mix": g_mix[i][None], "w_in": jnp.concatenate(w_in_tiles, axis=1).astype(BF16),
             "w_ret_out": w_ret_out[i].astype(BF16), "conv_w": conv_w[i],
             "w_conv_out": w_conv_out[i].astype(BF16), "w_o": w_o[i].astype(BF16),
             "g_ffn": g_ffn[i][None], "w_up": w_up[i].astype(BF16), "w_down": w_down[i].astype(BF16),
             "g_ple": g_ple[i][None], "w_ple_gate": w_ple_gate[i].astype(BF16),
             "w_ple_proj": w_ple_proj[i].astype(BF16), "g_final": g_final[None]}
        hp, hs, r_p, c_p, r_s, c_s = _layer(
            hp, hs, p_prompt[i].reshape(B * L, -1), p_sample[i].reshape(Bs * Ls, -1),
            state_ret[i], state_conv[i], pos_prompt, pos_sample, w,
            n_prompt=B, len_prompt=L, chunk_prompt=chunk_prompt, n_sample=Bs, len_sample=Ls,
            final_norm=i == depth - 1)
        rp.append(r_p)
        cp.append(c_p)
        rs.append(r_s)
        cs.append(c_s)
    return (hp.reshape(B, L, D), hs.reshape(Bs, Ls, D), jnp.stack(rp), jnp.stack(cp),
            jnp.stack(rs), jnp.stack(cs))
```

```python
import functools

import jax
import jax.numpy as jnp
from jax import lax
from jax.experimental import pallas as pl
from jax.experimental.pallas import tpu as pltpu

F32 = jnp.float32
BF16 = jnp.bfloat16

N_HEADS = 8
DK = 128
DV = 256
CONV_W = 3
RET_CHUNK = 128
PAST_LEN = 16384
ROPE_BASE = 10000.0
EPS = 1e-6

V7X_SUBLANES = 8
V7X_VMEM_BYTES = 64 * 1024 * 1024
V7X_VMEM_LIMIT_BYTES = V7X_VMEM_BYTES - 4 * 1024 * 1024

COL_TILE = 1024

IN_GROUP_TILES = ((0, 2), (1, 3), (4, 6), (10,), (5, 7), (8, 9))
IN_GROUPS = len(IN_GROUP_TILES)
IN_TILES = 11
W_SLOTS = 3


def _rms(x, g):
    return x * lax.rsqrt(jnp.mean(x * x, axis=-1, keepdims=True) + EPS) * g


def _dot(a, b):
    return jnp.dot(a, b, preferred_element_type=F32)


def _params(semantics):
    return pltpu.CompilerParams(dimension_semantics=semantics,
                                vmem_limit_bytes=V7X_VMEM_LIMIT_BYTES)


def _resident(shape):
    zeros = (0,) * len(shape)
    return pl.BlockSpec(shape, lambda *_: zeros, pipeline_mode=pl.Buffered(1))


def _tables(pos, chunk):
    inv_freq = ROPE_BASE ** (-jnp.arange(0, DK, 2, dtype=F32) / DK)
    ang = pos.astype(F32)[:, None] * inv_freq[None, :]
    c, s = jnp.cos(ang), jnp.sin(ang)
    cos2 = jnp.concatenate([c, c], axis=-1)
    sin2 = jnp.concatenate([-s, s], axis=-1)
    log_g = jnp.log(1.0 - 2.0 ** (-5.0 - jnp.arange(N_HEADS, dtype=F32)))
    idx = jnp.arange(chunk, dtype=F32)
    diff = idx[:, None] - idx[None, :]
    decay = jnp.where(diff >= 0, jnp.exp(jnp.maximum(diff, 0.0)[None] * log_g[:, None, None]), 0.0)
    q_decay = jnp.exp((idx + 1.0)[None, :] * log_g[:, None])
    k_decay = jnp.exp((chunk - 1.0 - idx)[None, :] * log_g[:, None])
    chunk_decay = jnp.exp(chunk * log_g)
    qd = jnp.repeat(q_decay.T, DK, axis=1)
    kd = jnp.repeat(k_decay.T, DK, axis=1)
    return cos2, sin2, decay, qd, kd, chunk_decay


def _inproj_kernel(*refs, tm, chunk, seq_len, has_init):
    if has_init:
        (x_ref, g_ref, w_hbm, cos_ref, sin_ref, qd_ref, kd_ref, cw_ref, st_ref,
         a_ref, b_ref, f_ref, cs_ref, h_ref, cx_ref, cb_ref, carry_ref, wbuf, wsem) = refs
    else:
        (x_ref, g_ref, w_hbm, cos_ref, sin_ref, qd_ref, kd_ref, cw_ref,
         a_ref, b_ref, f_ref, cs_ref, h_ref, cx_ref, cb_ref, carry_ref, wbuf, wsem) = refs
    i = pl.program_id(0)
    j = pl.program_id(1)
    n_pos = cos_ref.shape[0]

    step = i * IN_GROUPS + j
    n_steps = pl.num_programs(0) * IN_GROUPS

    def w_copies(group):
        slot = group % W_SLOTS
        return [pltpu.make_async_copy(w_hbm.at[:, pl.ds(tile * COL_TILE, COL_TILE)],
                                      wbuf.at[slot, :, pl.ds(t * COL_TILE, COL_TILE)],
                                      wsem.at[slot, t])
                for t, tile in enumerate(IN_GROUP_TILES[group])]

    @pl.when(step == 0)
    def _():
        for group in range(W_SLOTS - 1):
            for copy in w_copies(group):
                copy.start()

    for jj in range(IN_GROUPS):
        @pl.when((j == jj) & (step + W_SLOTS - 1 < n_steps))
        def _():
            for copy in w_copies((jj + W_SLOTS - 1) % IN_GROUPS):
                copy.start()

        @pl.when(j == jj)
        def _():
            for copy in w_copies(jj):
                copy.wait()

    def proj(jj, t, h=None):
        h = h_ref[...] if h is None else h
        return _dot(h, wbuf[jj % W_SLOTS, :, t * COL_TILE:(t + 1) * COL_TILE])

    def rotary_heads(acc, dec_ref, post_scale):
        plain, scaled = [], []
        for h in range(N_HEADS):
            cols = slice(h * DK, (h + 1) * DK)
            t = acc[:, cols]
            t3 = t.reshape(tm // n_pos, n_pos, DK)
            r3 = pltpu.roll(t, DK // 2, 1).reshape(tm // n_pos, n_pos, DK)
            r = (t3 * cos_ref[...][None] + r3 * sin_ref[...][None]).reshape(tm, DK)
            if post_scale != 1.0:
                r = r * post_scale
            plain.append(r.astype(BF16))
            d = r.reshape(tm // chunk, chunk, DK) * dec_ref[:, cols][None]
            scaled.append(d.reshape(tm, DK).astype(BF16))
        return jnp.concatenate(plain + scaled, axis=1)

    @pl.when(j == 0)
    def _():
        h = _rms(x_ref[...], g_ref[...]).astype(BF16)
        h_ref[...] = h
        a_ref[...] = rotary_heads(proj(0, 0, h), qd_ref, DK ** -0.5)
        b_ref[...] = proj(0, 1, h).astype(BF16)

    @pl.when(j == 1)
    def _():
        a_ref[...] = rotary_heads(proj(1, 0), kd_ref, 1.0)
        b_ref[...] = proj(1, 1).astype(BF16)

    @pl.when(j == 2)
    def _():
        f_ref[...] = jax.nn.silu(proj(2, 0))
        cx_ref[...] = proj(2, 1)

    @pl.when(j == 3)
    def _():
        f_ref[...] = jax.nn.sigmoid(proj(3, 0))

    @pl.when(j == 4)
    def _():
        f_ref[...] = jax.nn.silu(proj(4, 0))
        cb_ref[...] = proj(4, 1)

    @pl.when(j == 5)
    def _():
        f_ref[...] = jax.nn.sigmoid(proj(5, 1))
        u = proj(5, 0) * cx_ref[...]
        def patch(pos, prev2, prev1, r1, r2):
            return (jnp.where(pos == 0, prev1, r1),
                    jnp.where(pos == 0, prev2, jnp.where(pos == 1, prev1, r2)))

        r1, r2 = pltpu.roll(u, 1, 0), pltpu.roll(u, 2, 0)
        if has_init:
            n = tm // seq_len
            prev2 = jnp.broadcast_to(st_ref[:, 0:1, :], (n, seq_len, u.shape[1])).reshape(u.shape)
            prev1 = jnp.broadcast_to(st_ref[:, 1:2, :], (n, seq_len, u.shape[1])).reshape(u.shape)
            u1, u2 = patch(lax.broadcasted_iota(jnp.int32, u.shape, 0) % seq_len, prev2, prev1, r1, r2)
        else:
            s = V7X_SUBLANES
            first = (i % (seq_len // tm)) == 0
            halo = jnp.where(first, 0.0, carry_ref[...])
            prev2 = jnp.broadcast_to(halo[s - 2:s - 1, :], halo.shape)
            prev1 = jnp.broadcast_to(halo[s - 1:s, :], halo.shape)
            h1, h2 = patch(lax.broadcasted_iota(jnp.int32, halo.shape, 0), prev2, prev1, r1[:s], r2[:s])
            u1 = jnp.concatenate([h1, r1[s:]], axis=0)
            u2 = jnp.concatenate([h2, r2[s:]], axis=0)
        cv = cw_ref[0:1, :] * u2 + cw_ref[1:2, :] * u1 + cw_ref[2:3, :] * u
        b_ref[...] = (cb_ref[...] * cv).astype(BF16)
        if has_init:
            cs_ref[...] = u.reshape(tm // seq_len, seq_len, u.shape[1])[:, seq_len - (CONV_W - 1):, :]
        else:
            carry_ref[...] = u[tm - V7X_SUBLANES:, :]
            cs_ref[0] = u[tm - (CONV_W - 1):, :]


def _inproj(x2d, g, w, cw, conv_state, cos2, sin2, qd, kd, *, n_batch, seq_len, chunk, tm):
    T, D = x2d.shape
    has_init = conv_state is not None
    assert w.shape[1] == IN_TILES * COL_TILE and D == COL_TILE and IN_GROUPS % W_SLOTS == 0
    if has_init:
        assert tm % seq_len == 0
        n = tm // seq_len
        pos_spec = pl.BlockSpec((seq_len, DK), lambda i, j: (0, 0))
        st_specs = [pl.BlockSpec((n, CONV_W - 1, D), lambda i, j: (i, 0, 0))]
        st_args = [conv_state]
        cs_spec = pl.BlockSpec((n, CONV_W - 1, D), lambda i, j: (i, 0, 0))
    else:
        assert seq_len % tm == 0 and tm % chunk == 0
        tiles = seq_len // tm
        pos_spec = pl.BlockSpec((tm, DK), lambda i, j: (i % tiles, 0))
        st_specs, st_args = [], []
        cs_spec = pl.BlockSpec((1, CONV_W - 1, D), lambda i, j: (i // tiles, 0, 0))

    def b_col(j):
        return jnp.where(j == 0, 0, jnp.where(j < 5, 1, 2))

    def f_col(j):
        return jnp.where(j <= 2, 0, jnp.where(j == 3, 3, j - 3))

    kernel = functools.partial(_inproj_kernel, tm=tm, chunk=chunk, seq_len=seq_len, has_init=has_init)
    return pl.pallas_call(
        kernel,
        grid=(T // tm, IN_GROUPS),
        in_specs=[pl.BlockSpec((tm, D), lambda i, j: (i, 0)),
                  pl.BlockSpec((1, D), lambda i, j: (0, 0)),
                  pl.BlockSpec(memory_space=pl.ANY),
                  pos_spec, pos_spec,
                  _resident(qd.shape), _resident(kd.shape), _resident(cw.shape)] + st_specs,
        out_specs=[pl.BlockSpec((tm, 2 * COL_TILE), lambda i, j: (i, jnp.minimum(j, 1))),
                   pl.BlockSpec((tm, COL_TILE), lambda i, j: (i, b_col(j))),
                   pl.BlockSpec((tm, COL_TILE), lambda i, j: (i, f_col(j))),
                   cs_spec],
        out_shape=[jax.ShapeDtypeStruct((T, 4 * COL_TILE), BF16),
                   jax.ShapeDtypeStruct((T, 3 * COL_TILE), BF16),
                   jax.ShapeDtypeStruct((T, 4 * COL_TILE), F32),
                   jax.ShapeDtypeStruct((n_batch, CONV_W - 1, D), F32)],
        scratch_shapes=[pltpu.VMEM((tm, D), BF16), pltpu.VMEM((tm, D), F32), pltpu.VMEM((tm, D), F32),
                        pltpu.VMEM((V7X_SUBLANES, D), F32),
                        pltpu.VMEM((W_SLOTS, D, 2 * COL_TILE), BF16),
                        pltpu.SemaphoreType.DMA((W_SLOTS, 2))],
        compiler_params=_params(("arbitrary", "arbitrary")),
        name="inproj_init" if has_init else "inproj",
    )(x2d, g, w, cos2, sin2, qd, kd, cw, *st_args)


N_RET_INPUTS = 9


def _retention_body(q_ref, qd_ref, k_ref, kd_ref, v0_ref, v1_ref, sg_ref, dec_ref, cd_ref,
                    s0_ref, s_ref, o_ref, *, chunk, n_chunks, seqs):
    has_init = s0_ref is not None
    half = N_HEADS // 2
    fuse = chunk % DK == 0

    def v_at(rows, h):
        v_ref = v0_ref if h < half else v1_ref
        return v_ref[rows, pl.ds((h % half) * DV, DV)]

    for seq in seqs:
        row_slices = [pl.ds((seq * n_chunks + c) * chunk, chunk) for c in range(n_chunks)]
        probs, kvs = {}, {}
        for c, rows in enumerate(row_slices):
            for h in range(N_HEADS):
                kcols = pl.ds(h * DK, DK)
                scores = lax.dot_general(q_ref[rows, kcols], k_ref[rows, kcols], (((1,), (1,)), ((), ())),
                                         preferred_element_type=F32) * dec_ref[h]
                probs[c, h] = scores.astype(BF16)
                kdt = jnp.transpose(kd_ref[rows, kcols].astype(F32)).astype(BF16)
                kvs[c, h] = _dot(kdt, v_at(rows, h))
        if has_init:
            states = [s0_ref[seq, h] for h in range(N_HEADS)]
        else:
            states = [s_ref[seq, h] for h in range(N_HEADS)]
        for c, rows in enumerate(row_slices):
            for h in range(N_HEADS):
                kcols = pl.ds(h * DK, DK)
                ocols = pl.ds(h * DV, DV)
                S = states[h]
                if fuse:
                    o = _dot(jnp.concatenate([probs[c, h], qd_ref[rows, kcols]], axis=1),
                             jnp.concatenate([v_at(rows, h), S.astype(BF16)], axis=0))
                else:
                    o = _dot(probs[c, h], v_at(rows, h)) + _dot(qd_ref[rows, kcols], S.astype(BF16))
                states[h] = S * cd_ref[h] + kvs[c, h]
                o = o * lax.rsqrt(jnp.mean(o * o, axis=-1, keepdims=True) + EPS)
                o_ref[rows, ocols] = (o * sg_ref[rows, ocols]).astype(BF16)
        for h in range(N_HEADS):
            s_ref[seq, h] = states[h]


def _retention_specs(a, b, f, decay, cd, rows, row_block):
    def tile(width, col):
        return pl.BlockSpec((rows, width), lambda *g: (row_block(*g), col))

    specs = [tile(COL_TILE, 0), tile(COL_TILE, 1), tile(COL_TILE, 2), tile(COL_TILE, 3),
             tile(COL_TILE, 0), tile(COL_TILE, 1),
             tile(N_HEADS * DV, 0),
             _resident(decay.shape), pl.BlockSpec(memory_space=pltpu.SMEM)]
    return specs, [a, a, a, a, b, b, f, decay, cd]


def _retention_mix_kernel(*refs, chunk, n_chunks):
    ret_refs = refs[:N_RET_INPUTS]
    x_ref, cbv_ref, sa_ref, sb_ref, wro_ref, wco_ref, wo_ref, x1_ref, s_ref, o_ref = refs[N_RET_INPUTS:]

    @pl.when(pl.program_id(1) == 0)
    def _():
        s_ref[...] = jnp.zeros_like(s_ref)

    _retention_body(*ret_refs, None, s_ref, o_ref, chunk=chunk, n_chunks=n_chunks, seqs=range(1))
    y_conv = _dot(cbv_ref[...], wco_ref[...])
    merged = sa_ref[...] * _dot(o_ref[...], wro_ref[...]) + sb_ref[...] * y_conv
    x1_ref[...] = x_ref[...] + _dot(merged.astype(BF16), wo_ref[...])


def _retention_mix(a, b, f, decay, cd, x2d, wro, wco, wo, *, n_batch, seq_len, chunk, n_chunks):
    T, D = x2d.shape
    rows = n_chunks * chunk
    steps = seq_len // rows

    def row_block(b_, t):
        return b_ * steps + t

    def tile(width, col):
        return pl.BlockSpec((rows, width), lambda b_, t: (row_block(b_, t), col))

    in_specs, args = _retention_specs(a, b, f, decay, cd, rows, row_block)
    in_specs += [tile(D, 0), tile(D, 2), tile(D, 2), tile(D, 3),
                 _resident(wro.shape), _resident(wco.shape), _resident(wo.shape)]
    args += [x2d, b, f, f, wro, wco, wo]
    kernel = functools.partial(_retention_mix_kernel, chunk=chunk, n_chunks=n_chunks)
    return pl.pallas_call(
        kernel,
        grid=(n_batch, steps),
        in_specs=in_specs,
        out_specs=[tile(D, 0), pl.BlockSpec((1, N_HEADS, DK, DV), lambda b_, t: (b_, 0, 0, 0))],
        out_shape=[jax.ShapeDtypeStruct((T, D), F32),
                   jax.ShapeDtypeStruct((n_batch, N_HEADS, DK, DV), F32)],
        scratch_shapes=[pltpu.VMEM((rows, N_HEADS * DV), BF16)],
        compiler_params=_params(("parallel", "arbitrary")),
        name="retention_mix",
    )(*args)


def _mix_kernel(x_ref, og_ref, cbv_ref, sa_ref, sb_ref, wro_ref, wco_ref, wo_ref, x1_ref):
    y_ret = _dot(og_ref[...], wro_ref[...])
    y_conv = _dot(cbv_ref[...], wco_ref[...])
    merged = sa_ref[...] * y_ret + sb_ref[...] * y_conv
    x1_ref[...] = x_ref[...] + _dot(merged.astype(BF16), wo_ref[...])


def _mix(x2d, og, b, f, wro, wco, wo, *, tm):
    T, D = x2d.shape

    def tile(width, col):
        return pl.BlockSpec((tm, width), lambda i: (i, col))

    return pl.pallas_call(
        _mix_kernel,
        grid=(T // tm,),
        in_specs=[tile(D, 0), tile(N_HEADS * DV, 0), tile(D, 2), tile(D, 2), tile(D, 3),
                  _resident(wro.shape), _resident(wco.shape), _resident(wo.shape)],
        out_specs=tile(D, 0),
        out_shape=jax.ShapeDtypeStruct((T, D), F32),
        compiler_params=_params(("parallel",)),
        name="mix",
    )(x2d, og, b, f, f, wro, wco, wo)


N_FFN_INPUTS = 9


def _ffn_kernel(*refs, final_norm, side):
    x_ref, p_ref, gf_ref, wu_ref, wd_ref, gp_ref, wpg_ref, wpp_ref, gl_ref = refs[:N_FFN_INPUTS]
    if side is None:
        o_ref, = refs[N_FFN_INPUTS:]
    else:
        ret_refs = refs[N_FFN_INPUTS:N_FFN_INPUTS + N_RET_INPUTS]
        s0_ref, o_ref, og_ref, s_ref = refs[N_FFN_INPUTS + N_RET_INPUTS:]
        chunk, n_seq = side
    x = x_ref[...]
    hf = _rms(x, gf_ref[...]).astype(BF16)
    n_ff = wu_ref.shape[1] // COL_TILE
    acc = x
    for c in range(n_ff):
        cols = pl.ds(c * COL_TILE, COL_TILE)
        hid = jnp.square(jnp.maximum(_dot(hf, wu_ref[:, cols]), 0.0)).astype(BF16)
        acc = acc + _dot(hid, wd_ref[cols, :])
        if side is not None:
            _retention_body(*ret_refs, s0_ref, s_ref, og_ref, chunk=chunk, n_chunks=1,
                            seqs=range(c * n_seq // n_ff, (c + 1) * n_seq // n_ff))
    gate = jax.nn.sigmoid(_dot(_rms(acc, gp_ref[...]).astype(BF16), wpg_ref[...]))
    y = acc + gate * _dot(p_ref[...].astype(BF16), wpp_ref[...])
    if final_norm:
        y = _rms(y, gl_ref[...])
    o_ref[...] = y


def _ffn(x2d, p2d, gf, wu, wd, gp, wpg, wpp, gl, *, tm, final_norm, side=None):
    T, D = x2d.shape
    steps = T // tm
    in_specs = [pl.BlockSpec((tm, D), lambda i: (i, 0)),
                pl.BlockSpec((tm, p2d.shape[1]), lambda i: (i, 0)),
                _resident(gf.shape), _resident(wu.shape), _resident(wd.shape),
                _resident(gp.shape), _resident(wpg.shape), _resident(wpp.shape),
                _resident(gl.shape)]
    args = [x2d, p2d, gf, wu, wd, gp, wpg, wpp, gl]
    out_specs = [pl.BlockSpec((tm, D), lambda i: (i, 0))]
    out_shape = [jax.ShapeDtypeStruct((T, D), F32)]
    kernel_side = None
    if side is not None:
        a, b, f, s0, decay, cd, chunk = side
        n_all = s0.shape[0]
        assert n_all % steps == 0 and a.shape[0] == n_all * chunk
        n_seq = n_all // steps
        ret_specs, ret_args = _retention_specs(a, b, f, decay, cd, n_seq * chunk, lambda i: i)
        state_spec = pl.BlockSpec((n_seq,) + s0.shape[1:], lambda i: (i, 0, 0, 0))
        in_specs += ret_specs + [state_spec]
        args += ret_args + [s0]
        out_specs += [pl.BlockSpec((n_seq * chunk, N_HEADS * DV), lambda i: (i, 0)), state_spec]
        out_shape += [jax.ShapeDtypeStruct((a.shape[0], N_HEADS * DV), BF16),
                      jax.ShapeDtypeStruct(s0.shape, F32)]
        kernel_side = (chunk, n_seq)
    kernel = functools.partial(_ffn_kernel, final_norm=final_norm, side=kernel_side)
    out = pl.pallas_call(
        kernel,
        grid=(steps,),
        in_specs=in_specs,
        out_specs=out_specs,
        out_shape=out_shape,
        compiler_params=_params(("parallel",)),
        name="ffn" if side is None else "ffn_retention",
    )(*args)
    return out[0] if side is None else out


def _layer(xp, xs, pp, ps, s_ret, s_conv, pos_p, pos_s, w, *, n_prompt, len_prompt, chunk_prompt,
           n_sample, len_sample, final_norm):
    ffn_w = (w["g_ffn"], w["w_up"], w["w_down"], w["g_ple"], w["w_ple_gate"], w["w_ple_proj"],
             w["g_final"])
    mix_w = (w["w_ret_out"], w["w_conv_out"], w["w_o"])
    cos_p, sin_p, decay_p, qd_p, kd_p, cd_p = _tables(pos_p, chunk_prompt)
    cos_s, sin_s, decay_s, qd_s, kd_s, cd_s = _tables(pos_s, len_sample)
    ap, bp, fp, conv_p = _inproj(xp, w["g_mix"], w["w_in"], w["conv_w"], None, cos_p, sin_p, qd_p, kd_p,
                                 n_batch=n_prompt, seq_len=len_prompt, chunk=chunk_prompt, tm=1024)
    as_, bs, fs, conv_s = _inproj(xs, w["g_mix"], w["w_in"], w["conv_w"], s_conv, cos_s, sin_s, qd_s, kd_s,
                                  n_batch=n_sample, seq_len=len_sample, chunk=len_sample, tm=512)
    x1p, ret_p = _retention_mix(ap, bp, fp, decay_p, cd_p, xp, *mix_w, n_batch=n_prompt,
                                seq_len=len_prompt, chunk=chunk_prompt, n_chunks=4)
    yp, og_s, ret_s = _ffn(x1p, pp, *ffn_w, tm=512, final_norm=final_norm,
                           side=(as_, bs, fs, s_ret, decay_s, cd_s, len_sample))
    x1s = _mix(xs, og_s, bs, fs, *mix_w, tm=512)
    ys = _ffn(x1s, ps, *ffn_w, tm=512, final_norm=final_norm)
    return yp, ys, ret_p, conv_p, ret_s, conv_s


def kernel(x_prompt, x_sample, state_ret, state_conv, p_prompt, p_sample, g_mix, w_in, w_ret_out,
           conv_w, w_conv_out, w_o, g_ffn, w_up, w_down, g_ple, w_ple_gate, w_ple_proj, g_final):
    depth = w_in.shape[0]
    B, L, D = x_prompt.shape
    Bs, Ls, _ = x_sample.shape
    pos_prompt = jnp.arange(L, dtype=F32)
    pos_sample = PAST_LEN + jnp.arange(Ls, dtype=F32)
    chunk_prompt = min(RET_CHUNK, L)

    hp = x_prompt.reshape(B * L, D)
    hs = x_sample.reshape(Bs * Ls, D)
    rp, cp, rs, cs = [], [], [], []
    for i in range(depth):
        w = {"g_mix": g_mix[i][None], "w_in": w_in[i].astype(BF16),
             "w_ret_out": w_ret_out[i].astype(BF16), "conv_w": conv_w[i],
             "w_conv_out": w_conv_out[i].astype(BF16), "w_o": w_o[i].astype(BF16),
             "g_ffn": g_ffn[i][None], "w_up": w_up[i].astype(BF16), "w_down": w_down[i].astype(BF16),
             "g_ple": g_ple[i][None], "w_ple_gate": w_ple_gate[i].astype(BF16),
             "w_ple_proj": w_ple_proj[i].astype(BF16), "g_final": g_final[None]}
        hp, hs, r_p, c_p, r_s, c_s = _layer(
            hp, hs, p_prompt[i].reshape(B * L, -1), p_sample[i].reshape(Bs * Ls, -1),
            state_ret[i], state_conv[i], pos_prompt, pos_sample, w,
            n_prompt=B, len_prompt=L, chunk_prompt=chunk_prompt, n_sample=Bs, len_sample=Ls,
            final_norm=i == depth - 1)
        rp.append(r_p)
        cp.append(c_p)
        rs.append(r_s)
        cs.append(c_s)
    return (hp.reshape(B, L, D), hs.reshape(Bs, Ls, D), jnp.stack(rp), jnp.stack(cp),
            jnp.stack(rs), jnp.stack(cs))
```

```python
import functools

import jax
import jax.numpy as jnp
from jax import lax
from jax.experimental import pallas as pl
from jax.experimental.pallas import tpu as pltpu

F32 = jnp.float32
BF16 = jnp.bfloat16

N_HEADS = 8
DK = 128
DV = 256
CONV_W = 3
RET_CHUNK = 128
PAST_LEN = 16384
ROPE_BASE = 10000.0
EPS = 1e-6

V7X_SUBLANES = 8
V7X_VMEM_BYTES = 64 * 1024 * 1024
V7X_VMEM_LIMIT_BYTES = V7X_VMEM_BYTES - 4 * 1024 * 1024

COL_TILE = 1024

IN_GROUP_TILES = ((0, 2), (1, 3), (4, 6), (10,), (5, 7), (8, 9))
IN_GROUPS = len(IN_GROUP_TILES)
IN_TILES = 11
W_SLOTS = 3


def _rms(x, g):
    return x * lax.rsqrt(jnp.mean(x * x, axis=-1, keepdims=True) + EPS) * g


def _dot(a, b):
    return jnp.dot(a, b, preferred_element_type=F32)


def _params(semantics):
    return pltpu.CompilerParams(dimension_semantics=semantics,
                                vmem_limit_bytes=V7X_VMEM_LIMIT_BYTES)


def _resident(shape):
    zeros = (0,) * len(shape)
    return pl.BlockSpec(shape, lambda *_: zeros, pipeline_mode=pl.Buffered(1))


def _tables(pos, chunk):
    inv_freq = ROPE_BASE ** (-jnp.arange(0, DK, 2, dtype=F32) / DK)
    ang = pos.astype(F32)[:, None] * inv_freq[None, :]
    c, s = jnp.cos(ang), jnp.sin(ang)
    cos2 = jnp.concatenate([c, c], axis=-1)
    sin2 = jnp.concatenate([-s, s], axis=-1)
    log_g = jnp.log(1.0 - 2.0 ** (-5.0 - jnp.arange(N_HEADS, dtype=F32)))
    idx = jnp.arange(chunk, dtype=F32)
    diff = idx[:, None] - idx[None, :]
    decay = jnp.where(diff >= 0, jnp.exp(jnp.maximum(diff, 0.0)[None] * log_g[:, None, None]), 0.0)
    q_decay = jnp.exp((idx + 1.0)[None, :] * log_g[:, None])
    k_decay = jnp.exp((chunk - 1.0 - idx)[None, :] * log_g[:, None])
    chunk_decay = jnp.exp(chunk * log_g)
    qd = jnp.repeat(q_decay.T, DK, axis=1)
    kd = jnp.repeat(k_decay.T, DK, axis=1)
    return cos2, sin2, decay, qd, kd, chunk_decay


def _inproj_kernel(*refs, tm, chunk, seq_len, has_init):
    if has_init:
        (x_ref, g_ref, w_hbm, cos_ref, sin_ref, qd_ref, kd_ref, cw_ref, st_ref,
         a_ref, b_ref, f_ref, cs_ref, h_ref, cx_ref, cb_ref, carry_ref, wbuf, wsem) = refs
    else:
        (x_ref, g_ref, w_hbm, cos_ref, sin_ref, qd_ref, kd_ref, cw_ref,
         a_ref, b_ref, f_ref, cs_ref, h_ref, cx_ref, cb_ref, carry_ref, wbuf, wsem) = refs
    i = pl.program_id(0)
    j = pl.program_id(1)
    n_pos = cos_ref.shape[0]

    step = i * IN_GROUPS + j
    n_steps = pl.num_programs(0) * IN_GROUPS

    def w_copies(group):
        slot = group % W_SLOTS
        return [pltpu.make_async_copy(w_hbm.at[:, pl.ds(tile * COL_TILE, COL_TILE)],
                                      wbuf.at[slot, :, pl.ds(t * COL_TILE, COL_TILE)],
                                      wsem.at[slot, t])
                for t, tile in enumerate(IN_GROUP_TILES[group])]

    @pl.when(step == 0)
    def _():
        for group in range(W_SLOTS - 1):
            for copy in w_copies(group):
                copy.start()

    for jj in range(IN_GROUPS):
        @pl.when((j == jj) & (step + W_SLOTS - 1 < n_steps))
        def _():
            for copy in w_copies((jj + W_SLOTS - 1) % IN_GROUPS):
                copy.start()

        @pl.when(j == jj)
        def _():
            for copy in w_copies(jj):
                copy.wait()

    def proj(jj, t, h=None):
        h = h_ref[...] if h is None else h
        return _dot(h, wbuf[jj % W_SLOTS, :, t * COL_TILE:(t + 1) * COL_TILE])

    def rotary_heads(acc, dec_ref, post_scale):
        plain, scaled = [], []
        for h in range(N_HEADS):
            cols = slice(h * DK, (h + 1) * DK)
            t = acc[:, cols]
            t3 = t.reshape(tm // n_pos, n_pos, DK)
            r3 = pltpu.roll(t, DK // 2, 1).reshape(tm // n_pos, n_pos, DK)
            r = (t3 * cos_ref[...][None] + r3 * sin_ref[...][None]).reshape(tm, DK)
            if post_scale != 1.0:
                r = r * post_scale
            plain.append(r.astype(BF16))
            d = r.reshape(tm // chunk, chunk, DK) * dec_ref[:, cols][None]
            scaled.append(d.reshape(tm, DK).astype(BF16))
        return jnp.concatenate(plain + scaled, axis=1)

    @pl.when(j == 0)
    def _():
        h = _rms(x_ref[...], g_ref[...]).astype(BF16)
        h_ref[...] = h
        a_ref[...] = rotary_heads(proj(0, 0, h), qd_ref, DK ** -0.5)
        b_ref[...] = proj(0, 1, h).astype(BF16)

    @pl.when(j == 1)
    def _():
        a_ref[...] = rotary_heads(proj(1, 0), kd_ref, 1.0)
        b_ref[...] = proj(1, 1).astype(BF16)

    @pl.when(j == 2)
    def _():
        f_ref[...] = jax.nn.silu(proj(2, 0))
        cx_ref[...] = proj(2, 1)

    @pl.when(j == 3)
    def _():
        f_ref[...] = jax.nn.sigmoid(proj(3, 0))

    @pl.when(j == 4)
    def _():
        f_ref[...] = jax.nn.silu(proj(4, 0))
        cb_ref[...] = proj(4, 1)

    @pl.when(j == 5)
    def _():
        f_ref[...] = jax.nn.sigmoid(proj(5, 1))
        u = proj(5, 0) * cx_ref[...]
        def patch(pos, prev2, prev1, r1, r2):
            return (jnp.where(pos == 0, prev1, r1),
                    jnp.where(pos == 0, prev2, jnp.where(pos == 1, prev1, r2)))

        r1, r2 = pltpu.roll(u, 1, 0), pltpu.roll(u, 2, 0)
        if has_init:
            n = tm // seq_len
            prev2 = jnp.broadcast_to(st_ref[:, 0:1, :], (n, seq_len, u.shape[1])).reshape(u.shape)
            prev1 = jnp.broadcast_to(st_ref[:, 1:2, :], (n, seq_len, u.shape[1])).reshape(u.shape)
            u1, u2 = patch(lax.broadcasted_iota(jnp.int32, u.shape, 0) % seq_len, prev2, prev1, r1, r2)
        else:
            s = V7X_SUBLANES
            first = (i % (seq_len // tm)) == 0
            halo = jnp.where(first, 0.0, carry_ref[...])
            prev2 = jnp.broadcast_to(halo[s - 2:s - 1, :], halo.shape)
            prev1 = jnp.broadcast_to(halo[s - 1:s, :], halo.shape)
            h1, h2 = patch(lax.broadcasted_iota(jnp.int32, halo.shape, 0), prev2, prev1, r1[:s], r2[:s])
            u1 = jnp.concatenate([h1, r1[s:]], axis=0)
            u2 = jnp.concatenate([h2, r2[s:]], axis=0)
        cv = cw_ref[0:1, :] * u2 + cw_ref[1:2, :] * u1 + cw_ref[2:3, :] * u
        b_ref[...] = (cb_ref[...] * cv).astype(BF16)
        if has_init:
            cs_ref[...] = u.reshape(tm // seq_len, seq_len, u.shape[1])[:, seq_len - (CONV_W - 1):, :]
        else:
            carry_ref[...] = u[tm - V7X_SUBLANES:, :]
            cs_ref[0] = u[tm - (CONV_W - 1):, :]


def _inproj(x2d, g, w, cw, conv_state, cos2, sin2, qd, kd, *, n_batch, seq_len, chunk, tm):
    T, D = x2d.shape
    has_init = conv_state is not None
    assert w.shape[1] == IN_TILES * COL_TILE and D == COL_TILE and IN_GROUPS % W_SLOTS == 0
    if has_init:
        assert tm % seq_len == 0
        n = tm // seq_len
        pos_spec = pl.BlockSpec((seq_len, DK), lambda i, j: (0, 0))
        st_specs = [pl.BlockSpec((n, CONV_W - 1, D), lambda i, j: (i, 0, 0))]
        st_args = [conv_state]
        cs_spec = pl.BlockSpec((n, CONV_W - 1, D), lambda i, j: (i, 0, 0))
    else:
        assert seq_len % tm == 0 and tm % chunk == 0
        tiles = seq_len // tm
        pos_spec = pl.BlockSpec((tm, DK), lambda i, j: (i % tiles, 0))
        st_specs, st_args = [], []
        cs_spec = pl.BlockSpec((1, CONV_W - 1, D), lambda i, j: (i // tiles, 0, 0))

    def b_col(j):
        return jnp.where(j == 0, 0, jnp.where(j < 5, 1, 2))

    def f_col(j):
        return jnp.where(j <= 2, 0, jnp.where(j == 3, 3, j - 3))

    kernel = functools.partial(_inproj_kernel, tm=tm, chunk=chunk, seq_len=seq_len, has_init=has_init)
    return pl.pallas_call(
        kernel,
        grid=(T // tm, IN_GROUPS),
        in_specs=[pl.BlockSpec((tm, D), lambda i, j: (i, 0)),
                  pl.BlockSpec((1, D), lambda i, j: (0, 0)),
                  pl.BlockSpec(memory_space=pl.ANY),
                  pos_spec, pos_spec,
                  _resident(qd.shape), _resident(kd.shape), _resident(cw.shape)] + st_specs,
        out_specs=[pl.BlockSpec((tm, 2 * COL_TILE), lambda i, j: (i, jnp.minimum(j, 1))),
                   pl.BlockSpec((tm, COL_TILE), lambda i, j: (i, b_col(j))),
                   pl.BlockSpec((tm, COL_TILE), lambda i, j: (i, f_col(j))),
                   cs_spec],
        out_shape=[jax.ShapeDtypeStruct((T, 4 * COL_TILE), BF16),
                   jax.ShapeDtypeStruct((T, 3 * COL_TILE), BF16),
                   jax.ShapeDtypeStruct((T, 4 * COL_TILE), F32),
                   jax.ShapeDtypeStruct((n_batch, CONV_W - 1, D), F32)],
        scratch_shapes=[pltpu.VMEM((tm, D), BF16), pltpu.VMEM((tm, D), F32), pltpu.VMEM((tm, D), F32),
                        pltpu.VMEM((V7X_SUBLANES, D), F32),
                        pltpu.VMEM((W_SLOTS, D, 2 * COL_TILE), BF16),
                        pltpu.SemaphoreType.DMA((W_SLOTS, 2))],
        compiler_params=_params(("arbitrary", "arbitrary")),
        name="inproj_init" if has_init else "inproj",
    )(x2d, g, w, cos2, sin2, qd, kd, cw, *st_args)


N_RET_INPUTS = 5


def _retention_body(a_ref, b_ref, f_ref, dec_ref, cd_ref, s0_ref, s_ref, o_ref, *, chunk, n_chunks, seqs):
    has_init = s0_ref is not None
    fuse = chunk % DK == 0
    q_ref, qd_ref, k_ref, kd_ref = (a_ref.at[:, pl.ds(t * COL_TILE, COL_TILE)] for t in range(4))
    sg_ref = f_ref

    def v_at(rows, h):
        return b_ref[rows, pl.ds(h * DV, DV)]

    for seq in seqs:
        row_slices = [pl.ds((seq * n_chunks + c) * chunk, chunk) for c in range(n_chunks)]
        probs, kvs = {}, {}
        for c, rows in enumerate(row_slices):
            for h in range(N_HEADS):
                kcols = pl.ds(h * DK, DK)
                scores = lax.dot_general(q_ref[rows, kcols], k_ref[rows, kcols], (((1,), (1,)), ((), ())),
                                         preferred_element_type=F32) * dec_ref[h]
                probs[c, h] = scores.astype(BF16)
                kdt = jnp.transpose(kd_ref[rows, kcols].astype(F32)).astype(BF16)
                kvs[c, h] = _dot(kdt, v_at(rows, h))
        if has_init:
            states = [s0_ref[seq, h] for h in range(N_HEADS)]
        else:
            states = [s_ref[seq, h] for h in range(N_HEADS)]
        for c, rows in enumerate(row_slices):
            for h in range(N_HEADS):
                kcols = pl.ds(h * DK, DK)
                ocols = pl.ds(h * DV, DV)
                S = states[h]
                if fuse:
                    o = _dot(jnp.concatenate([probs[c, h], qd_ref[rows, kcols]], axis=1),
                             jnp.concatenate([v_at(rows, h), S.astype(BF16)], axis=0))
                else:
                    o = _dot(probs[c, h], v_at(rows, h)) + _dot(qd_ref[rows, kcols], S.astype(BF16))
                states[h] = S * cd_ref[h] + kvs[c, h]
                o = o * lax.rsqrt(jnp.mean(o * o, axis=-1, keepdims=True) + EPS)
                o_ref[rows, ocols] = (o * sg_ref[rows, ocols]).astype(BF16)
        for h in range(N_HEADS):
            s_ref[seq, h] = states[h]


def _retention_specs(a, b, f, decay, cd, rows, row_block):
    def whole(arr):
        return pl.BlockSpec((rows, arr.shape[1]), lambda *g: (row_block(*g), 0))

    specs = [whole(a), whole(b), whole(f), _resident(decay.shape), pl.BlockSpec(memory_space=pltpu.SMEM)]
    return specs, [a, b, f, decay, cd]


def _retention_mix_kernel(a_ref, b_ref, f_ref, dec_ref, cd_ref, x_ref, wro_ref, wco_ref, wo_ref,
                          x1_ref, s_ref, o_ref, *, chunk, n_chunks):
    @pl.when(pl.program_id(1) == 0)
    def _():
        s_ref[...] = jnp.zeros_like(s_ref)

    _retention_body(a_ref, b_ref, f_ref, dec_ref, cd_ref, None, s_ref, o_ref,
                    chunk=chunk, n_chunks=n_chunks, seqs=range(1))
    cbv = b_ref[:, 2 * COL_TILE:]
    sig_a, sig_b = f_ref[:, 2 * COL_TILE:3 * COL_TILE], f_ref[:, 3 * COL_TILE:]
    y_conv = _dot(cbv, wco_ref[...])
    merged = sig_a * _dot(o_ref[...], wro_ref[...]) + sig_b * y_conv
    x1_ref[...] = x_ref[...] + _dot(merged.astype(BF16), wo_ref[...])


def _retention_mix(a, b, f, decay, cd, x2d, wro, wco, wo, *, n_batch, seq_len, chunk, n_chunks):
    T, D = x2d.shape
    rows = n_chunks * chunk
    steps = seq_len // rows

    def row_block(b_, t):
        return b_ * steps + t

    def tile(width, col):
        return pl.BlockSpec((rows, width), lambda b_, t: (row_block(b_, t), col))

    in_specs, args = _retention_specs(a, b, f, decay, cd, rows, row_block)
    in_specs += [tile(D, 0), _resident(wro.shape), _resident(wco.shape), _resident(wo.shape)]
    args += [x2d, wro, wco, wo]
    kernel = functools.partial(_retention_mix_kernel, chunk=chunk, n_chunks=n_chunks)
    return pl.pallas_call(
        kernel,
        grid=(n_batch, steps),
        in_specs=in_specs,
        out_specs=[tile(D, 0), pl.BlockSpec((1, N_HEADS, DK, DV), lambda b_, t: (b_, 0, 0, 0))],
        out_shape=[jax.ShapeDtypeStruct((T, D), F32),
                   jax.ShapeDtypeStruct((n_batch, N_HEADS, DK, DV), F32)],
        scratch_shapes=[pltpu.VMEM((rows, N_HEADS * DV), BF16)],
        compiler_params=_params(("parallel", "arbitrary")),
        name="retention_mix",
    )(*args)


def _mix_kernel(x_ref, og_ref, cbv_ref, sa_ref, sb_ref, wro_ref, wco_ref, wo_ref, x1_ref):
    y_ret = _dot(og_ref[...], wro_ref[...])
    y_conv = _dot(cbv_ref[...], wco_ref[...])
    merged = sa_ref[...] * y_ret + sb_ref[...] * y_conv
    x1_ref[...] = x_ref[...] + _dot(merged.astype(BF16), wo_ref[...])


def _mix(x2d, og, b, f, wro, wco, wo, *, tm):
    T, D = x2d.shape

    def tile(width, col):
        return pl.BlockSpec((tm, width), lambda i: (i, col))

    return pl.pallas_call(
        _mix_kernel,
        grid=(T // tm,),
        in_specs=[tile(D, 0), tile(N_HEADS * DV, 0), tile(D, 2), tile(D, 2), tile(D, 3),
                  _resident(wro.shape), _resident(wco.shape), _resident(wo.shape)],
        out_specs=tile(D, 0),
        out_shape=jax.ShapeDtypeStruct((T, D), F32),
        compiler_params=_params(("parallel",)),
        name="mix",
    )(x2d, og, b, f, f, wro, wco, wo)


N_FFN_INPUTS = 9


def _ffn_kernel(*refs, final_norm, side):
    x_ref, p_ref, gf_ref, wu_ref, wd_ref, gp_ref, wpg_ref, wpp_ref, gl_ref = refs[:N_FFN_INPUTS]
    if side is None:
        o_ref, = refs[N_FFN_INPUTS:]
    else:
        ret_refs = refs[N_FFN_INPUTS:N_FFN_INPUTS + N_RET_INPUTS]
        s0_ref, o_ref, og_ref, s_ref = refs[N_FFN_INPUTS + N_RET_INPUTS:]
        chunk, n_seq = side
    x = x_ref[...]
    hf = _rms(x, gf_ref[...]).astype(BF16)
    n_ff = wu_ref.shape[1] // COL_TILE
    acc = x
    for c in range(n_ff):
        cols = pl.ds(c * COL_TILE, COL_TILE)
        hid = jnp.square(jnp.maximum(_dot(hf, wu_ref[:, cols]), 0.0)).astype(BF16)
        acc = acc + _dot(hid, wd_ref[cols, :])
        if side is not None:
            _retention_body(*ret_refs, s0_ref, s_ref, og_ref, chunk=chunk, n_chunks=1,
                            seqs=range(c * n_seq // n_ff, (c + 1) * n_seq // n_ff))
    gate = jax.nn.sigmoid(_dot(_rms(acc, gp_ref[...]).astype(BF16), wpg_ref[...]))
    y = acc + gate * _dot(p_ref[...].astype(BF16), wpp_ref[...])
    if final_norm:
        y = _rms(y, gl_ref[...])
    o_ref[...] = y


def _ffn(x2d, p2d, gf, wu, wd, gp, wpg, wpp, gl, *, tm, final_norm, side=None):
    T, D = x2d.shape
    steps = T // tm
    in_specs = [pl.BlockSpec((tm, D), lambda i: (i, 0)),
                pl.BlockSpec((tm, p2d.shape[1]), lambda i: (i, 0)),
                _resident(gf.shape), _resident(wu.shape), _resident(wd.shape),
                _resident(gp.shape), _resident(wpg.shape), _resident(wpp.shape),
                _resident(gl.shape)]
    args = [x2d, p2d, gf, wu, wd, gp, wpg, wpp, gl]
    out_specs = [pl.BlockSpec((tm, D), lambda i: (i, 0))]
    out_shape = [jax.ShapeDtypeStruct((T, D), F32)]
    kernel_side = None
    if side is not None:
        a, b, f, s0, decay, cd, chunk = side
        n_all = s0.shape[0]
        assert n_all % steps == 0 and a.shape[0] == n_all * chunk
        n_seq = n_all // steps
        ret_specs, ret_args = _retention_specs(a, b, f, decay, cd, n_seq * chunk, lambda i: i)
        state_spec = pl.BlockSpec((n_seq,) + s0.shape[1:], lambda i: (i, 0, 0, 0))
        in_specs += ret_specs + [state_spec]
        args += ret_args + [s0]
        out_specs += [pl.BlockSpec((n_seq * chunk, N_HEADS * DV), lambda i: (i, 0)), state_spec]
        out_shape += [jax.ShapeDtypeStruct((a.shape[0], N_HEADS * DV), BF16),
                      jax.ShapeDtypeStruct(s0.shape, F32)]
        kernel_side = (chunk, n_seq)
    kernel = functools.partial(_ffn_kernel, final_norm=final_norm, side=kernel_side)
    out = pl.pallas_call(
        kernel,
        grid=(steps,),
        in_specs=in_specs,
        out_specs=out_specs,
        out_shape=out_shape,
        compiler_params=_params(("parallel",)),
        name="ffn" if side is None else "ffn_retention",
    )(*args)
    return out[0] if side is None else out


def _layer(xp, xs, pp, ps, s_ret, s_conv, pos_p, pos_s, w, *, n_prompt, len_prompt, chunk_prompt,
           n_sample, len_sample, final_norm):
    ffn_w = (w["g_ffn"], w["w_up"], w["w_down"], w["g_ple"], w["w_ple_gate"], w["w_ple_proj"],
             w["g_final"])
    mix_w = (w["w_ret_out"], w["w_conv_out"], w["w_o"])
    cos_p, sin_p, decay_p, qd_p, kd_p, cd_p = _tables(pos_p, chunk_prompt)
    cos_s, sin_s, decay_s, qd_s, kd_s, cd_s = _tables(pos_s, len_sample)
    ap, bp, fp, conv_p = _inproj(xp, w["g_mix"], w["w_in"], w["conv_w"], None, cos_p, sin_p, qd_p, kd_p,
                                 n_batch=n_prompt, seq_len=len_prompt, chunk=chunk_prompt, tm=1024)
    as_, bs, fs, conv_s = _inproj(xs, w["g_mix"], w["w_in"], w["conv_w"], s_conv, cos_s, sin_s, qd_s, kd_s,
                                  n_batch=n_sample, seq_len=len_sample, chunk=len_sample, tm=512)
    x1p, ret_p = _retention_mix(ap, bp, fp, decay_p, cd_p, xp, *mix_w, n_batch=n_prompt,
                                seq_len=len_prompt, chunk=chunk_prompt, n_chunks=4)
    yp, og_s, ret_s = _ffn(x1p, pp, *ffn_w, tm=512, final_norm=final_norm,
                           side=(as_, bs, fs, s_ret, decay_s, cd_s, len_sample))
    x1s = _mix(xs, og_s, bs, fs, *mix_w, tm=512)
    ys = _ffn(x1s, ps, *ffn_w, tm=512, final_norm=final_norm)
    return yp, ys, ret_p, conv_p, ret_s, conv_s


def kernel(x_prompt, x_sample, state_ret, state_conv, p_prompt, p_sample, g_mix, w_in, w_ret_out,
           conv_w, w_conv_out, w_o, g_ffn, w_up, w_down, g_ple, w_ple_gate, w_ple_proj, g_final):
    depth = w_in.shape[0]
    B, L, D = x_prompt.shape
    Bs, Ls, _ = x_sample.shape
    pos_prompt = jnp.arange(L, dtype=F32)
    pos_sample = PAST_LEN + jnp.arange(Ls, dtype=F32)
    chunk_prompt = min(RET_CHUNK, L)

    hp = x_prompt.reshape(B * L, D)
    hs = x_sample.reshape(Bs * Ls, D)
    rp, cp, rs, cs = [], [], [], []
    for i in range(depth):
        w = {"g_mix": g_mix[i][None], "w_in": w_in[i].astype(BF16),
             "w_ret_out": w_ret_out[i].astype(BF16), "conv_w": conv_w[i],
             "w_conv_out": w_conv_out[i].astype(BF16), "w_o": w_o[i].astype(BF16),
             "g_ffn": g_ffn[i][None], "w_up": w_up[i].astype(BF16), "w_down": w_down[i].astype(BF16),
             "g_ple": g_ple[i][None], "w_ple_gate": w_ple_gate[i].astype(BF16),
             "w_ple_proj": w_ple_proj[i].astype(BF16), "g_final": g_final[None]}
        hp, hs, r_p, c_p, r_s, c_s = _layer(
            hp, hs, p_prompt[i].reshape(B * L, -1), p_sample[i].reshape(Bs * Ls, -1),
            state_ret[i], state_conv[i], pos_prompt, pos_sample, w,
            n_prompt=B, len_prompt=L, chunk_prompt=chunk_prompt, n_sample=Bs, len_sample=Ls,
            final_norm=i == depth - 1)
        rp.append(r_p)
        cp.append(c_p)
        rs.append(r_s)
        cs.append(c_s)
    return (hp.reshape(B, L, D), hs.reshape(Bs, Ls, D), jnp.stack(rp), jnp.stack(cp),
            jnp.stack(rs), jnp.stack(cs))
```

```python
import functools

import jax
import jax.numpy as jnp
from jax import lax
from jax.experimental import pallas as pl
from jax.experimental.pallas import tpu as pltpu

F32 = jnp.float32
BF16 = jnp.bfloat16

N_HEADS = 8
DK = 128
DV = 256
CONV_W = 3
RET_CHUNK = 128
PAST_LEN = 16384
ROPE_BASE = 10000.0
EPS = 1e-6

V7X_SUBLANES = 8
V7X_VMEM_BYTES = 64 * 1024 * 1024
V7X_VMEM_LIMIT_BYTES = V7X_VMEM_BYTES - 4 * 1024 * 1024

COL_TILE = 1024

IN_GROUP_TILES = ((0, 2), (1, 3), (4, 6), (10,), (5, 7), (8, 9))
IN_GROUPS = len(IN_GROUP_TILES)
IN_TILES = 11
W_SLOTS = 3
CAST_CHUNK_BYTES = 512 * 1024


def _rms(x, g):
    return x * lax.rsqrt(jnp.mean(x * x, axis=-1, keepdims=True) + EPS) * g


def _dot(a, b):
    return jnp.dot(a, b, preferred_element_type=F32)


def _params(semantics):
    return pltpu.CompilerParams(dimension_semantics=semantics,
                                vmem_limit_bytes=V7X_VMEM_LIMIT_BYTES)


def _resident(shape):
    zeros = (0,) * len(shape)
    return pl.BlockSpec(shape, lambda *_: zeros, pipeline_mode=pl.Buffered(1))


def _tables(pos, chunk):
    inv_freq = ROPE_BASE ** (-jnp.arange(0, DK, 2, dtype=F32) / DK)
    ang = pos.astype(F32)[:, None] * inv_freq[None, :]
    c, s = jnp.cos(ang), jnp.sin(ang)
    cos2 = jnp.concatenate([c, c], axis=-1)
    sin2 = jnp.concatenate([-s, s], axis=-1)
    log_g = jnp.log(1.0 - 2.0 ** (-5.0 - jnp.arange(N_HEADS, dtype=F32)))
    idx = jnp.arange(chunk, dtype=F32)
    diff = idx[:, None] - idx[None, :]
    decay = jnp.where(diff >= 0, jnp.exp(jnp.maximum(diff, 0.0)[None] * log_g[:, None, None]), 0.0)
    q_decay = jnp.exp((idx + 1.0)[None, :] * log_g[:, None])
    k_decay = jnp.exp((chunk - 1.0 - idx)[None, :] * log_g[:, None])
    chunk_decay = jnp.exp(chunk * log_g)
    qd = jnp.repeat(q_decay.T, DK, axis=1)
    kd = jnp.repeat(k_decay.T, DK, axis=1)
    return cos2, sin2, decay, qd, kd, chunk_decay


def _inproj_kernel(*refs, tm, chunk, seq_len, has_init, cast_ranges):
    refs = list(refs)
    x_ref, g_ref, w_hbm, cos_ref, sin_ref, qd_ref, kd_ref, cw_ref = refs[:8]
    del refs[:8]
    st_ref = refs.pop(0) if has_init else None
    n_cast = len(cast_ranges)
    cast_in = refs[:n_cast]
    a_ref, b_ref, f_ref, cs_ref = refs[n_cast:n_cast + 4]
    cast_out = refs[n_cast + 4:2 * n_cast + 4]
    h_ref, cx_ref, cb_ref, carry_ref, wbuf, wsem = refs[2 * n_cast + 4:]
    i = pl.program_id(0)
    j = pl.program_id(1)
    n_pos = cos_ref.shape[0]

    step = i * IN_GROUPS + j
    n_steps = pl.num_programs(0) * IN_GROUPS

    for src, dst, (first, last) in zip(cast_in, cast_out, cast_ranges):
        @pl.when((step >= first) & (step < last))
        def _():
            dst[...] = src[...].astype(BF16)

    def w_copies(group):
        slot = group % W_SLOTS
        return [pltpu.make_async_copy(w_hbm.at[:, pl.ds(tile * COL_TILE, COL_TILE)],
                                      wbuf.at[slot, :, pl.ds(t * COL_TILE, COL_TILE)],
                                      wsem.at[slot, t])
                for t, tile in enumerate(IN_GROUP_TILES[group])]

    @pl.when(step == 0)
    def _():
        for group in range(W_SLOTS - 1):
            for copy in w_copies(group):
                copy.start()

    for jj in range(IN_GROUPS):
        @pl.when((j == jj) & (step + W_SLOTS - 1 < n_steps))
        def _():
            for copy in w_copies((jj + W_SLOTS - 1) % IN_GROUPS):
                copy.start()

        @pl.when(j == jj)
        def _():
            for copy in w_copies(jj):
                copy.wait()

    def proj(jj, t, h=None):
        h = h_ref[...] if h is None else h
        return _dot(h, wbuf[jj % W_SLOTS, :, t * COL_TILE:(t + 1) * COL_TILE])

    def rotary_heads(acc, dec_ref, post_scale):
        plain, scaled = [], []
        for h in range(N_HEADS):
            cols = slice(h * DK, (h + 1) * DK)
            t = acc[:, cols]
            t3 = t.reshape(tm // n_pos, n_pos, DK)
            r3 = pltpu.roll(t, DK // 2, 1).reshape(tm // n_pos, n_pos, DK)
            r = (t3 * cos_ref[...][None] + r3 * sin_ref[...][None]).reshape(tm, DK)
            if post_scale != 1.0:
                r = r * post_scale
            plain.append(r.astype(BF16))
            d = r.reshape(tm // chunk, chunk, DK) * dec_ref[:, cols][None]
            scaled.append(d.reshape(tm, DK).astype(BF16))
        return jnp.concatenate(plain + scaled, axis=1)

    @pl.when(j == 0)
    def _():
        h = _rms(x_ref[...], g_ref[...]).astype(BF16)
        h_ref[...] = h
        a_ref[...] = rotary_heads(proj(0, 0, h), qd_ref, DK ** -0.5)
        b_ref[...] = proj(0, 1, h).astype(BF16)

    @pl.when(j == 1)
    def _():
        a_ref[...] = rotary_heads(proj(1, 0), kd_ref, 1.0)
        b_ref[...] = proj(1, 1).astype(BF16)

    @pl.when(j == 2)
    def _():
        f_ref[...] = jax.nn.silu(proj(2, 0))
        cx_ref[...] = proj(2, 1)

    @pl.when(j == 3)
    def _():
        f_ref[...] = jax.nn.sigmoid(proj(3, 0))

    @pl.when(j == 4)
    def _():
        f_ref[...] = jax.nn.silu(proj(4, 0))
        cb_ref[...] = proj(4, 1)

    @pl.when(j == 5)
    def _():
        f_ref[...] = jax.nn.sigmoid(proj(5, 1))
        u = proj(5, 0) * cx_ref[...]
        def patch(pos, prev2, prev1, r1, r2):
            return (jnp.where(pos == 0, prev1, r1),
                    jnp.where(pos == 0, prev2, jnp.where(pos == 1, prev1, r2)))

        r1, r2 = pltpu.roll(u, 1, 0), pltpu.roll(u, 2, 0)
        if has_init:
            n = tm // seq_len
            prev2 = jnp.broadcast_to(st_ref[:, 0:1, :], (n, seq_len, u.shape[1])).reshape(u.shape)
            prev1 = jnp.broadcast_to(st_ref[:, 1:2, :], (n, seq_len, u.shape[1])).reshape(u.shape)
            u1, u2 = patch(lax.broadcasted_iota(jnp.int32, u.shape, 0) % seq_len, prev2, prev1, r1, r2)
        else:
            s = V7X_SUBLANES
            first = (i % (seq_len // tm)) == 0
            halo = jnp.where(first, 0.0, carry_ref[...])
            prev2 = jnp.broadcast_to(halo[s - 2:s - 1, :], halo.shape)
            prev1 = jnp.broadcast_to(halo[s - 1:s, :], halo.shape)
            h1, h2 = patch(lax.broadcasted_iota(jnp.int32, halo.shape, 0), prev2, prev1, r1[:s], r2[:s])
            u1 = jnp.concatenate([h1, r1[s:]], axis=0)
            u2 = jnp.concatenate([h2, r2[s:]], axis=0)
        cv = cw_ref[0:1, :] * u2 + cw_ref[1:2, :] * u1 + cw_ref[2:3, :] * u
        b_ref[...] = (cb_ref[...] * cv).astype(BF16)
        if has_init:
            cs_ref[...] = u.reshape(tm // seq_len, seq_len, u.shape[1])[:, seq_len - (CONV_W - 1):, :]
        else:
            carry_ref[...] = u[tm - V7X_SUBLANES:, :]
            cs_ref[0] = u[tm - (CONV_W - 1):, :]


def _inproj(x2d, g, w, cw, conv_state, cos2, sin2, qd, kd, *, n_batch, seq_len, chunk, tm, cast=()):
    T, D = x2d.shape
    has_init = conv_state is not None
    assert w.shape[1] == IN_TILES * COL_TILE and D == COL_TILE and IN_GROUPS % W_SLOTS == 0
    if has_init:
        assert tm % seq_len == 0
        n = tm // seq_len
        pos_spec = pl.BlockSpec((seq_len, DK), lambda i, j: (0, 0))
        st_specs = [pl.BlockSpec((n, CONV_W - 1, D), lambda i, j: (i, 0, 0))]
        st_args = [conv_state]
        cs_spec = pl.BlockSpec((n, CONV_W - 1, D), lambda i, j: (i, 0, 0))
    else:
        assert seq_len % tm == 0 and tm % chunk == 0
        tiles = seq_len // tm
        pos_spec = pl.BlockSpec((tm, DK), lambda i, j: (i % tiles, 0))
        st_specs, st_args = [], []
        cs_spec = pl.BlockSpec((1, CONV_W - 1, D), lambda i, j: (i // tiles, 0, 0))

    def b_col(j):
        return jnp.where(j == 0, 0, jnp.where(j < 5, 1, 2))

    def f_col(j):
        return jnp.where(j <= 2, 0, jnp.where(j == 3, 3, j - 3))

    n_steps = (T // tm) * IN_GROUPS
    cast_specs_in, cast_specs_out, cast_shapes, cast_ranges = [], [], [], []
    first = 0
    for m in cast:
        rows = CAST_CHUNK_BYTES // (m.shape[1] * m.dtype.itemsize)
        n_chunks = m.shape[0] // rows
        assert m.shape[0] % rows == 0 and first + n_chunks <= n_steps

        def chunk_map(i, j, first=first, n_chunks=n_chunks):
            return jnp.clip(i * IN_GROUPS + j - first, 0, n_chunks - 1), 0

        cast_specs_in.append(pl.BlockSpec((rows, m.shape[1]), chunk_map))
        cast_specs_out.append(pl.BlockSpec((rows, m.shape[1]), chunk_map))
        cast_shapes.append(jax.ShapeDtypeStruct(m.shape, BF16))
        cast_ranges.append((first, first + n_chunks))
        first += n_chunks

    kernel = functools.partial(_inproj_kernel, tm=tm, chunk=chunk, seq_len=seq_len, has_init=has_init,
                               cast_ranges=tuple(cast_ranges))
    return pl.pallas_call(
        kernel,
        grid=(T // tm, IN_GROUPS),
        in_specs=[pl.BlockSpec((tm, D), lambda i, j: (i, 0)),
                  pl.BlockSpec((1, D), lambda i, j: (0, 0)),
                  pl.BlockSpec(memory_space=pl.ANY),
                  pos_spec, pos_spec,
                  _resident(qd.shape), _resident(kd.shape), _resident(cw.shape)] + st_specs + cast_specs_in,
        out_specs=[pl.BlockSpec((tm, 2 * COL_TILE), lambda i, j: (i, jnp.minimum(j, 1))),
                   pl.BlockSpec((tm, COL_TILE), lambda i, j: (i, b_col(j))),
                   pl.BlockSpec((tm, COL_TILE), lambda i, j: (i, f_col(j))),
                   cs_spec] + cast_specs_out,
        out_shape=[jax.ShapeDtypeStruct((T, 4 * COL_TILE), BF16),
                   jax.ShapeDtypeStruct((T, 3 * COL_TILE), BF16),
                   jax.ShapeDtypeStruct((T, 4 * COL_TILE), F32),
                   jax.ShapeDtypeStruct((n_batch, CONV_W - 1, D), F32)] + cast_shapes,
        scratch_shapes=[pltpu.VMEM((tm, D), BF16), pltpu.VMEM((tm, D), F32), pltpu.VMEM((tm, D), F32),
                        pltpu.VMEM((V7X_SUBLANES, D), F32),
                        pltpu.VMEM((W_SLOTS, D, 2 * COL_TILE), BF16),
                        pltpu.SemaphoreType.DMA((W_SLOTS, 2))],
        compiler_params=_params(("arbitrary", "arbitrary")),
        name="inproj_init" if has_init else "inproj",
    )(x2d, g, w, cos2, sin2, qd, kd, cw, *st_args, *cast)


N_RET_INPUTS = 5


def _retention_body(a_ref, b_ref, f_ref, dec_ref, cd_ref, s0_ref, s_ref, o_ref, *, chunk, n_chunks, seqs):
    has_init = s0_ref is not None
    fuse = chunk % DK == 0
    q_ref, qd_ref, k_ref, kd_ref = (a_ref.at[:, pl.ds(t * COL_TILE, COL_TILE)] for t in range(4))
    sg_ref = f_ref

    def v_at(rows, h):
        return b_ref[rows, pl.ds(h * DV, DV)]

    for seq in seqs:
        row_slices = [pl.ds((seq * n_chunks + c) * chunk, chunk) for c in range(n_chunks)]
        probs, kvs = {}, {}
        for c, rows in enumerate(row_slices):
            for h in range(N_HEADS):
                kcols = pl.ds(h * DK, DK)
                scores = lax.dot_general(q_ref[rows, kcols], k_ref[rows, kcols], (((1,), (1,)), ((), ())),
                                         preferred_element_type=F32) * dec_ref[h]
                probs[c, h] = scores.astype(BF16)
                kdt = jnp.transpose(kd_ref[rows, kcols].astype(F32)).astype(BF16)
                kvs[c, h] = _dot(kdt, v_at(rows, h))
        if has_init:
            states = [s0_ref[seq, h] for h in range(N_HEADS)]
        else:
            states = [s_ref[seq, h] for h in range(N_HEADS)]
        for c, rows in enumerate(row_slices):
            for h in range(N_HEADS):
                kcols = pl.ds(h * DK, DK)
                ocols = pl.ds(h * DV, DV)
                S = states[h]
                if fuse:
                    o = _dot(jnp.concatenate([probs[c, h], qd_ref[rows, kcols]], axis=1),
                             jnp.concatenate([v_at(rows, h), S.astype(BF16)], axis=0))
                else:
                    o = _dot(probs[c, h], v_at(rows, h)) + _dot(qd_ref[rows, kcols], S.astype(BF16))
                states[h] = S * cd_ref[h] + kvs[c, h]
                o = o * lax.rsqrt(jnp.mean(o * o, axis=-1, keepdims=True) + EPS)
                o_ref[rows, ocols] = (o * sg_ref[rows, ocols]).astype(BF16)
        for h in range(N_HEADS):
            s_ref[seq, h] = states[h]


def _retention_specs(a, b, f, decay, cd, rows, row_block):
    def whole(arr):
        return pl.BlockSpec((rows, arr.shape[1]), lambda *g: (row_block(*g), 0))

    specs = [whole(a), whole(b), whole(f), _resident(decay.shape), pl.BlockSpec(memory_space=pltpu.SMEM)]
    return specs, [a, b, f, decay, cd]


def _retention_mix_kernel(a_ref, b_ref, f_ref, dec_ref, cd_ref, x_ref, wro_ref, wco_ref, wo_ref,
                          x1_ref, s_ref, o_ref, *, chunk, n_chunks):
    @pl.when(pl.program_id(1) == 0)
    def _():
        s_ref[...] = jnp.zeros_like(s_ref)

    _retention_body(a_ref, b_ref, f_ref, dec_ref, cd_ref, None, s_ref, o_ref,
                    chunk=chunk, n_chunks=n_chunks, seqs=range(1))
    cbv = b_ref[:, 2 * COL_TILE:]
    sig_a, sig_b = f_ref[:, 2 * COL_TILE:3 * COL_TILE], f_ref[:, 3 * COL_TILE:]
    y_conv = _dot(cbv, wco_ref[...])
    merged = sig_a * _dot(o_ref[...], wro_ref[...]) + sig_b * y_conv
    x1_ref[...] = x_ref[...] + _dot(merged.astype(BF16), wo_ref[...])


def _retention_mix(a, b, f, decay, cd, x2d, wro, wco, wo, *, n_batch, seq_len, chunk, n_chunks):
    T, D = x2d.shape
    rows = n_chunks * chunk
    steps = seq_len // rows

    def row_block(b_, t):
        return b_ * steps + t

    def tile(width, col):
        return pl.BlockSpec((rows, width), lambda b_, t: (row_block(b_, t), col))

    in_specs, args = _retention_specs(a, b, f, decay, cd, rows, row_block)
    in_specs += [tile(D, 0), _resident(wro.shape), _resident(wco.shape), _resident(wo.shape)]
    args += [x2d, wro, wco, wo]
    kernel = functools.partial(_retention_mix_kernel, chunk=chunk, n_chunks=n_chunks)
    return pl.pallas_call(
        kernel,
        grid=(n_batch, steps),
        in_specs=in_specs,
        out_specs=[tile(D, 0), pl.BlockSpec((1, N_HEADS, DK, DV), lambda b_, t: (b_, 0, 0, 0))],
        out_shape=[jax.ShapeDtypeStruct((T, D), F32),
                   jax.ShapeDtypeStruct((n_batch, N_HEADS, DK, DV), F32)],
        scratch_shapes=[pltpu.VMEM((rows, N_HEADS * DV), BF16)],
        compiler_params=_params(("parallel", "arbitrary")),
        name="retention_mix",
    )(*args)


def _mix_kernel(x_ref, og_ref, cbv_ref, sa_ref, sb_ref, wro_ref, wco_ref, wo_ref, x1_ref):
    y_ret = _dot(og_ref[...], wro_ref[...])
    y_conv = _dot(cbv_ref[...], wco_ref[...])
    merged = sa_ref[...] * y_ret + sb_ref[...] * y_conv
    x1_ref[...] = x_ref[...] + _dot(merged.astype(BF16), wo_ref[...])


def _mix(x2d, og, b, f, wro, wco, wo, *, tm):
    T, D = x2d.shape

    def tile(width, col):
        return pl.BlockSpec((tm, width), lambda i: (i, col))

    return pl.pallas_call(
        _mix_kernel,
        grid=(T // tm,),
        in_specs=[tile(D, 0), tile(N_HEADS * DV, 0), tile(D, 2), tile(D, 2), tile(D, 3),
                  _resident(wro.shape), _resident(wco.shape), _resident(wo.shape)],
        out_specs=tile(D, 0),
        out_shape=jax.ShapeDtypeStruct((T, D), F32),
        compiler_params=_params(("parallel",)),
        name="mix",
    )(x2d, og, b, f, f, wro, wco, wo)


N_FFN_INPUTS = 9


def _ffn_kernel(*refs, final_norm, side):
    x_ref, p_ref, gf_ref, wu_ref, wd_ref, gp_ref, wpg_ref, wpp_ref, gl_ref = refs[:N_FFN_INPUTS]
    if side is None:
        o_ref, = refs[N_FFN_INPUTS:]
    else:
        ret_refs = refs[N_FFN_INPUTS:N_FFN_INPUTS + N_RET_INPUTS]
        s0_ref, o_ref, og_ref, s_ref = refs[N_FFN_INPUTS + N_RET_INPUTS:]
        chunk, n_seq = side
    x = x_ref[...]
    hf = _rms(x, gf_ref[...]).astype(BF16)
    n_ff = wu_ref.shape[1] // COL_TILE
    acc = x
    for c in range(n_ff):
        cols = pl.ds(c * COL_TILE, COL_TILE)
        hid = jnp.square(jnp.maximum(_dot(hf, wu_ref[:, cols]), 0.0)).astype(BF16)
        acc = acc + _dot(hid, wd_ref[cols, :])
        if side is not None:
            _retention_body(*ret_refs, s0_ref, s_ref, og_ref, chunk=chunk, n_chunks=1,
                            seqs=range(c * n_seq // n_ff, (c + 1) * n_seq // n_ff))
    gate = jax.nn.sigmoid(_dot(_rms(acc, gp_ref[...]).astype(BF16), wpg_ref[...]))
    y = acc + gate * _dot(p_ref[...].astype(BF16), wpp_ref[...])
    if final_norm:
        y = _rms(y, gl_ref[...])
    o_ref[...] = y


def _ffn(x2d, p2d, gf, wu, wd, gp, wpg, wpp, gl, *, tm, final_norm, side=None):
    T, D = x2d.shape
    steps = T // tm
    in_specs = [pl.BlockSpec((tm, D), lambda i: (i, 0)),
                pl.BlockSpec((tm, p2d.shape[1]), lambda i: (i, 0)),
                _resident(gf.shape), _resident(wu.shape), _resident(wd.shape),
                _resident(gp.shape), _resident(wpg.shape), _resident(wpp.shape),
                _resident(gl.shape)]
    args = [x2d, p2d, gf, wu, wd, gp, wpg, wpp, gl]
    out_specs = [pl.BlockSpec((tm, D), lambda i: (i, 0))]
    out_shape = [jax.ShapeDtypeStruct((T, D), F32)]
    kernel_side = None
    if side is not None:
        a, b, f, s0, decay, cd, chunk = side
        n_all = s0.shape[0]
        assert n_all % steps == 0 and a.shape[0] == n_all * chunk
        n_seq = n_all // steps
        ret_specs, ret_args = _retention_specs(a, b, f, decay, cd, n_seq * chunk, lambda i: i)
        state_spec = pl.BlockSpec((n_seq,) + s0.shape[1:], lambda i: (i, 0, 0, 0))
        in_specs += ret_specs + [state_spec]
        args += ret_args + [s0]
        out_specs += [pl.BlockSpec((n_seq * chunk, N_HEADS * DV), lambda i: (i, 0)), state_spec]
        out_shape += [jax.ShapeDtypeStruct((a.shape[0], N_HEADS * DV), BF16),
                      jax.ShapeDtypeStruct(s0.shape, F32)]
        kernel_side = (chunk, n_seq)
    kernel = functools.partial(_ffn_kernel, final_norm=final_norm, side=kernel_side)
    out = pl.pallas_call(
        kernel,
        grid=(steps,),
        in_specs=in_specs,
        out_specs=out_specs,
        out_shape=out_shape,
        compiler_params=_params(("parallel",)),
        name="ffn" if side is None else "ffn_retention",
    )(*args)
    return out[0] if side is None else out


def _layer(xp, xs, pp, ps, s_ret, s_conv, pos_p, pos_s, w, *, n_prompt, len_prompt, chunk_prompt,
           n_sample, len_sample, final_norm):
    cos_p, sin_p, decay_p, qd_p, kd_p, cd_p = _tables(pos_p, chunk_prompt)
    cos_s, sin_s, decay_s, qd_s, kd_s, cd_s = _tables(pos_s, len_sample)
    ap, bp, fp, conv_p, w_up, w_down = _inproj(
        xp, w["g_mix"], w["w_in"], w["conv_w"], None, cos_p, sin_p, qd_p, kd_p, n_batch=n_prompt,
        seq_len=len_prompt, chunk=chunk_prompt, tm=1024, cast=(w["w_up"], w["w_down"]))
    ffn_w = (w["g_ffn"], w_up, w_down, w["g_ple"], w["w_ple_gate"], w["w_ple_proj"], w["g_final"])
    mix_w = (w["w_ret_out"], w["w_conv_out"], w["w_o"])
    as_, bs, fs, conv_s = _inproj(xs, w["g_mix"], w["w_in"], w["conv_w"], s_conv, cos_s, sin_s, qd_s, kd_s,
                                  n_batch=n_sample, seq_len=len_sample, chunk=len_sample, tm=512)
    x1p, ret_p = _retention_mix(ap, bp, fp, decay_p, cd_p, xp, *mix_w, n_batch=n_prompt,
                                seq_len=len_prompt, chunk=chunk_prompt, n_chunks=4)
    yp, og_s, ret_s = _ffn(x1p, pp, *ffn_w, tm=512, final_norm=final_norm,
                           side=(as_, bs, fs, s_ret, decay_s, cd_s, len_sample))
    x1s = _mix(xs, og_s, bs, fs, *mix_w, tm=512)
    ys = _ffn(x1s, ps, *ffn_w, tm=512, final_norm=final_norm)
    return yp, ys, ret_p, conv_p, ret_s, conv_s


def kernel(x_prompt, x_sample, state_ret, state_conv, p_prompt, p_sample, g_mix, w_in, w_ret_out,
           conv_w, w_conv_out, w_o, g_ffn, w_up, w_down, g_ple, w_ple_gate, w_ple_proj, g_final):
    depth = w_in.shape[0]
    B, L, D = x_prompt.shape
    Bs, Ls, _ = x_sample.shape
    pos_prompt = jnp.arange(L, dtype=F32)
    pos_sample = PAST_LEN + jnp.arange(Ls, dtype=F32)
    chunk_prompt = min(RET_CHUNK, L)

    hp = x_prompt.reshape(B * L, D)
    hs = x_sample.reshape(Bs * Ls, D)
    rp, cp, rs, cs = [], [], [], []
    for i in range(depth):
        w = {"g_mix": g_mix[i][None], "w_in": w_in[i].astype(BF16),
             "w_ret_out": w_ret_out[i].astype(BF16), "conv_w": conv_w[i],
             "w_conv_out": w_conv_out[i].astype(BF16), "w_o": w_o[i].astype(BF16),
             "g_ffn": g_ffn[i][None], "w_up": w_up[i], "w_down": w_down[i],
             "g_ple": g_ple[i][None], "w_ple_gate": w_ple_gate[i].astype(BF16),
             "w_ple_proj": w_ple_proj[i].astype(BF16), "g_final": g_final[None]}
        hp, hs, r_p, c_p, r_s, c_s = _layer(
            hp, hs, p_prompt[i].reshape(B * L, -1), p_sample[i].reshape(Bs * Ls, -1),
            state_ret[i], state_conv[i], pos_prompt, pos_sample, w,
            n_prompt=B, len_prompt=L, chunk_prompt=chunk_prompt, n_sample=Bs, len_sample=Ls,
            final_norm=i == depth - 1)
        rp.append(r_p)
        cp.append(c_p)
        rs.append(r_s)
        cs.append(c_s)
    return (hp.reshape(B, L, D), hs.reshape(Bs, Ls, D), jnp.stack(rp), jnp.stack(cp),
            jnp.stack(rs), jnp.stack(cs))
```

```python
import functools

import jax
import jax.numpy as jnp
from jax import lax
from jax.experimental import pallas as pl
from jax.experimental.pallas import tpu as pltpu

F32 = jnp.float32
BF16 = jnp.bfloat16

N_HEADS = 8
DK = 128
DV = 256
CONV_W = 3
RET_CHUNK = 128
PAST_LEN = 16384
ROPE_BASE = 10000.0
EPS = 1e-6

V7X_SUBLANES = 8
V7X_VMEM_BYTES = 64 * 1024 * 1024
V7X_VMEM_LIMIT_BYTES = V7X_VMEM_BYTES - 4 * 1024 * 1024

COL_TILE = 1024

IN_GROUP_TILES = ((0, 2), (1, 3), (4, 6), (10,), (5, 7), (8, 9))
IN_GROUPS = len(IN_GROUP_TILES)
IN_TILES = 11
W_SLOTS = 3
CAST_CHUNK_BYTES = 512 * 1024


def _rms(x, g):
    return x * lax.rsqrt(jnp.mean(x * x, axis=-1, keepdims=True) + EPS) * g


def _dot(a, b):
    return jnp.dot(a, b, preferred_element_type=F32)


def _params(semantics):
    return pltpu.CompilerParams(dimension_semantics=semantics,
                                vmem_limit_bytes=V7X_VMEM_LIMIT_BYTES)


def _resident(shape):
    zeros = (0,) * len(shape)
    return pl.BlockSpec(shape, lambda *_: zeros, pipeline_mode=pl.Buffered(1))


def _tables(pos, chunk):
    inv_freq = ROPE_BASE ** (-jnp.arange(0, DK, 2, dtype=F32) / DK)
    ang = pos.astype(F32)[:, None] * inv_freq[None, :]
    c, s = jnp.cos(ang), jnp.sin(ang)
    cos2 = jnp.concatenate([c, c], axis=-1)
    sin2 = jnp.concatenate([-s, s], axis=-1)
    log_g = jnp.log(1.0 - 2.0 ** (-5.0 - jnp.arange(N_HEADS, dtype=F32)))
    idx = jnp.arange(chunk, dtype=F32)
    diff = idx[:, None] - idx[None, :]
    decay = jnp.where(diff >= 0, jnp.exp(jnp.maximum(diff, 0.0)[None] * log_g[:, None, None]), 0.0)
    q_decay = jnp.exp((idx + 1.0)[None, :] * log_g[:, None])
    k_decay = jnp.exp((chunk - 1.0 - idx)[None, :] * log_g[:, None])
    chunk_decay = jnp.exp(chunk * log_g)
    qd = jnp.repeat(q_decay.T, DK, axis=1)
    kd = jnp.repeat(k_decay.T, DK, axis=1)
    return cos2, sin2, decay, qd, kd, chunk_decay


def _inproj_kernel(*refs, tm, chunk, seq_len, has_init, cast_ranges):
    refs = list(refs)
    x_ref, g_ref, w_hbm, cos_ref, sin_ref, qd_ref, kd_ref, cw_ref = refs[:8]
    del refs[:8]
    st_ref = refs.pop(0) if has_init else None
    n_cast = len(cast_ranges)
    cast_in = refs[:n_cast]
    a_ref, b_ref, f_ref, cs_ref = refs[n_cast:n_cast + 4]
    cast_out = refs[n_cast + 4:2 * n_cast + 4]
    h_ref, cx_ref, cb_ref, carry_ref, wbuf, wsem = refs[2 * n_cast + 4:]
    i = pl.program_id(0)
    j = pl.program_id(1)
    n_pos = cos_ref.shape[0]

    step = i * IN_GROUPS + j
    n_steps = pl.num_programs(0) * IN_GROUPS

    for src, dst, (first, last) in zip(cast_in, cast_out, cast_ranges):
        @pl.when((step >= first) & (step < last))
        def _():
            dst[...] = src[...].astype(BF16)

    def w_copies(group):
        slot = group % W_SLOTS
        return [pltpu.make_async_copy(w_hbm.at[:, pl.ds(tile * COL_TILE, COL_TILE)],
                                      wbuf.at[slot, :, pl.ds(t * COL_TILE, COL_TILE)],
                                      wsem.at[slot, t])
                for t, tile in enumerate(IN_GROUP_TILES[group])]

    @pl.when(step == 0)
    def _():
        for group in range(W_SLOTS - 1):
            for copy in w_copies(group):
                copy.start()

    for jj in range(IN_GROUPS):
        @pl.when((j == jj) & (step + W_SLOTS - 1 < n_steps))
        def _():
            for copy in w_copies((jj + W_SLOTS - 1) % IN_GROUPS):
                copy.start()

        @pl.when(j == jj)
        def _():
            for copy in w_copies(jj):
                copy.wait()

    def proj(jj, t, h=None):
        h = h_ref[...] if h is None else h
        return _dot(h, wbuf[jj % W_SLOTS, :, t * COL_TILE:(t + 1) * COL_TILE])

    def rotary_heads(acc, dec_ref, post_scale):
        plain, scaled = [], []
        for h in range(N_HEADS):
            cols = slice(h * DK, (h + 1) * DK)
            t = acc[:, cols]
            t3 = t.reshape(tm // n_pos, n_pos, DK)
            r3 = pltpu.roll(t, DK // 2, 1).reshape(tm // n_pos, n_pos, DK)
            r = (t3 * cos_ref[...][None] + r3 * sin_ref[...][None]).reshape(tm, DK)
            if post_scale != 1.0:
                r = r * post_scale
            plain.append(r.astype(BF16))
            d = r.reshape(tm // chunk, chunk, DK) * dec_ref[:, cols][None]
            scaled.append(d.reshape(tm, DK).astype(BF16))
        return jnp.concatenate(plain + scaled, axis=1)

    @pl.when(j == 0)
    def _():
        h = _rms(x_ref[...], g_ref[...]).astype(BF16)
        h_ref[...] = h
        a_ref[...] = rotary_heads(proj(0, 0, h), qd_ref, DK ** -0.5)
        b_ref[...] = proj(0, 1, h).astype(BF16)

    @pl.when(j == 1)
    def _():
        a_ref[...] = rotary_heads(proj(1, 0), kd_ref, 1.0)
        b_ref[...] = proj(1, 1).astype(BF16)

    @pl.when(j == 2)
    def _():
        f_ref[...] = jax.nn.silu(proj(2, 0))
        cx_ref[...] = proj(2, 1)

    @pl.when(j == 3)
    def _():
        f_ref[...] = jax.nn.sigmoid(proj(3, 0))

    @pl.when(j == 4)
    def _():
        f_ref[...] = jax.nn.silu(proj(4, 0))
        cb_ref[...] = proj(4, 1)

    @pl.when(j == 5)
    def _():
        f_ref[...] = jax.nn.sigmoid(proj(5, 1))
        u = proj(5, 0) * cx_ref[...]
        def patch(pos, prev2, prev1, r1, r2):
            return (jnp.where(pos == 0, prev1, r1),
                    jnp.where(pos == 0, prev2, jnp.where(pos == 1, prev1, r2)))

        if has_init:
            n = tm // seq_len
            prev2 = jnp.broadcast_to(st_ref[:, 0:1, :], (n, seq_len, u.shape[1])).reshape(u.shape)
            prev1 = jnp.broadcast_to(st_ref[:, 1:2, :], (n, seq_len, u.shape[1])).reshape(u.shape)
            u1, u2 = patch(lax.broadcasted_iota(jnp.int32, u.shape, 0) % seq_len, prev2, prev1,
                           pltpu.roll(u, 1, 0), pltpu.roll(u, 2, 0))
        else:
            s = V7X_SUBLANES
            first = (i % (seq_len // tm)) == 0
            halo = jnp.where(first, 0.0, carry_ref[...])
            prev2 = jnp.broadcast_to(halo[s - 2:s - 1, :], halo.shape)
            prev1 = jnp.broadcast_to(halo[s - 1:s, :], halo.shape)
            r1, r2 = pltpu.roll(u, 1, 0), pltpu.roll(u, 2, 0)
            h1, h2 = patch(lax.broadcasted_iota(jnp.int32, halo.shape, 0), prev2, prev1, r1[:s], r2[:s])
            u1 = jnp.concatenate([h1, r1[s:]], axis=0)
            u2 = jnp.concatenate([h2, r2[s:]], axis=0)
        cv = cw_ref[0:1, :] * u2 + cw_ref[1:2, :] * u1 + cw_ref[2:3, :] * u
        b_ref[...] = (cb_ref[...] * cv).astype(BF16)
        if has_init:
            cs_ref[...] = u.reshape(tm // seq_len, seq_len, u.shape[1])[:, seq_len - (CONV_W - 1):, :]
        else:
            carry_ref[...] = u[tm - V7X_SUBLANES:, :]
            cs_ref[0] = u[tm - (CONV_W - 1):, :]


def _inproj(x2d, g, w, cw, conv_state, cos2, sin2, qd, kd, *, n_batch, seq_len, chunk, tm, cast=()):
    T, D = x2d.shape
    has_init = conv_state is not None
    assert w.shape[1] == IN_TILES * COL_TILE and D == COL_TILE and IN_GROUPS % W_SLOTS == 0
    if has_init:
        assert tm % seq_len == 0
        n = tm // seq_len
        pos_spec = pl.BlockSpec((seq_len, DK), lambda i, j: (0, 0))
        st_specs = [pl.BlockSpec((n, CONV_W - 1, D), lambda i, j: (i, 0, 0))]
        st_args = [conv_state]
        cs_spec = pl.BlockSpec((n, CONV_W - 1, D), lambda i, j: (i, 0, 0))
    else:
        assert seq_len % tm == 0 and tm % chunk == 0
        tiles = seq_len // tm
        pos_spec = pl.BlockSpec((tm, DK), lambda i, j: (i % tiles, 0))
        st_specs, st_args = [], []
        cs_spec = pl.BlockSpec((1, CONV_W - 1, D), lambda i, j: (i // tiles, 0, 0))

    def b_col(j):
        return jnp.where(j == 0, 0, jnp.where(j < 5, 1, 2))

    def f_col(j):
        return jnp.where(j <= 2, 0, jnp.where(j == 3, 3, j - 3))

    n_steps = (T // tm) * IN_GROUPS
    cast_specs_in, cast_specs_out, cast_shapes, cast_ranges = [], [], [], []
    first = 0
    for m in cast:
        rows = CAST_CHUNK_BYTES // (m.shape[1] * m.dtype.itemsize)
        n_chunks = m.shape[0] // rows
        assert m.shape[0] % rows == 0 and first + n_chunks <= n_steps

        def chunk_map(i, j, first=first, n_chunks=n_chunks):
            return jnp.clip(i * IN_GROUPS + j - first, 0, n_chunks - 1), 0

        cast_specs_in.append(pl.BlockSpec((rows, m.shape[1]), chunk_map))
        cast_specs_out.append(pl.BlockSpec((rows, m.shape[1]), chunk_map))
        cast_shapes.append(jax.ShapeDtypeStruct(m.shape, BF16))
        cast_ranges.append((first, first + n_chunks))
        first += n_chunks

    kernel = functools.partial(_inproj_kernel, tm=tm, chunk=chunk, seq_len=seq_len, has_init=has_init,
                               cast_ranges=tuple(cast_ranges))
    return pl.pallas_call(
        kernel,
        grid=(T // tm, IN_GROUPS),
        in_specs=[pl.BlockSpec((tm, D), lambda i, j: (i, 0)),
                  pl.BlockSpec((1, D), lambda i, j: (0, 0)),
                  pl.BlockSpec(memory_space=pl.ANY),
                  pos_spec, pos_spec,
                  _resident(qd.shape), _resident(kd.shape), _resident(cw.shape)] + st_specs + cast_specs_in,
        out_specs=[pl.BlockSpec((tm, 2 * COL_TILE), lambda i, j: (i, jnp.minimum(j, 1))),
                   pl.BlockSpec((tm, COL_TILE), lambda i, j: (i, b_col(j))),
                   pl.BlockSpec((tm, COL_TILE), lambda i, j: (i, f_col(j))),
                   cs_spec] + cast_specs_out,
        out_shape=[jax.ShapeDtypeStruct((T, 4 * COL_TILE), BF16),
                   jax.ShapeDtypeStruct((T, 3 * COL_TILE), BF16),
                   jax.ShapeDtypeStruct((T, 4 * COL_TILE), F32),
                   jax.ShapeDtypeStruct((n_batch, CONV_W - 1, D), F32)] + cast_shapes,
        scratch_shapes=[pltpu.VMEM((tm, D), BF16), pltpu.VMEM((tm, D), F32), pltpu.VMEM((tm, D), F32),
                        pltpu.VMEM((V7X_SUBLANES, D), F32),
                        pltpu.VMEM((W_SLOTS, D, 2 * COL_TILE), BF16),
                        pltpu.SemaphoreType.DMA((W_SLOTS, 2))],
        compiler_params=_params(("arbitrary", "arbitrary")),
        name="inproj_init" if has_init else "inproj",
    )(x2d, g, w, cos2, sin2, qd, kd, cw, *st_args, *cast)


N_RET_INPUTS = 5


def _retention_body(a_ref, b_ref, f_ref, dec_ref, cd_ref, s0_ref, s_ref, o_ref, *, chunk, n_chunks, seqs):
    has_init = s0_ref is not None
    fuse = chunk % DK == 0
    q_ref, qd_ref, k_ref, kd_ref = (a_ref.at[:, pl.ds(t * COL_TILE, COL_TILE)] for t in range(4))
    sg_ref = f_ref

    def v_at(rows, h):
        return b_ref[rows, pl.ds(h * DV, DV)]

    for seq in seqs:
        row_slices = [pl.ds((seq * n_chunks + c) * chunk, chunk) for c in range(n_chunks)]
        probs, kvs = {}, {}
        for c, rows in enumerate(row_slices):
            for h in range(N_HEADS):
                kcols = pl.ds(h * DK, DK)
                scores = lax.dot_general(q_ref[rows, kcols], k_ref[rows, kcols], (((1,), (1,)), ((), ())),
                                         preferred_element_type=F32) * dec_ref[h]
                probs[c, h] = scores.astype(BF16)
                kdt = jnp.transpose(kd_ref[rows, kcols].astype(F32)).astype(BF16)
                kvs[c, h] = _dot(kdt, v_at(rows, h))
        if has_init:
            states = [s0_ref[seq, h] for h in range(N_HEADS)]
        else:
            states = [s_ref[seq, h] for h in range(N_HEADS)]
        for c, rows in enumerate(row_slices):
            for h in range(N_HEADS):
                kcols = pl.ds(h * DK, DK)
                ocols = pl.ds(h * DV, DV)
                S = states[h]
                if fuse:
                    o = _dot(jnp.concatenate([probs[c, h], qd_ref[rows, kcols]], axis=1),
                             jnp.concatenate([v_at(rows, h), S.astype(BF16)], axis=0))
                else:
                    o = _dot(probs[c, h], v_at(rows, h)) + _dot(qd_ref[rows, kcols], S.astype(BF16))
                states[h] = S * cd_ref[h] + kvs[c, h]
                o = o * lax.rsqrt(jnp.mean(o * o, axis=-1, keepdims=True) + EPS)
                o_ref[rows, ocols] = (o * sg_ref[rows, ocols]).astype(BF16)
        for h in range(N_HEADS):
            s_ref[seq, h] = states[h]


def _retention_specs(a, b, f, decay, cd, rows, row_block):
    def whole(arr):
        return pl.BlockSpec((rows, arr.shape[1]), lambda *g: (row_block(*g), 0))

    specs = [whole(a), whole(b), whole(f), _resident(decay.shape), pl.BlockSpec(memory_space=pltpu.SMEM)]
    return specs, [a, b, f, decay, cd]


def _retention_mix_kernel(a_ref, b_ref, f_ref, dec_ref, cd_ref, wro_ref, wco_ref, wo_ref,
                          d_ref, s_ref, o_ref, *, chunk, n_chunks):
    @pl.when(pl.program_id(1) == 0)
    def _():
        s_ref[...] = jnp.zeros_like(s_ref)

    _retention_body(a_ref, b_ref, f_ref, dec_ref, cd_ref, None, s_ref, o_ref,
                    chunk=chunk, n_chunks=n_chunks, seqs=range(1))
    cbv = b_ref[:, 2 * COL_TILE:]
    sig_a, sig_b = f_ref[:, 2 * COL_TILE:3 * COL_TILE], f_ref[:, 3 * COL_TILE:]
    y_conv = _dot(cbv, wco_ref[...])
    merged = sig_a * _dot(o_ref[...], wro_ref[...]) + sig_b * y_conv
    d_ref[...] = _dot(merged.astype(BF16), wo_ref[...])


def _retention_mix(a, b, f, decay, cd, wro, wco, wo, *, n_batch, seq_len, chunk, n_chunks):
    T, D = a.shape[0], wo.shape[1]
    rows = n_chunks * chunk
    steps = seq_len // rows

    def row_block(b_, t):
        return b_ * steps + t

    def tile(width, col):
        return pl.BlockSpec((rows, width), lambda b_, t: (row_block(b_, t), col))

    in_specs, args = _retention_specs(a, b, f, decay, cd, rows, row_block)
    in_specs += [_resident(wro.shape), _resident(wco.shape), _resident(wo.shape)]
    args += [wro, wco, wo]
    kernel = functools.partial(_retention_mix_kernel, chunk=chunk, n_chunks=n_chunks)
    return pl.pallas_call(
        kernel,
        grid=(n_batch, steps),
        in_specs=in_specs,
        out_specs=[tile(D, 0), pl.BlockSpec((1, N_HEADS, DK, DV), lambda b_, t: (b_, 0, 0, 0))],
        out_shape=[jax.ShapeDtypeStruct((T, D), F32),
                   jax.ShapeDtypeStruct((n_batch, N_HEADS, DK, DV), F32)],
        scratch_shapes=[pltpu.VMEM((rows, N_HEADS * DV), BF16)],
        compiler_params=_params(("parallel", "arbitrary")),
        name="retention_mix",
    )(*args)


def _mix_kernel(x_ref, og_ref, cbv_ref, sa_ref, sb_ref, wro_ref, wco_ref, wo_ref, x1_ref):
    y_ret = _dot(og_ref[...], wro_ref[...])
    y_conv = _dot(cbv_ref[...], wco_ref[...])
    merged = sa_ref[...] * y_ret + sb_ref[...] * y_conv
    x1_ref[...] = x_ref[...] + _dot(merged.astype(BF16), wo_ref[...])


def _mix(x2d, og, b, f, wro, wco, wo, *, tm):
    T, D = x2d.shape

    def tile(width, col):
        return pl.BlockSpec((tm, width), lambda i: (i, col))

    return pl.pallas_call(
        _mix_kernel,
        grid=(T // tm,),
        in_specs=[tile(D, 0), tile(N_HEADS * DV, 0), tile(D, 2), tile(D, 2), tile(D, 3),
                  _resident(wro.shape), _resident(wco.shape), _resident(wo.shape)],
        out_specs=tile(D, 0),
        out_shape=jax.ShapeDtypeStruct((T, D), F32),
        compiler_params=_params(("parallel",)),
        name="mix",
    )(x2d, og, b, f, f, wro, wco, wo)


N_FFN_INPUTS = 9


def _ffn_kernel(*refs, final_norm, side):
    x_ref, p_ref, gf_ref, wu_ref, wd_ref, gp_ref, wpg_ref, wpp_ref, gl_ref = refs[:N_FFN_INPUTS]
    if side is None:
        o_ref, = refs[N_FFN_INPUTS:]
        x = x_ref[...]
    else:
        ret_refs = refs[N_FFN_INPUTS:N_FFN_INPUTS + N_RET_INPUTS]
        s0_ref, base_ref, o_ref, og_ref, s_ref = refs[N_FFN_INPUTS + N_RET_INPUTS:]
        chunk, n_seq = side
        x = base_ref[...] + x_ref[...]
    hf = _rms(x, gf_ref[...]).astype(BF16)
    n_ff = wu_ref.shape[1] // COL_TILE
    acc = x
    for c in range(n_ff):
        cols = pl.ds(c * COL_TILE, COL_TILE)
        hid = jnp.square(jnp.maximum(_dot(hf, wu_ref[:, cols]), 0.0)).astype(BF16)
        acc = acc + _dot(hid, wd_ref[cols, :])
        if side is not None:
            _retention_body(*ret_refs, s0_ref, s_ref, og_ref, chunk=chunk, n_chunks=1,
                            seqs=range(c * n_seq // n_ff, (c + 1) * n_seq // n_ff))
    gate = jax.nn.sigmoid(_dot(_rms(acc, gp_ref[...]).astype(BF16), wpg_ref[...]))
    y = acc + gate * _dot(p_ref[...].astype(BF16), wpp_ref[...])
    if final_norm:
        y = _rms(y, gl_ref[...])
    o_ref[...] = y


def _ffn(x2d, p2d, gf, wu, wd, gp, wpg, wpp, gl, *, tm, final_norm, side=None):
    T, D = x2d.shape
    steps = T // tm
    in_specs = [pl.BlockSpec((tm, D), lambda i: (i, 0)),
                pl.BlockSpec((tm, p2d.shape[1]), lambda i: (i, 0)),
                _resident(gf.shape), _resident(wu.shape), _resident(wd.shape),
                _resident(gp.shape), _resident(wpg.shape), _resident(wpp.shape),
                _resident(gl.shape)]
    args = [x2d, p2d, gf, wu, wd, gp, wpg, wpp, gl]
    out_specs = [pl.BlockSpec((tm, D), lambda i: (i, 0))]
    out_shape = [jax.ShapeDtypeStruct((T, D), F32)]
    kernel_side = None
    if side is not None:
        base, a, b, f, s0, decay, cd, chunk = side
        n_all = s0.shape[0]
        assert n_all % steps == 0 and a.shape[0] == n_all * chunk
        n_seq = n_all // steps
        ret_specs, ret_args = _retention_specs(a, b, f, decay, cd, n_seq * chunk, lambda i: i)
        state_spec = pl.BlockSpec((n_seq,) + s0.shape[1:], lambda i: (i, 0, 0, 0))
        in_specs += ret_specs + [state_spec, pl.BlockSpec((tm, D), lambda i: (i, 0))]
        args += ret_args + [s0, base]
        out_specs += [pl.BlockSpec((n_seq * chunk, N_HEADS * DV), lambda i: (i, 0)), state_spec]
        out_shape += [jax.ShapeDtypeStruct((a.shape[0], N_HEADS * DV), BF16),
                      jax.ShapeDtypeStruct(s0.shape, F32)]
        kernel_side = (chunk, n_seq)
    kernel = functools.partial(_ffn_kernel, final_norm=final_norm, side=kernel_side)
    out = pl.pallas_call(
        kernel,
        grid=(steps,),
        in_specs=in_specs,
        out_specs=out_specs,
        out_shape=out_shape,
        compiler_params=_params(("parallel",)),
        name="ffn" if side is None else "ffn_retention",
    )(*args)
    return out[0] if side is None else out


def _layer(xp, xs, pp, ps, s_ret, s_conv, pos_p, pos_s, w, *, n_prompt, len_prompt, chunk_prompt,
           n_sample, len_sample, final_norm):
    cos_p, sin_p, decay_p, qd_p, kd_p, cd_p = _tables(pos_p, chunk_prompt)
    cos_s, sin_s, decay_s, qd_s, kd_s, cd_s = _tables(pos_s, len_sample)
    ap, bp, fp, conv_p, w_up, w_down = _inproj(
        xp, w["g_mix"], w["w_in"], w["conv_w"], None, cos_p, sin_p, qd_p, kd_p, n_batch=n_prompt,
        seq_len=len_prompt, chunk=chunk_prompt, tm=1024, cast=(w["w_up"], w["w_down"]))
    ffn_w = (w["g_ffn"], w_up, w_down, w["g_ple"], w["w_ple_gate"], w["w_ple_proj"], w["g_final"])
    mix_w = (w["w_ret_out"], w["w_conv_out"], w["w_o"])
    as_, bs, fs, conv_s = _inproj(xs, w["g_mix"], w["w_in"], w["conv_w"], s_conv, cos_s, sin_s, qd_s, kd_s,
                                  n_batch=n_sample, seq_len=len_sample, chunk=len_sample, tm=512)
    dxp, ret_p = _retention_mix(ap, bp, fp, decay_p, cd_p, *mix_w, n_batch=n_prompt,
                                seq_len=len_prompt, chunk=chunk_prompt, n_chunks=4)
    yp, og_s, ret_s = _ffn(dxp, pp, *ffn_w, tm=512, final_norm=final_norm,
                           side=(xp, as_, bs, fs, s_ret, decay_s, cd_s, len_sample))
    x1s = _mix(xs, og_s, bs, fs, *mix_w, tm=512)
    ys = _ffn(x1s, ps, *ffn_w, tm=512, final_norm=final_norm)
    return yp, ys, ret_p, conv_p, ret_s, conv_s


def kernel(x_prompt, x_sample, state_ret, state_conv, p_prompt, p_sample, g_mix, w_in, w_ret_out,
           conv_w, w_conv_out, w_o, g_ffn, w_up, w_down, g_ple, w_ple_gate, w_ple_proj, g_final):
    depth = w_in.shape[0]
    B, L, D = x_prompt.shape
    Bs, Ls, _ = x_sample.shape
    pos_prompt = jnp.arange(L, dtype=F32)
    pos_sample = PAST_LEN + jnp.arange(Ls, dtype=F32)
    chunk_prompt = min(RET_CHUNK, L)

    hp = x_prompt.reshape(B * L, D)
    hs = x_sample.reshape(Bs * Ls, D)
    rp, cp, rs, cs = [], [], [], []
    for i in range(depth):
        w = {"g_mix": g_mix[i][None], "w_in": w_in[i].astype(BF16),
             "w_ret_out": w_ret_out[i].astype(BF16), "conv_w": conv_w[i],
             "w_conv_out": w_conv_out[i].astype(BF16), "w_o": w_o[i].astype(BF16),
             "g_ffn": g_ffn[i][None], "w_up": w_up[i], "w_down": w_down[i],
             "g_ple": g_ple[i][None], "w_ple_gate": w_ple_gate[i].astype(BF16),
             "w_ple_proj": w_ple_proj[i].astype(BF16), "g_final": g_final[None]}
        hp, hs, r_p, c_p, r_s, c_s = _layer(
            hp, hs, p_prompt[i].reshape(B * L, -1), p_sample[i].reshape(Bs * Ls, -1),
            state_ret[i], state_conv[i], pos_prompt, pos_sample, w,
            n_prompt=B, len_prompt=L, chunk_prompt=chunk_prompt, n_sample=Bs, len_sample=Ls,
            final_norm=i == depth - 1)
        rp.append(r_p)
        cp.append(c_p)
        rs.append(r_s)
        cs.append(c_s)
    return (hp.reshape(B, L, D), hs.reshape(Bs, Ls, D), jnp.stack(rp), jnp.stack(cp),
            jnp.stack(rs), jnp.stack(cs))
```

```python
import functools

import jax
import jax.numpy as jnp
from jax import lax
from jax.experimental import pallas as pl
from jax.experimental.pallas import tpu as pltpu

F32 = jnp.float32
BF16 = jnp.bfloat16

N_HEADS = 8
DK = 128
DV = 256
CONV_W = 3
RET_CHUNK = 128
PAST_LEN = 16384
ROPE_BASE = 10000.0
EPS = 1e-6

V7X_SUBLANES = 8
V7X_VMEM_BYTES = 64 * 1024 * 1024
V7X_VMEM_LIMIT_BYTES = V7X_VMEM_BYTES - 4 * 1024 * 1024

COL_TILE = 1024

IN_GROUP_TILES = ((0, 2), (1, 3), (4, 6), (10,), (5, 7), (8, 9))
IN_GROUPS = len(IN_GROUP_TILES)
IN_TILES = sum(len(tiles) for tiles in IN_GROUP_TILES)
W_SLOTS = 3
CAST_CHUNK_BYTES = 512 * 1024


def _rms(x, g):
    return x * lax.rsqrt(jnp.mean(x * x, axis=-1, keepdims=True) + EPS) * g


def _dot(a, b):
    return jnp.dot(a, b, preferred_element_type=F32)


def _params(semantics):
    return pltpu.CompilerParams(dimension_semantics=semantics,
                                vmem_limit_bytes=V7X_VMEM_LIMIT_BYTES)


def _resident(shape):
    zeros = (0,) * len(shape)
    return pl.BlockSpec(shape, lambda *_: zeros, pipeline_mode=pl.Buffered(1))


def _tables(pos, chunk):
    inv_freq = ROPE_BASE ** (-jnp.arange(0, DK, 2, dtype=F32) / DK)
    ang = pos.astype(F32)[:, None] * inv_freq[None, :]
    c, s = jnp.cos(ang), jnp.sin(ang)
    cos2 = jnp.concatenate([c, c], axis=-1)
    sin2 = jnp.concatenate([-s, s], axis=-1)
    log_g = jnp.log(1.0 - 2.0 ** (-5.0 - jnp.arange(N_HEADS, dtype=F32)))
    idx = jnp.arange(chunk, dtype=F32)
    diff = idx[:, None] - idx[None, :]
    decay = jnp.where(diff >= 0, jnp.exp(jnp.maximum(diff, 0.0)[None] * log_g[:, None, None]), 0.0)
    q_decay = jnp.exp((idx + 1.0)[None, :] * log_g[:, None])
    k_decay = jnp.exp((chunk - 1.0 - idx)[None, :] * log_g[:, None])
    chunk_decay = jnp.exp(chunk * log_g)
    qd = jnp.repeat(q_decay.T, DK, axis=1)
    kd = jnp.repeat(k_decay.T, DK, axis=1)
    return cos2, sin2, decay, qd, kd, chunk_decay


def _inproj_kernel(*refs, tm, chunk, seq_len, has_init, cast_ranges):
    refs = list(refs)
    x_ref, g_ref, w_hbm, cos_ref, sin_ref, qd_ref, kd_ref, cw_ref = refs[:8]
    del refs[:8]
    st_ref = refs.pop(0) if has_init else None
    n_cast = len(cast_ranges)
    cast_in = refs[:n_cast]
    a_ref, b_ref, f_ref, cs_ref = refs[n_cast:n_cast + 4]
    cast_out = refs[n_cast + 4:2 * n_cast + 4]
    h_ref, cx_ref, cb_ref, carry_ref, wbuf, wsem = refs[2 * n_cast + 4:]
    i = pl.program_id(0)
    j = pl.program_id(1)
    n_pos = cos_ref.shape[0]

    step = i * IN_GROUPS + j
    n_steps = pl.num_programs(0) * IN_GROUPS

    for src, dst, (first, last) in zip(cast_in, cast_out, cast_ranges):
        @pl.when((step >= first) & (step < last))
        def _():
            dst[...] = src[...].astype(BF16)

    def w_copies(group):
        slot = group % W_SLOTS
        return [pltpu.make_async_copy(w_hbm.at[:, pl.ds(tile * COL_TILE, COL_TILE)],
                                      wbuf.at[slot, :, pl.ds(t * COL_TILE, COL_TILE)],
                                      wsem.at[slot, t])
                for t, tile in enumerate(IN_GROUP_TILES[group])]

    @pl.when(step == 0)
    def _():
        for group in range(W_SLOTS - 1):
            for copy in w_copies(group):
                copy.start()

    for jj in range(IN_GROUPS):
        @pl.when((j == jj) & (step + W_SLOTS - 1 < n_steps))
        def _():
            for copy in w_copies((jj + W_SLOTS - 1) % IN_GROUPS):
                copy.start()

        @pl.when(j == jj)
        def _():
            for copy in w_copies(jj):
                copy.wait()

    def proj(jj, t, h=None):
        h = h_ref[...] if h is None else h
        return _dot(h, wbuf[jj % W_SLOTS, :, t * COL_TILE:(t + 1) * COL_TILE])

    def rotary_heads(acc, dec_ref, post_scale):
        plain, scaled = [], []
        for h in range(N_HEADS):
            cols = slice(h * DK, (h + 1) * DK)
            t = acc[:, cols]
            t3 = t.reshape(tm // n_pos, n_pos, DK)
            r3 = pltpu.roll(t, DK // 2, 1).reshape(tm // n_pos, n_pos, DK)
            r = (t3 * cos_ref[...][None] + r3 * sin_ref[...][None]).reshape(tm, DK)
            if post_scale != 1.0:
                r = r * post_scale
            plain.append(r.astype(BF16))
            d = r.reshape(tm // chunk, chunk, DK) * dec_ref[:, cols][None]
            scaled.append(d.reshape(tm, DK).astype(BF16))
        return jnp.concatenate(plain + scaled, axis=1)

    @pl.when(j == 0)
    def _():
        h = _rms(x_ref[...], g_ref[...]).astype(BF16)
        h_ref[...] = h
        a_ref[...] = rotary_heads(proj(0, 0, h), qd_ref, DK ** -0.5)
        b_ref[...] = proj(0, 1, h).astype(BF16)

    @pl.when(j == 1)
    def _():
        a_ref[...] = rotary_heads(proj(1, 0), kd_ref, 1.0)
        b_ref[...] = proj(1, 1).astype(BF16)

    @pl.when(j == 2)
    def _():
        f_ref[...] = jax.nn.silu(proj(2, 0))
        cx_ref[...] = proj(2, 1)

    @pl.when(j == 3)
    def _():
        f_ref[...] = jax.nn.sigmoid(proj(3, 0))

    @pl.when(j == 4)
    def _():
        f_ref[...] = jax.nn.silu(proj(4, 0))
        cb_ref[...] = proj(4, 1)

    @pl.when(j == 5)
    def _():
        f_ref[...] = jax.nn.sigmoid(proj(5, 1))
        u = proj(5, 0) * cx_ref[...]
        def patch(pos, prev2, prev1, r1, r2):
            return (jnp.where(pos == 0, prev1, r1),
                    jnp.where(pos == 0, prev2, jnp.where(pos == 1, prev1, r2)))

        if has_init:
            n = tm // seq_len
            prev2 = jnp.broadcast_to(st_ref[:, 0:1, :], (n, seq_len, u.shape[1])).reshape(u.shape)
            prev1 = jnp.broadcast_to(st_ref[:, 1:2, :], (n, seq_len, u.shape[1])).reshape(u.shape)
            u1, u2 = patch(lax.broadcasted_iota(jnp.int32, u.shape, 0) % seq_len, prev2, prev1,
                           pltpu.roll(u, 1, 0), pltpu.roll(u, 2, 0))
        else:
            s = V7X_SUBLANES
            first = (i % (seq_len // tm)) == 0
            halo = jnp.where(first, 0.0, carry_ref[...])
            prev2 = jnp.broadcast_to(halo[s - 2:s - 1, :], halo.shape)
            prev1 = jnp.broadcast_to(halo[s - 1:s, :], halo.shape)
            r1, r2 = pltpu.roll(u, 1, 0), pltpu.roll(u, 2, 0)
            h1, h2 = patch(lax.broadcasted_iota(jnp.int32, halo.shape, 0), prev2, prev1, r1[:s], r2[:s])
            u1 = jnp.concatenate([h1, r1[s:]], axis=0)
            u2 = jnp.concatenate([h2, r2[s:]], axis=0)
        cv = cw_ref[0:1, :] * u2 + cw_ref[1:2, :] * u1 + cw_ref[2:3, :] * u
        b_ref[...] = (cb_ref[...] * cv).astype(BF16)
        if has_init:
            cs_ref[...] = u.reshape(tm // seq_len, seq_len, u.shape[1])[:, seq_len - (CONV_W - 1):, :]
        else:
            carry_ref[...] = u[tm - V7X_SUBLANES:, :]
            cs_ref[0] = u[tm - (CONV_W - 1):, :]


def _inproj(x2d, g, w, cw, conv_state, cos2, sin2, qd, kd, *, n_batch, seq_len, chunk, tm, cast=(),
            cast_chunk_bytes=CAST_CHUNK_BYTES):
    T, D = x2d.shape
    has_init = conv_state is not None
    assert w.shape[1] == IN_TILES * COL_TILE and D == COL_TILE and IN_GROUPS % W_SLOTS == 0
    if has_init:
        assert tm % seq_len == 0
        n = tm // seq_len
        pos_spec = pl.BlockSpec((seq_len, DK), lambda i, j: (0, 0))
        st_specs = [pl.BlockSpec((n, CONV_W - 1, D), lambda i, j: (i, 0, 0))]
        st_args = [conv_state]
        cs_spec = pl.BlockSpec((n, CONV_W - 1, D), lambda i, j: (i, 0, 0))
    else:
        assert seq_len % tm == 0 and tm % chunk == 0
        tiles = seq_len // tm
        pos_spec = pl.BlockSpec((tm, DK), lambda i, j: (i % tiles, 0))
        st_specs, st_args = [], []
        cs_spec = pl.BlockSpec((1, CONV_W - 1, D), lambda i, j: (i // tiles, 0, 0))

    def b_col(j):
        return jnp.where(j == 0, 0, jnp.where(j < 5, 1, 2))

    def f_col(j):
        return jnp.where(j <= 2, 0, jnp.where(j == 3, 3, j - 3))

    n_steps = (T // tm) * IN_GROUPS
    cast_specs_in, cast_specs_out, cast_shapes, cast_ranges = [], [], [], []
    first = 0
    for m in cast:
        rows = cast_chunk_bytes // (m.shape[1] * m.dtype.itemsize)
        n_chunks = m.shape[0] // rows
        assert m.shape[0] % rows == 0 and n_chunks <= n_steps
        if first + n_chunks > n_steps:
            first = 0

        def chunk_map(i, j, first=first, n_chunks=n_chunks):
            return jnp.clip(i * IN_GROUPS + j - first, 0, n_chunks - 1), 0

        cast_specs_in.append(pl.BlockSpec((rows, m.shape[1]), chunk_map))
        cast_specs_out.append(pl.BlockSpec((rows, m.shape[1]), chunk_map))
        cast_shapes.append(jax.ShapeDtypeStruct(m.shape, BF16))
        cast_ranges.append((first, first + n_chunks))
        first += n_chunks

    kernel = functools.partial(_inproj_kernel, tm=tm, chunk=chunk, seq_len=seq_len, has_init=has_init,
                               cast_ranges=tuple(cast_ranges))
    return pl.pallas_call(
        kernel,
        grid=(T // tm, IN_GROUPS),
        in_specs=[pl.BlockSpec((tm, D), lambda i, j: (i, 0)),
                  pl.BlockSpec((1, D), lambda i, j: (0, 0)),
                  pl.BlockSpec(memory_space=pl.ANY),
                  pos_spec, pos_spec,
                  _resident(qd.shape), _resident(kd.shape), _resident(cw.shape)] + st_specs + cast_specs_in,
        out_specs=[pl.BlockSpec((tm, 2 * COL_TILE), lambda i, j: (i, jnp.minimum(j, 1))),
                   pl.BlockSpec((tm, COL_TILE), lambda i, j: (i, b_col(j))),
                   pl.BlockSpec((tm, COL_TILE), lambda i, j: (i, f_col(j))),
                   cs_spec] + cast_specs_out,
        out_shape=[jax.ShapeDtypeStruct((T, 4 * COL_TILE), BF16),
                   jax.ShapeDtypeStruct((T, 3 * COL_TILE), BF16),
                   jax.ShapeDtypeStruct((T, 4 * COL_TILE), F32),
                   jax.ShapeDtypeStruct((n_batch, CONV_W - 1, D), F32)] + cast_shapes,
        scratch_shapes=[pltpu.VMEM((tm, D), BF16), pltpu.VMEM((tm, D), F32), pltpu.VMEM((tm, D), F32),
                        pltpu.VMEM((V7X_SUBLANES, D), F32),
                        pltpu.VMEM((W_SLOTS, D, 2 * COL_TILE), BF16),
                        pltpu.SemaphoreType.DMA((W_SLOTS, 2))],
        compiler_params=_params(("arbitrary", "arbitrary")),
        name="inproj_init" if has_init else "inproj",
    )(x2d, g, w, cos2, sin2, qd, kd, cw, *st_args, *cast)


N_RET_INPUTS = 5


def _retention_body(a_ref, b_ref, f_ref, dec_ref, cd_ref, s0_ref, s_ref, o_ref, *, chunk, n_chunks, seqs):
    has_init = s0_ref is not None
    fuse = chunk % DK == 0
    q_ref, qd_ref, k_ref, kd_ref = (a_ref.at[:, pl.ds(t * COL_TILE, COL_TILE)] for t in range(4))
    sg_ref = f_ref

    def v_at(rows, h):
        return b_ref[rows, pl.ds(h * DV, DV)]

    for seq in seqs:
        row_slices = [pl.ds((seq * n_chunks + c) * chunk, chunk) for c in range(n_chunks)]
        probs, kvs = {}, {}
        for c, rows in enumerate(row_slices):
            for h in range(N_HEADS):
                kcols = pl.ds(h * DK, DK)
                scores = lax.dot_general(q_ref[rows, kcols], k_ref[rows, kcols], (((1,), (1,)), ((), ())),
                                         preferred_element_type=F32) * dec_ref[h]
                probs[c, h] = scores.astype(BF16)
                kdt = jnp.transpose(kd_ref[rows, kcols].astype(F32)).astype(BF16)
                kvs[c, h] = _dot(kdt, v_at(rows, h))
        if has_init:
            states = [s0_ref[seq, h] for h in range(N_HEADS)]
        else:
            states = [s_ref[seq, h] for h in range(N_HEADS)]
        for c, rows in enumerate(row_slices):
            for h in range(N_HEADS):
                kcols = pl.ds(h * DK, DK)
                ocols = pl.ds(h * DV, DV)
                S = states[h]
                if fuse:
                    o = _dot(jnp.concatenate([probs[c, h], qd_ref[rows, kcols]], axis=1),
                             jnp.concatenate([v_at(rows, h), S.astype(BF16)], axis=0))
                else:
                    o = _dot(probs[c, h], v_at(rows, h)) + _dot(qd_ref[rows, kcols], S.astype(BF16))
                states[h] = S * cd_ref[h] + kvs[c, h]
                o = o * lax.rsqrt(jnp.mean(o * o, axis=-1, keepdims=True) + EPS)
                o_ref[rows, ocols] = (o * sg_ref[rows, ocols]).astype(BF16)
        for h in range(N_HEADS):
            s_ref[seq, h] = states[h]


def _retention_specs(a, b, f, decay, cd, rows, row_block):
    def whole(arr):
        return pl.BlockSpec((rows, arr.shape[1]), lambda *g: (row_block(*g), 0))

    specs = [whole(a), whole(b), whole(f), _resident(decay.shape), pl.BlockSpec(memory_space=pltpu.SMEM)]
    return specs, [a, b, f, decay, cd]


def _retention_mix_kernel(a_ref, b_ref, f_ref, dec_ref, cd_ref, wro_ref, wco_ref, wo_ref,
                          d_ref, s_ref, o_ref, *, chunk, n_chunks):
    @pl.when(pl.program_id(1) == 0)
    def _():
        s_ref[...] = jnp.zeros_like(s_ref)

    _retention_body(a_ref, b_ref, f_ref, dec_ref, cd_ref, None, s_ref, o_ref,
                    chunk=chunk, n_chunks=n_chunks, seqs=range(1))
    cbv = b_ref[:, 2 * COL_TILE:]
    sig_a, sig_b = f_ref[:, 2 * COL_TILE:3 * COL_TILE], f_ref[:, 3 * COL_TILE:]
    y_conv = _dot(cbv, wco_ref[...])
    merged = sig_a * _dot(o_ref[...], wro_ref[...]) + sig_b * y_conv
    d_ref[...] = _dot(merged.astype(BF16), wo_ref[...])


def _retention_mix(a, b, f, decay, cd, wro, wco, wo, *, n_batch, seq_len, chunk, n_chunks):
    T, D = a.shape[0], wo.shape[1]
    rows = n_chunks * chunk
    steps = seq_len // rows

    def row_block(b_, t):
        return b_ * steps + t

    def tile(width, col):
        return pl.BlockSpec((rows, width), lambda b_, t: (row_block(b_, t), col))

    in_specs, args = _retention_specs(a, b, f, decay, cd, rows, row_block)
    in_specs += [_resident(wro.shape), _resident(wco.shape), _resident(wo.shape)]
    args += [wro, wco, wo]
    kernel = functools.partial(_retention_mix_kernel, chunk=chunk, n_chunks=n_chunks)
    return pl.pallas_call(
        kernel,
        grid=(n_batch, steps),
        in_specs=in_specs,
        out_specs=[tile(D, 0), pl.BlockSpec((1, N_HEADS, DK, DV), lambda b_, t: (b_, 0, 0, 0))],
        out_shape=[jax.ShapeDtypeStruct((T, D), F32),
                   jax.ShapeDtypeStruct((n_batch, N_HEADS, DK, DV), F32)],
        scratch_shapes=[pltpu.VMEM((rows, N_HEADS * DV), BF16)],
        compiler_params=_params(("parallel", "arbitrary")),
        name="retention_mix",
    )(*args)


def _mix_kernel(x_ref, og_ref, cbv_ref, sa_ref, sb_ref, wro_ref, wco_ref, wo_ref, x1_ref):
    y_ret = _dot(og_ref[...], wro_ref[...])
    y_conv = _dot(cbv_ref[...], wco_ref[...])
    merged = sa_ref[...] * y_ret + sb_ref[...] * y_conv
    x1_ref[...] = x_ref[...] + _dot(merged.astype(BF16), wo_ref[...])


def _mix(x2d, og, b, f, wro, wco, wo, *, tm):
    T, D = x2d.shape

    def tile(width, col):
        return pl.BlockSpec((tm, width), lambda i: (i, col))

    return pl.pallas_call(
        _mix_kernel,
        grid=(T // tm,),
        in_specs=[tile(D, 0), tile(N_HEADS * DV, 0), tile(D, 2), tile(D, 2), tile(D, 3),
                  _resident(wro.shape), _resident(wco.shape), _resident(wo.shape)],
        out_specs=tile(D, 0),
        out_shape=jax.ShapeDtypeStruct((T, D), F32),
        compiler_params=_params(("parallel",)),
        name="mix",
    )(x2d, og, b, f, f, wro, wco, wo)


N_FFN_INPUTS = 9


def _ffn_kernel(*refs, final_norm, side):
    x_ref, p_ref, gf_ref, wu_ref, wd_ref, gp_ref, wpg_ref, wpp_ref, gl_ref = refs[:N_FFN_INPUTS]
    if side is None:
        o_ref, = refs[N_FFN_INPUTS:]
        x = x_ref[...]
    else:
        ret_refs = refs[N_FFN_INPUTS:N_FFN_INPUTS + N_RET_INPUTS]
        s0_ref, base_ref, o_ref, og_ref, s_ref = refs[N_FFN_INPUTS + N_RET_INPUTS:]
        chunk, n_seq = side
        x = base_ref[...] + x_ref[...]
    hf = _rms(x, gf_ref[...]).astype(BF16)
    n_ff = wu_ref.shape[1] // COL_TILE
    acc = x
    for c in range(n_ff):
        cols = pl.ds(c * COL_TILE, COL_TILE)
        hid = jnp.square(jnp.maximum(_dot(hf, wu_ref[:, cols]), 0.0)).astype(BF16)
        acc = acc + _dot(hid, wd_ref[cols, :])
        if side is not None:
            _retention_body(*ret_refs, s0_ref, s_ref, og_ref, chunk=chunk, n_chunks=1,
                            seqs=range(c * n_seq // n_ff, (c + 1) * n_seq // n_ff))
    gate = jax.nn.sigmoid(_dot(_rms(acc, gp_ref[...]).astype(BF16), wpg_ref[...]))
    y = acc + gate * _dot(p_ref[...].astype(BF16), wpp_ref[...])
    if final_norm:
        y = _rms(y, gl_ref[...])
    o_ref[...] = y


def _ffn(x2d, p2d, gf, wu, wd, gp, wpg, wpp, gl, *, tm, final_norm, side=None):
    T, D = x2d.shape
    steps = T // tm
    in_specs = [pl.BlockSpec((tm, D), lambda i: (i, 0)),
                pl.BlockSpec((tm, p2d.shape[1]), lambda i: (i, 0)),
                _resident(gf.shape), _resident(wu.shape), _resident(wd.shape),
                _resident(gp.shape), _resident(wpg.shape), _resident(wpp.shape),
                _resident(gl.shape)]
    args = [x2d, p2d, gf, wu, wd, gp, wpg, wpp, gl]
    out_specs = [pl.BlockSpec((tm, D), lambda i: (i, 0))]
    out_shape = [jax.ShapeDtypeStruct((T, D), F32)]
    kernel_side = None
    if side is not None:
        base, a, b, f, s0, decay, cd, chunk = side
        n_all = s0.shape[0]
        assert n_all % steps == 0 and a.shape[0] == n_all * chunk
        n_seq = n_all // steps
        ret_specs, ret_args = _retention_specs(a, b, f, decay, cd, n_seq * chunk, lambda i: i)
        state_spec = pl.BlockSpec((n_seq,) + s0.shape[1:], lambda i: (i, 0, 0, 0))
        in_specs += ret_specs + [state_spec, pl.BlockSpec((tm, D), lambda i: (i, 0))]
        args += ret_args + [s0, base]
        out_specs += [pl.BlockSpec((n_seq * chunk, N_HEADS * DV), lambda i: (i, 0)), state_spec]
        out_shape += [jax.ShapeDtypeStruct((a.shape[0], N_HEADS * DV), BF16),
                      jax.ShapeDtypeStruct(s0.shape, F32)]
        kernel_side = (chunk, n_seq)
    kernel = functools.partial(_ffn_kernel, final_norm=final_norm, side=kernel_side)
    out = pl.pallas_call(
        kernel,
        grid=(steps,),
        in_specs=in_specs,
        out_specs=out_specs,
        out_shape=out_shape,
        compiler_params=_params(("parallel",)),
        name="ffn" if side is None else "ffn_retention",
    )(*args)
    return out[0] if side is None else out


def _layer(xp, xs, pp, ps, s_ret, s_conv, pos_p, pos_s, w, *, n_prompt, len_prompt, chunk_prompt,
           n_sample, len_sample, final_norm):
    cos_p, sin_p, decay_p, qd_p, kd_p, cd_p = _tables(pos_p, chunk_prompt)
    cos_s, sin_s, decay_s, qd_s, kd_s, cd_s = _tables(pos_s, len_sample)
    ap, bp, fp, conv_p, w_up, w_down = _inproj(
        xp, w["g_mix"], w["w_in"], w["conv_w"], None, cos_p, sin_p, qd_p, kd_p, n_batch=n_prompt,
        seq_len=len_prompt, chunk=chunk_prompt, tm=1024, cast=(w["w_up"], w["w_down"]))
    as_, bs, fs, conv_s, w_ret_out, w_conv_out, w_o, w_ple_gate = _inproj(
        xs, w["g_mix"], w["w_in"], w["conv_w"], s_conv, cos_s, sin_s, qd_s, kd_s, n_batch=n_sample,
        seq_len=len_sample, chunk=len_sample, tm=512, cast_chunk_bytes=2 * CAST_CHUNK_BYTES,
        cast=(w["w_ret_out"], w["w_conv_out"], w["w_o"], w["w_ple_gate"]))
    ffn_w = (w["g_ffn"], w_up, w_down, w["g_ple"], w_ple_gate, w["w_ple_proj"], w["g_final"])
    mix_w = (w_ret_out, w_conv_out, w_o)
    dxp, ret_p = _retention_mix(ap, bp, fp, decay_p, cd_p, *mix_w, n_batch=n_prompt,
                                seq_len=len_prompt, chunk=chunk_prompt, n_chunks=4)
    yp, og_s, ret_s = _ffn(dxp, pp, *ffn_w, tm=512, final_norm=final_norm,
                           side=(xp, as_, bs, fs, s_ret, decay_s, cd_s, len_sample))
    x1s = _mix(xs, og_s, bs, fs, *mix_w, tm=512)
    ys = _ffn(x1s, ps, *ffn_w, tm=512, final_norm=final_norm)
    return yp, ys, ret_p, conv_p, ret_s, conv_s


def kernel(x_prompt, x_sample, state_ret, state_conv, p_prompt, p_sample, g_mix, w_in, w_ret_out,
           conv_w, w_conv_out, w_o, g_ffn, w_up, w_down, g_ple, w_ple_gate, w_ple_proj, g_final):
    depth = w_in.shape[0]
    B, L, D = x_prompt.shape
    Bs, Ls, _ = x_sample.shape
    pos_prompt = jnp.arange(L, dtype=F32)
    pos_sample = PAST_LEN + jnp.arange(Ls, dtype=F32)
    chunk_prompt = min(RET_CHUNK, L)

    hp = x_prompt.reshape(B * L, D)
    hs = x_sample.reshape(Bs * Ls, D)
    rp, cp, rs, cs = [], [], [], []
    for i in range(depth):
        w = {"g_mix": g_mix[i][None], "w_in": w_in[i].astype(BF16),
             "w_ret_out": w_ret_out[i], "conv_w": conv_w[i],
             "w_conv_out": w_conv_out[i], "w_o": w_o[i],
             "g_ffn": g_ffn[i][None], "w_up": w_up[i], "w_down": w_down[i],
             "g_ple": g_ple[i][None], "w_ple_gate": w_ple_gate[i],
             "w_ple_proj": w_ple_proj[i].astype(BF16), "g_final": g_final[None]}
        hp, hs, r_p, c_p, r_s, c_s = _layer(
            hp, hs, p_prompt[i].reshape(B * L, -1), p_sample[i].reshape(Bs * Ls, -1),
            state_ret[i], state_conv[i], pos_prompt, pos_sample, w,
            n_prompt=B, len_prompt=L, chunk_prompt=chunk_prompt, n_sample=Bs, len_sample=Ls,
            final_norm=i == depth - 1)
        rp.append(r_p)
        cp.append(c_p)
        rs.append(r_s)
        cs.append(c_s)
    return (hp.reshape(B, L, D), hs.reshape(Bs, Ls, D), jnp.stack(rp), jnp.stack(cp),
            jnp.stack(rs), jnp.stack(cs))
```

```python
import functools

import jax
import jax.numpy as jnp
from jax import lax
from jax.experimental import pallas as pl
from jax.experimental.pallas import tpu as pltpu

F32 = jnp.float32
BF16 = jnp.bfloat16

N_HEADS = 8
DK = 128
DV = 256
CONV_W = 3
RET_CHUNK = 128
PAST_LEN = 16384
ROPE_BASE = 10000.0
EPS = 1e-6

V7X_SUBLANES = 8
V7X_VMEM_BYTES = 64 * 1024 * 1024
V7X_VMEM_LIMIT_BYTES = V7X_VMEM_BYTES - 4 * 1024 * 1024

COL_TILE = 1024

IN_GROUP_TILES = ((0, 2), (1, 3), (4, 6), (10,), (5, 7), (8, 9))
IN_GROUPS = len(IN_GROUP_TILES)
IN_TILES = sum(len(tiles) for tiles in IN_GROUP_TILES)
IN_GROUP_TILES_NO_GATES = ((0, 2), (1, 3), (6, 7, 8))
GATE_TILES = (4, 5, 9, 10)
W_SLOTS = 3
CAST_CHUNK_BYTES = 512 * 1024


def _rms(x, g):
    return x * lax.rsqrt(jnp.mean(x * x, axis=-1, keepdims=True) + EPS) * g


def _dot(a, b):
    return jnp.dot(a, b, preferred_element_type=F32)


def _params(semantics):
    return pltpu.CompilerParams(dimension_semantics=semantics,
                                vmem_limit_bytes=V7X_VMEM_LIMIT_BYTES)


def _resident(shape):
    zeros = (0,) * len(shape)
    return pl.BlockSpec(shape, lambda *_: zeros, pipeline_mode=pl.Buffered(1))


def _tables(pos, chunk):
    inv_freq = ROPE_BASE ** (-jnp.arange(0, DK, 2, dtype=F32) / DK)
    ang = pos.astype(F32)[:, None] * inv_freq[None, :]
    c, s = jnp.cos(ang), jnp.sin(ang)
    cos2 = jnp.concatenate([c, c], axis=-1)
    sin2 = jnp.concatenate([-s, s], axis=-1)
    log_g = jnp.log(1.0 - 2.0 ** (-5.0 - jnp.arange(N_HEADS, dtype=F32)))
    idx = jnp.arange(chunk, dtype=F32)
    diff = idx[:, None] - idx[None, :]
    decay = jnp.where(diff >= 0, jnp.exp(jnp.maximum(diff, 0.0)[None] * log_g[:, None, None]), 0.0)
    q_decay = jnp.exp((idx + 1.0)[None, :] * log_g[:, None])
    k_decay = jnp.exp((chunk - 1.0 - idx)[None, :] * log_g[:, None])
    chunk_decay = jnp.exp(chunk * log_g)
    qd = jnp.repeat(q_decay.T, DK, axis=1)
    kd = jnp.repeat(k_decay.T, DK, axis=1)
    return cos2, sin2, decay, qd, kd, chunk_decay


def _inproj_kernel(*refs, tm, chunk, seq_len, has_init, cast_ranges, gates):
    groups = IN_GROUP_TILES if gates else IN_GROUP_TILES_NO_GATES
    n_groups = len(groups)
    refs = list(refs)
    x_ref, g_ref, w_hbm, cos_ref, sin_ref, qd_ref, kd_ref, cw_ref = refs[:8]
    del refs[:8]
    st_ref = refs.pop(0) if has_init else None
    n_cast = len(cast_ranges)
    cast_in = refs[:n_cast]
    a_ref, b_ref, f_ref, cs_ref = refs[n_cast:n_cast + 4]
    cast_out = refs[n_cast + 4:2 * n_cast + 4]
    if gates:
        h_ref, cx_ref, cb_ref, carry_ref, wbuf, wsem = refs[2 * n_cast + 4:]
    else:
        carry_ref, wbuf, wsem = refs[2 * n_cast + 4:]
        h_ref = f_ref
    i = pl.program_id(0)
    j = pl.program_id(1)
    n_pos = cos_ref.shape[0]

    step = i * n_groups + j
    n_steps = pl.num_programs(0) * n_groups

    for src, dst, (first, last) in zip(cast_in, cast_out, cast_ranges):
        @pl.when((step >= first) & (step < last))
        def _():
            dst[...] = src[...].astype(BF16)

    def w_copies(group):
        slot = group % W_SLOTS
        return [pltpu.make_async_copy(w_hbm.at[:, pl.ds(tile * COL_TILE, COL_TILE)],
                                      wbuf.at[slot, :, pl.ds(t * COL_TILE, COL_TILE)],
                                      wsem.at[slot, t])
                for t, tile in enumerate(groups[group])]

    @pl.when(step == 0)
    def _():
        for group in range(W_SLOTS - 1):
            for copy in w_copies(group):
                copy.start()

    for jj in range(n_groups):
        @pl.when((j == jj) & (step + W_SLOTS - 1 < n_steps))
        def _():
            for copy in w_copies((jj + W_SLOTS - 1) % n_groups):
                copy.start()

        @pl.when(j == jj)
        def _():
            for copy in w_copies(jj):
                copy.wait()

    def proj(jj, t, h=None):
        h = h_ref[...] if h is None else h
        return _dot(h, wbuf[jj % W_SLOTS, :, t * COL_TILE:(t + 1) * COL_TILE])

    def rotary_heads(acc, dec_ref, post_scale):
        plain, scaled = [], []
        for h in range(N_HEADS):
            cols = slice(h * DK, (h + 1) * DK)
            t = acc[:, cols]
            t3 = t.reshape(tm // n_pos, n_pos, DK)
            r3 = pltpu.roll(t, DK // 2, 1).reshape(tm // n_pos, n_pos, DK)
            r = (t3 * cos_ref[...][None] + r3 * sin_ref[...][None]).reshape(tm, DK)
            if post_scale != 1.0:
                r = r * post_scale
            plain.append(r.astype(BF16))
            d = r.reshape(tm // chunk, chunk, DK) * dec_ref[:, cols][None]
            scaled.append(d.reshape(tm, DK).astype(BF16))
        return jnp.concatenate(plain + scaled, axis=1)

    def conv(u, cb):
        def patch(pos, prev2, prev1, r1, r2):
            return (jnp.where(pos == 0, prev1, r1),
                    jnp.where(pos == 0, prev2, jnp.where(pos == 1, prev1, r2)))

        if has_init:
            n = tm // seq_len
            prev2 = jnp.broadcast_to(st_ref[:, 0:1, :], (n, seq_len, u.shape[1])).reshape(u.shape)
            prev1 = jnp.broadcast_to(st_ref[:, 1:2, :], (n, seq_len, u.shape[1])).reshape(u.shape)
            u1, u2 = patch(lax.broadcasted_iota(jnp.int32, u.shape, 0) % seq_len, prev2, prev1,
                           pltpu.roll(u, 1, 0), pltpu.roll(u, 2, 0))
        else:
            s = V7X_SUBLANES
            first = (i % (seq_len // tm)) == 0
            halo = jnp.where(first, 0.0, carry_ref[...])
            prev2 = jnp.broadcast_to(halo[s - 2:s - 1, :], halo.shape)
            prev1 = jnp.broadcast_to(halo[s - 1:s, :], halo.shape)
            r1, r2 = pltpu.roll(u, 1, 0), pltpu.roll(u, 2, 0)
            h1, h2 = patch(lax.broadcasted_iota(jnp.int32, halo.shape, 0), prev2, prev1, r1[:s], r2[:s])
            u1 = jnp.concatenate([h1, r1[s:]], axis=0)
            u2 = jnp.concatenate([h2, r2[s:]], axis=0)
        cv = cw_ref[0:1, :] * u2 + cw_ref[1:2, :] * u1 + cw_ref[2:3, :] * u
        b_ref[...] = (cb * cv).astype(BF16)
        if has_init:
            cs_ref[...] = u.reshape(tm // seq_len, seq_len, u.shape[1])[:, seq_len - (CONV_W - 1):, :]
        else:
            carry_ref[...] = u[tm - V7X_SUBLANES:, :]
            cs_ref[0] = u[tm - (CONV_W - 1):, :]

    @pl.when(j == 0)
    def _():
        h = _rms(x_ref[...], g_ref[...]).astype(BF16)
        h_ref[...] = h
        a_ref[...] = rotary_heads(proj(0, 0, h), qd_ref, DK ** -0.5)
        b_ref[...] = proj(0, 1, h).astype(BF16)

    @pl.when(j == 1)
    def _():
        a_ref[...] = rotary_heads(proj(1, 0), kd_ref, 1.0)
        b_ref[...] = proj(1, 1).astype(BF16)

    if gates:
        @pl.when(j == 2)
        def _():
            f_ref[...] = jax.nn.silu(proj(2, 0))
            cx_ref[...] = proj(2, 1)

        @pl.when(j == 3)
        def _():
            f_ref[...] = jax.nn.sigmoid(proj(3, 0))

        @pl.when(j == 4)
        def _():
            f_ref[...] = jax.nn.silu(proj(4, 0))
            cb_ref[...] = proj(4, 1)

        @pl.when(j == 5)
        def _():
            f_ref[...] = jax.nn.sigmoid(proj(5, 1))
            conv(proj(5, 0) * cx_ref[...], cb_ref[...])
    else:
        @pl.when(j == 2)
        def _():
            cx, cb, cc = proj(2, 0), proj(2, 1), proj(2, 2)
            conv(cc * cx, cb)


def _inproj(x2d, g, w, cw, conv_state, cos2, sin2, qd, kd, *, n_batch, seq_len, chunk, tm, cast=(),
            cast_chunk_bytes=CAST_CHUNK_BYTES, gates=True):
    T, D = x2d.shape
    has_init = conv_state is not None
    groups = IN_GROUP_TILES if gates else IN_GROUP_TILES_NO_GATES
    n_groups = len(groups)
    group_width = max(len(tiles) for tiles in groups)
    assert w.shape[1] == IN_TILES * COL_TILE and D == COL_TILE and n_groups % W_SLOTS == 0
    if has_init:
        assert tm % seq_len == 0
        n = tm // seq_len
        pos_spec = pl.BlockSpec((seq_len, DK), lambda i, j: (0, 0))
        st_specs = [pl.BlockSpec((n, CONV_W - 1, D), lambda i, j: (i, 0, 0))]
        st_args = [conv_state]
        cs_spec = pl.BlockSpec((n, CONV_W - 1, D), lambda i, j: (i, 0, 0))
    else:
        assert seq_len % tm == 0 and tm % chunk == 0
        tiles = seq_len // tm
        pos_spec = pl.BlockSpec((tm, DK), lambda i, j: (i % tiles, 0))
        st_specs, st_args = [], []
        cs_spec = pl.BlockSpec((1, CONV_W - 1, D), lambda i, j: (i // tiles, 0, 0))

    if gates:
        def b_col(j):
            return jnp.where(j == 0, 0, jnp.where(j < 5, 1, 2))

        def f_col(j):
            return jnp.where(j <= 2, 0, jnp.where(j == 3, 3, j - 3))

        third_shape = jax.ShapeDtypeStruct((T, 4 * COL_TILE), F32)
        conv_scratch = [pltpu.VMEM((tm, D), BF16), pltpu.VMEM((tm, D), F32), pltpu.VMEM((tm, D), F32)]
    else:
        def b_col(j):
            return j

        def f_col(j):
            return 0

        third_shape = jax.ShapeDtypeStruct((T, D), BF16)
        conv_scratch = []

    n_steps = (T // tm) * n_groups
    cast_specs_in, cast_specs_out, cast_shapes, cast_ranges = [], [], [], []
    first = 0
    for m in cast:
        rows = cast_chunk_bytes // (m.shape[1] * m.dtype.itemsize)
        n_chunks = m.shape[0] // rows
        assert m.shape[0] % rows == 0 and n_chunks <= n_steps
        if first + n_chunks > n_steps:
            first = 0

        def chunk_map(i, j, first=first, n_chunks=n_chunks):
            return jnp.clip(i * n_groups + j - first, 0, n_chunks - 1), 0

        cast_specs_in.append(pl.BlockSpec((rows, m.shape[1]), chunk_map))
        cast_specs_out.append(pl.BlockSpec((rows, m.shape[1]), chunk_map))
        cast_shapes.append(jax.ShapeDtypeStruct(m.shape, BF16))
        cast_ranges.append((first, first + n_chunks))
        first += n_chunks

    kernel = functools.partial(_inproj_kernel, tm=tm, chunk=chunk, seq_len=seq_len, has_init=has_init,
                               cast_ranges=tuple(cast_ranges), gates=gates)
    return pl.pallas_call(
        kernel,
        grid=(T // tm, n_groups),
        in_specs=[pl.BlockSpec((tm, D), lambda i, j: (i, 0)),
                  pl.BlockSpec((1, D), lambda i, j: (0, 0)),
                  pl.BlockSpec(memory_space=pl.ANY),
                  pos_spec, pos_spec,
                  _resident(qd.shape), _resident(kd.shape), _resident(cw.shape)] + st_specs + cast_specs_in,
        out_specs=[pl.BlockSpec((tm, 2 * COL_TILE), lambda i, j: (i, jnp.minimum(j, 1))),
                   pl.BlockSpec((tm, COL_TILE), lambda i, j: (i, b_col(j))),
                   pl.BlockSpec((tm, COL_TILE), lambda i, j: (i, f_col(j))),
                   cs_spec] + cast_specs_out,
        out_shape=[jax.ShapeDtypeStruct((T, 4 * COL_TILE), BF16),
                   jax.ShapeDtypeStruct((T, 3 * COL_TILE), BF16),
                   third_shape,
                   jax.ShapeDtypeStruct((n_batch, CONV_W - 1, D), F32)] + cast_shapes,
        scratch_shapes=conv_scratch + [pltpu.VMEM((V7X_SUBLANES, D), F32),
                                       pltpu.VMEM((W_SLOTS, D, group_width * COL_TILE), BF16),
                                       pltpu.SemaphoreType.DMA((W_SLOTS, group_width))],
        compiler_params=_params(("arbitrary", "arbitrary")),
        name="inproj_init" if has_init else "inproj",
    )(x2d, g, w, cos2, sin2, qd, kd, cw, *st_args, *cast)


N_RET_INPUTS = 5


def _retention_body(a_ref, b_ref, f_ref, dec_ref, cd_ref, s0_ref, s_ref, o_ref, *, chunk, n_chunks, seqs):
    has_init = s0_ref is not None
    fuse = chunk % DK == 0
    q_ref, qd_ref, k_ref, kd_ref = (a_ref.at[:, pl.ds(t * COL_TILE, COL_TILE)] for t in range(4))
    sg_ref = f_ref

    def v_at(rows, h):
        return b_ref[rows, pl.ds(h * DV, DV)]

    for seq in seqs:
        row_slices = [pl.ds((seq * n_chunks + c) * chunk, chunk) for c in range(n_chunks)]
        probs, kvs = {}, {}
        for c, rows in enumerate(row_slices):
            for h in range(N_HEADS):
                kcols = pl.ds(h * DK, DK)
                scores = lax.dot_general(q_ref[rows, kcols], k_ref[rows, kcols], (((1,), (1,)), ((), ())),
                                         preferred_element_type=F32) * dec_ref[h]
                probs[c, h] = scores.astype(BF16)
                kdt = jnp.transpose(kd_ref[rows, kcols].astype(F32)).astype(BF16)
                kvs[c, h] = _dot(kdt, v_at(rows, h))
        if has_init:
            states = [s0_ref[seq, h] for h in range(N_HEADS)]
        else:
            states = [s_ref[seq, h] for h in range(N_HEADS)]
        for c, rows in enumerate(row_slices):
            for h in range(N_HEADS):
                kcols = pl.ds(h * DK, DK)
                ocols = pl.ds(h * DV, DV)
                S = states[h]
                if fuse:
                    o = _dot(jnp.concatenate([probs[c, h], qd_ref[rows, kcols]], axis=1),
                             jnp.concatenate([v_at(rows, h), S.astype(BF16)], axis=0))
                else:
                    o = _dot(probs[c, h], v_at(rows, h)) + _dot(qd_ref[rows, kcols], S.astype(BF16))
                states[h] = S * cd_ref[h] + kvs[c, h]
                o = o * lax.rsqrt(jnp.mean(o * o, axis=-1, keepdims=True) + EPS)
                o_ref[rows, ocols] = (o * sg_ref[rows, ocols]).astype(BF16)
        for h in range(N_HEADS):
            s_ref[seq, h] = states[h]


def _retention_specs(a, b, f, decay, cd, rows, row_block):
    def whole(arr):
        return pl.BlockSpec((rows, arr.shape[1]), lambda *g: (row_block(*g), 0))

    specs = [whole(a), whole(b), whole(f), _resident(decay.shape), pl.BlockSpec(memory_space=pltpu.SMEM)]
    return specs, [a, b, f, decay, cd]


def _retention_mix_kernel(a_ref, b_ref, h_ref, dec_ref, cd_ref, wg0_ref, wg1_ref, wga_ref, wgb_ref,
                          wro_ref, wco_ref, wo_ref, d_ref, s_ref, o_ref, sg_ref, *, chunk, n_chunks):
    @pl.when(pl.program_id(1) == 0)
    def _():
        s_ref[...] = jnp.zeros_like(s_ref)

    h = h_ref[...]
    sg_ref[:, :COL_TILE] = jax.nn.silu(_dot(h, wg0_ref[...]))
    sg_ref[:, COL_TILE:] = jax.nn.silu(_dot(h, wg1_ref[...]))
    _retention_body(a_ref, b_ref, sg_ref, dec_ref, cd_ref, None, s_ref, o_ref,
                    chunk=chunk, n_chunks=n_chunks, seqs=range(1))
    cbv = b_ref[:, 2 * COL_TILE:]
    sig_a = jax.nn.sigmoid(_dot(h, wga_ref[...]))
    sig_b = jax.nn.sigmoid(_dot(h, wgb_ref[...]))
    y_conv = _dot(cbv, wco_ref[...])
    merged = sig_a * _dot(o_ref[...], wro_ref[...]) + sig_b * y_conv
    d_ref[...] = _dot(merged.astype(BF16), wo_ref[...])


def _retention_mix(a, b, h, w_in, decay, cd, wro, wco, wo, *, n_batch, seq_len, chunk, n_chunks):
    T, D = a.shape[0], wo.shape[1]
    rows = n_chunks * chunk
    steps = seq_len // rows

    def row_block(b_, t):
        return b_ * steps + t

    def tile(width, col):
        return pl.BlockSpec((rows, width), lambda b_, t: (row_block(b_, t), col))

    in_specs, args = _retention_specs(a, b, h, decay, cd, rows, row_block)
    in_specs += [pl.BlockSpec((D, COL_TILE), lambda b_, t, tile_=tile_: (0, tile_), pipeline_mode=pl.Buffered(1))
                 for tile_ in GATE_TILES]
    in_specs += [_resident(wro.shape), _resident(wco.shape), _resident(wo.shape)]
    args += [w_in] * len(GATE_TILES) + [wro, wco, wo]
    kernel = functools.partial(_retention_mix_kernel, chunk=chunk, n_chunks=n_chunks)
    return pl.pallas_call(
        kernel,
        grid=(n_batch, steps),
        in_specs=in_specs,
        out_specs=[tile(D, 0), pl.BlockSpec((1, N_HEADS, DK, DV), lambda b_, t: (b_, 0, 0, 0))],
        out_shape=[jax.ShapeDtypeStruct((T, D), F32),
                   jax.ShapeDtypeStruct((n_batch, N_HEADS, DK, DV), F32)],
        scratch_shapes=[pltpu.VMEM((rows, N_HEADS * DV), BF16), pltpu.VMEM((rows, N_HEADS * DV), F32)],
        compiler_params=_params(("parallel", "arbitrary")),
        name="retention_mix",
    )(*args)


def _mix_kernel(x_ref, og_ref, cbv_ref, sa_ref, sb_ref, wro_ref, wco_ref, wo_ref, x1_ref):
    y_ret = _dot(og_ref[...], wro_ref[...])
    y_conv = _dot(cbv_ref[...], wco_ref[...])
    merged = sa_ref[...] * y_ret + sb_ref[...] * y_conv
    x1_ref[...] = x_ref[...] + _dot(merged.astype(BF16), wo_ref[...])


def _mix(x2d, og, b, f, wro, wco, wo, *, tm):
    T, D = x2d.shape

    def tile(width, col):
        return pl.BlockSpec((tm, width), lambda i: (i, col))

    return pl.pallas_call(
        _mix_kernel,
        grid=(T // tm,),
        in_specs=[tile(D, 0), tile(N_HEADS * DV, 0), tile(D, 2), tile(D, 2), tile(D, 3),
                  _resident(wro.shape), _resident(wco.shape), _resident(wo.shape)],
        out_specs=tile(D, 0),
        out_shape=jax.ShapeDtypeStruct((T, D), F32),
        compiler_params=_params(("parallel",)),
        name="mix",
    )(x2d, og, b, f, f, wro, wco, wo)


N_FFN_INPUTS = 9


def _ffn_kernel(*refs, final_norm, side):
    x_ref, p_ref, gf_ref, wu_ref, wd_ref, gp_ref, wpg_ref, wpp_ref, gl_ref = refs[:N_FFN_INPUTS]
    if side is None:
        o_ref, = refs[N_FFN_INPUTS:]
        x = x_ref[...]
    else:
        ret_refs = refs[N_FFN_INPUTS:N_FFN_INPUTS + N_RET_INPUTS]
        s0_ref, base_ref, o_ref, og_ref, s_ref = refs[N_FFN_INPUTS + N_RET_INPUTS:]
        chunk, n_seq = side
        x = base_ref[...] + x_ref[...]
    hf = _rms(x, gf_ref[...]).astype(BF16)
    n_ff = wu_ref.shape[1] // COL_TILE
    acc = x
    for c in range(n_ff):
        cols = pl.ds(c * COL_TILE, COL_TILE)
        hid = jnp.square(jnp.maximum(_dot(hf, wu_ref[:, cols]), 0.0)).astype(BF16)
        acc = acc + _dot(hid, wd_ref[cols, :])
        if side is not None:
            _retention_body(*ret_refs, s0_ref, s_ref, og_ref, chunk=chunk, n_chunks=1,
                            seqs=range(c * n_seq // n_ff, (c + 1) * n_seq // n_ff))
    gate = jax.nn.sigmoid(_dot(_rms(acc, gp_ref[...]).astype(BF16), wpg_ref[...]))
    y = acc + gate * _dot(p_ref[...].astype(BF16), wpp_ref[...])
    if final_norm:
        y = _rms(y, gl_ref[...])
    o_ref[...] = y


def _ffn(x2d, p2d, gf, wu, wd, gp, wpg, wpp, gl, *, tm, final_norm, side=None):
    T, D = x2d.shape
    steps = T // tm
    in_specs = [pl.BlockSpec((tm, D), lambda i: (i, 0)),
                pl.BlockSpec((tm, p2d.shape[1]), lambda i: (i, 0)),
                _resident(gf.shape), _resident(wu.shape), _resident(wd.shape),
                _resident(gp.shape), _resident(wpg.shape), _resident(wpp.shape),
                _resident(gl.shape)]
    args = [x2d, p2d, gf, wu, wd, gp, wpg, wpp, gl]
    out_specs = [pl.BlockSpec((tm, D), lambda i: (i, 0))]
    out_shape = [jax.ShapeDtypeStruct((T, D), F32)]
    kernel_side = None
    if side is not None:
        base, a, b, f, s0, decay, cd, chunk = side
        n_all = s0.shape[0]
        assert n_all % steps == 0 and a.shape[0] == n_all * chunk
        n_seq = n_all // steps
        ret_specs, ret_args = _retention_specs(a, b, f, decay, cd, n_seq * chunk, lambda i: i)
        state_spec = pl.BlockSpec((n_seq,) + s0.shape[1:], lambda i: (i, 0, 0, 0))
        in_specs += ret_specs + [state_spec, pl.BlockSpec((tm, D), lambda i: (i, 0))]
        args += ret_args + [s0, base]
        out_specs += [pl.BlockSpec((n_seq * chunk, N_HEADS * DV), lambda i: (i, 0)), state_spec]
        out_shape += [jax.ShapeDtypeStruct((a.shape[0], N_HEADS * DV), BF16),
                      jax.ShapeDtypeStruct(s0.shape, F32)]
        kernel_side = (chunk, n_seq)
    kernel = functools.partial(_ffn_kernel, final_norm=final_norm, side=kernel_side)
    out = pl.pallas_call(
        kernel,
        grid=(steps,),
        in_specs=in_specs,
        out_specs=out_specs,
        out_shape=out_shape,
        compiler_params=_params(("parallel",)),
        name="ffn" if side is None else "ffn_retention",
    )(*args)
    return out[0] if side is None else out


def _layer(xp, xs, pp, ps, s_ret, s_conv, pos_p, pos_s, w, *, n_prompt, len_prompt, chunk_prompt,
           n_sample, len_sample, final_norm):
    cos_p, sin_p, decay_p, qd_p, kd_p, cd_p = _tables(pos_p, chunk_prompt)
    cos_s, sin_s, decay_s, qd_s, kd_s, cd_s = _tables(pos_s, len_sample)
    ap, bp, hp, conv_p, w_up, w_down = _inproj(
        xp, w["g_mix"], w["w_in"], w["conv_w"], None, cos_p, sin_p, qd_p, kd_p, n_batch=n_prompt,
        seq_len=len_prompt, chunk=chunk_prompt, tm=1024, cast=(w["w_up"], w["w_down"]), gates=False)
    as_, bs, fs, conv_s, w_ret_out, w_conv_out, w_o, w_ple_gate = _inproj(
        xs, w["g_mix"], w["w_in"], w["conv_w"], s_conv, cos_s, sin_s, qd_s, kd_s, n_batch=n_sample,
        seq_len=len_sample, chunk=len_sample, tm=512, cast_chunk_bytes=2 * CAST_CHUNK_BYTES,
        cast=(w["w_ret_out"], w["w_conv_out"], w["w_o"], w["w_ple_gate"]))
    ffn_w = (w["g_ffn"], w_up, w_down, w["g_ple"], w_ple_gate, w["w_ple_proj"], w["g_final"])
    mix_w = (w_ret_out, w_conv_out, w_o)
    dxp, ret_p = _retention_mix(ap, bp, hp, w["w_in"], decay_p, cd_p, *mix_w, n_batch=n_prompt,
                                seq_len=len_prompt, chunk=chunk_prompt, n_chunks=4)
    yp, og_s, ret_s = _ffn(dxp, pp, *ffn_w, tm=512, final_norm=final_norm,
                           side=(xp, as_, bs, fs, s_ret, decay_s, cd_s, len_sample))
    x1s = _mix(xs, og_s, bs, fs, *mix_w, tm=512)
    ys = _ffn(x1s, ps, *ffn_w, tm=512, final_norm=final_norm)
    return yp, ys, ret_p, conv_p, ret_s, conv_s


def kernel(x_prompt, x_sample, state_ret, state_conv, p_prompt, p_sample, g_mix, w_in, w_ret_out,
           conv_w, w_conv_out, w_o, g_ffn, w_up, w_down, g_ple, w_ple_gate, w_ple_proj, g_final):
    depth = w_in.shape[0]
    B, L, D = x_prompt.shape
    Bs, Ls, _ = x_sample.shape
    pos_prompt = jnp.arange(L, dtype=F32)
    pos_sample = PAST_LEN + jnp.arange(Ls, dtype=F32)
    chunk_prompt = min(RET_CHUNK, L)

    hp = x_prompt.reshape(B * L, D)
    hs = x_sample.reshape(Bs * Ls, D)
    rp, cp, rs, cs = [], [], [], []
    for i in range(depth):
        w = {"g_mix": g_mix[i][None], "w_in": w_in[i].astype(BF16),
             "w_ret_out": w_ret_out[i], "conv_w": conv_w[i],
             "w_conv_out": w_conv_out[i], "w_o": w_o[i],
             "g_ffn": g_ffn[i][None], "w_up": w_up[i], "w_down": w_down[i],
             "g_ple": g_ple[i][None], "w_ple_gate": w_ple_gate[i],
             "w_ple_proj": w_ple_proj[i].astype(BF16), "g_final": g_final[None]}
        hp, hs, r_p, c_p, r_s, c_s = _layer(
            hp, hs, p_prompt[i].reshape(B * L, -1), p_sample[i].reshape(Bs * Ls, -1),
            state_ret[i], state_conv[i], pos_prompt, pos_sample, w,
            n_prompt=B, len_prompt=L, chunk_prompt=chunk_prompt, n_sample=Bs, len_sample=Ls,
            final_norm=i == depth - 1)
        rp.append(r_p)
        cp.append(c_p)
        rs.append(r_s)
        cs.append(c_s)
    return (hp.reshape(B, L, D), hs.reshape(Bs, Ls, D), jnp.stack(rp), jnp.stack(cp),
            jnp.stack(rs), jnp.stack(cs))
```

```python
import functools

import jax
import jax.numpy as jnp
from jax import lax
from jax.experimental import pallas as pl
from jax.experimental.pallas import tpu as pltpu

F32 = jnp.float32
BF16 = jnp.bfloat16

N_HEADS = 8
DK = 128
DV = 256
CONV_W = 3
RET_CHUNK = 128
PAST_LEN = 16384
ROPE_BASE = 10000.0
EPS = 1e-6

V7X_SUBLANES = 8
V7X_VMEM_BYTES = 64 * 1024 * 1024
V7X_VMEM_LIMIT_BYTES = V7X_VMEM_BYTES - 4 * 1024 * 1024

COL_TILE = 1024

IN_GROUP_TILES = ((0, 2), (1, 3), (4, 6), (10,), (5, 7), (8, 9))
IN_GROUPS = len(IN_GROUP_TILES)
IN_TILES = sum(len(tiles) for tiles in IN_GROUP_TILES)
IN_GROUP_TILES_NO_GATES = ((0, 2), (1, 3), (6, 7, 8))
GATE_TILES = (4, 5, 9, 10)
W_SLOTS = 3
CAST_CHUNK_BYTES = 512 * 1024


def _rms(x, g):
    return x * lax.rsqrt(jnp.mean(x * x, axis=-1, keepdims=True) + EPS) * g


def _dot(a, b):
    return jnp.dot(a, b, preferred_element_type=F32)


def _params(semantics):
    return pltpu.CompilerParams(dimension_semantics=semantics,
                                vmem_limit_bytes=V7X_VMEM_LIMIT_BYTES)


def _resident(shape):
    zeros = (0,) * len(shape)
    return pl.BlockSpec(shape, lambda *_: zeros, pipeline_mode=pl.Buffered(1))


def _tables(pos, chunk):
    inv_freq = ROPE_BASE ** (-jnp.arange(0, DK, 2, dtype=F32) / DK)
    ang = pos.astype(F32)[:, None] * inv_freq[None, :]
    c, s = jnp.cos(ang), jnp.sin(ang)
    cos2 = jnp.concatenate([c, c], axis=-1)
    sin2 = jnp.concatenate([-s, s], axis=-1)
    log_g = jnp.log(1.0 - 2.0 ** (-5.0 - jnp.arange(N_HEADS, dtype=F32)))
    idx = jnp.arange(chunk, dtype=F32)
    diff = idx[:, None] - idx[None, :]
    decay = jnp.where(diff >= 0, jnp.exp(jnp.maximum(diff, 0.0)[None] * log_g[:, None, None]), 0.0)
    q_decay = jnp.exp((idx + 1.0)[None, :] * log_g[:, None])
    k_decay = jnp.exp((chunk - 1.0 - idx)[None, :] * log_g[:, None])
    chunk_decay = jnp.exp(chunk * log_g)
    qd = jnp.repeat(q_decay.T, DK, axis=1)
    kd = jnp.repeat(k_decay.T, DK, axis=1)
    return cos2, sin2, decay, qd, kd, chunk_decay


def _inproj_kernel(*refs, tm, chunk, seq_len, has_init, cast_ranges, gates):
    groups = IN_GROUP_TILES if gates else IN_GROUP_TILES_NO_GATES
    n_groups = len(groups)
    refs = list(refs)
    x_ref, g_ref, w_hbm, cos_ref, sin_ref, qd_ref, kd_ref, cw_ref = refs[:8]
    del refs[:8]
    st_ref = refs.pop(0) if has_init else None
    n_cast = len(cast_ranges)
    cast_in = refs[:n_cast]
    a_ref, b_ref, f_ref, cs_ref = refs[n_cast:n_cast + 4]
    cast_out = refs[n_cast + 4:2 * n_cast + 4]
    if gates:
        h_ref, cx_ref, cb_ref, carry_ref, wbuf, wsem = refs[2 * n_cast + 4:]
    else:
        carry_ref, wbuf, wsem = refs[2 * n_cast + 4:]
        h_ref = f_ref
    i = pl.program_id(0)
    j = pl.program_id(1)
    n_pos = cos_ref.shape[0]

    step = i * n_groups + j
    n_steps = pl.num_programs(0) * n_groups

    for src, dst, (first, last) in zip(cast_in, cast_out, cast_ranges):
        @pl.when((step >= first) & (step < last))
        def _():
            dst[...] = src[...].astype(BF16)

    def w_copies(group):
        slot = group % W_SLOTS
        return [pltpu.make_async_copy(w_hbm.at[:, pl.ds(tile * COL_TILE, COL_TILE)],
                                      wbuf.at[slot, :, pl.ds(t * COL_TILE, COL_TILE)],
                                      wsem.at[slot, t])
                for t, tile in enumerate(groups[group])]

    @pl.when(step == 0)
    def _():
        for group in range(W_SLOTS - 1):
            for copy in w_copies(group):
                copy.start()

    for jj in range(n_groups):
        @pl.when((j == jj) & (step + W_SLOTS - 1 < n_steps))
        def _():
            for copy in w_copies((jj + W_SLOTS - 1) % n_groups):
                copy.start()

        @pl.when(j == jj)
        def _():
            for copy in w_copies(jj):
                copy.wait()

    def proj(jj, t, h=None):
        h = h_ref[...] if h is None else h
        return _dot(h, wbuf[jj % W_SLOTS, :, t * COL_TILE:(t + 1) * COL_TILE])

    def rotary_heads(acc, dec_ref, post_scale):
        plain, scaled = [], []
        for h in range(N_HEADS):
            cols = slice(h * DK, (h + 1) * DK)
            t = acc[:, cols]
            t3 = t.reshape(tm // n_pos, n_pos, DK)
            r3 = pltpu.roll(t, DK // 2, 1).reshape(tm // n_pos, n_pos, DK)
            r = (t3 * cos_ref[...][None] + r3 * sin_ref[...][None]).reshape(tm, DK)
            if post_scale != 1.0:
                r = r * post_scale
            plain.append(r.astype(BF16))
            d = r.reshape(tm // chunk, chunk, DK) * dec_ref[:, cols][None]
            scaled.append(d.reshape(tm, DK).astype(BF16))
        return jnp.concatenate(plain + scaled, axis=1)

    def conv(u, cb):
        def patch(pos, prev2, prev1, r1, r2):
            return (jnp.where(pos == 0, prev1, r1),
                    jnp.where(pos == 0, prev2, jnp.where(pos == 1, prev1, r2)))

        if has_init:
            n = tm // seq_len
            prev2 = jnp.broadcast_to(st_ref[:, 0:1, :], (n, seq_len, u.shape[1])).reshape(u.shape)
            prev1 = jnp.broadcast_to(st_ref[:, 1:2, :], (n, seq_len, u.shape[1])).reshape(u.shape)
            u1, u2 = patch(lax.broadcasted_iota(jnp.int32, u.shape, 0) % seq_len, prev2, prev1,
                           pltpu.roll(u, 1, 0), pltpu.roll(u, 2, 0))
        else:
            s = V7X_SUBLANES
            first = (i % (seq_len // tm)) == 0
            halo = jnp.where(first, 0.0, carry_ref[...])
            prev2 = jnp.broadcast_to(halo[s - 2:s - 1, :], halo.shape)
            prev1 = jnp.broadcast_to(halo[s - 1:s, :], halo.shape)
            r1, r2 = pltpu.roll(u, 1, 0), pltpu.roll(u, 2, 0)
            h1, h2 = patch(lax.broadcasted_iota(jnp.int32, halo.shape, 0), prev2, prev1, r1[:s], r2[:s])
            u1 = jnp.concatenate([h1, r1[s:]], axis=0)
            u2 = jnp.concatenate([h2, r2[s:]], axis=0)
        cv = cw_ref[0:1, :] * u2 + cw_ref[1:2, :] * u1 + cw_ref[2:3, :] * u
        b_ref[...] = (cb * cv).astype(BF16)
        if has_init:
            cs_ref[...] = u.reshape(tm // seq_len, seq_len, u.shape[1])[:, seq_len - (CONV_W - 1):, :]
        else:
            carry_ref[...] = u[tm - V7X_SUBLANES:, :]
            cs_ref[0] = u[tm - (CONV_W - 1):, :]

    @pl.when(j == 0)
    def _():
        h = _rms(x_ref[...], g_ref[...]).astype(BF16)
        h_ref[...] = h
        a_ref[...] = rotary_heads(proj(0, 0, h), qd_ref, DK ** -0.5)
        b_ref[...] = proj(0, 1, h).astype(BF16)

    @pl.when(j == 1)
    def _():
        a_ref[...] = rotary_heads(proj(1, 0), kd_ref, 1.0)
        b_ref[...] = proj(1, 1).astype(BF16)

    if gates:
        @pl.when(j == 2)
        def _():
            f_ref[...] = jax.nn.silu(proj(2, 0))
            cx_ref[...] = proj(2, 1)

        @pl.when(j == 3)
        def _():
            f_ref[...] = jax.nn.sigmoid(proj(3, 0))

        @pl.when(j == 4)
        def _():
            f_ref[...] = jax.nn.silu(proj(4, 0))
            cb_ref[...] = proj(4, 1)

        @pl.when(j == 5)
        def _():
            f_ref[...] = jax.nn.sigmoid(proj(5, 1))
            conv(proj(5, 0) * cx_ref[...], cb_ref[...])
    else:
        @pl.when(j == 2)
        def _():
            cx, cb, cc = proj(2, 0), proj(2, 1), proj(2, 2)
            conv(cc * cx, cb)


def _inproj(x2d, g, w, cw, conv_state, cos2, sin2, qd, kd, *, n_batch, seq_len, chunk, tm, cast=(),
            cast_chunk_bytes=CAST_CHUNK_BYTES, gates=True):
    T, D = x2d.shape
    has_init = conv_state is not None
    groups = IN_GROUP_TILES if gates else IN_GROUP_TILES_NO_GATES
    n_groups = len(groups)
    group_width = max(len(tiles) for tiles in groups)
    assert w.shape[1] == IN_TILES * COL_TILE and D == COL_TILE and n_groups % W_SLOTS == 0
    if has_init:
        assert tm % seq_len == 0
        n = tm // seq_len
        pos_spec = pl.BlockSpec((seq_len, DK), lambda i, j: (0, 0))
        st_specs = [pl.BlockSpec((n, CONV_W - 1, D), lambda i, j: (i, 0, 0))]
        st_args = [conv_state]
        cs_spec = pl.BlockSpec((n, CONV_W - 1, D), lambda i, j: (i, 0, 0))
    else:
        assert seq_len % tm == 0 and tm % chunk == 0
        tiles = seq_len // tm
        pos_spec = pl.BlockSpec((tm, DK), lambda i, j: (i % tiles, 0))
        st_specs, st_args = [], []
        cs_spec = pl.BlockSpec((1, CONV_W - 1, D), lambda i, j: (i // tiles, 0, 0))

    if gates:
        def b_col(j):
            return jnp.where(j == 0, 0, jnp.where(j < 5, 1, 2))

        def f_col(j):
            return jnp.where(j <= 2, 0, jnp.where(j == 3, 3, j - 3))

        third_shape = jax.ShapeDtypeStruct((T, 4 * COL_TILE), F32)
        conv_scratch = [pltpu.VMEM((tm, D), BF16), pltpu.VMEM((tm, D), F32), pltpu.VMEM((tm, D), F32)]
    else:
        def b_col(j):
            return j

        def f_col(j):
            return 0

        third_shape = jax.ShapeDtypeStruct((T, D), BF16)
        conv_scratch = []

    n_steps = (T // tm) * n_groups
    cast_specs_in, cast_specs_out, cast_shapes, cast_ranges = [], [], [], []
    first = 0
    for m in cast:
        rows = cast_chunk_bytes // (m.shape[1] * m.dtype.itemsize)
        n_chunks = m.shape[0] // rows
        assert m.shape[0] % rows == 0 and n_chunks <= n_steps
        if first + n_chunks > n_steps:
            first = 0

        def chunk_map(i, j, first=first, n_chunks=n_chunks):
            return jnp.clip(i * n_groups + j - first, 0, n_chunks - 1), 0

        cast_specs_in.append(pl.BlockSpec((rows, m.shape[1]), chunk_map))
        cast_specs_out.append(pl.BlockSpec((rows, m.shape[1]), chunk_map))
        cast_shapes.append(jax.ShapeDtypeStruct(m.shape, BF16))
        cast_ranges.append((first, first + n_chunks))
        first += n_chunks

    kernel = functools.partial(_inproj_kernel, tm=tm, chunk=chunk, seq_len=seq_len, has_init=has_init,
                               cast_ranges=tuple(cast_ranges), gates=gates)
    return pl.pallas_call(
        kernel,
        grid=(T // tm, n_groups),
        in_specs=[pl.BlockSpec((tm, D), lambda i, j: (i, 0)),
                  pl.BlockSpec((1, D), lambda i, j: (0, 0)),
                  pl.BlockSpec(memory_space=pl.ANY),
                  pos_spec, pos_spec,
                  _resident(qd.shape), _resident(kd.shape), _resident(cw.shape)] + st_specs + cast_specs_in,
        out_specs=[pl.BlockSpec((tm, 2 * COL_TILE), lambda i, j: (i, jnp.minimum(j, 1))),
                   pl.BlockSpec((tm, COL_TILE), lambda i, j: (i, b_col(j))),
                   pl.BlockSpec((tm, COL_TILE), lambda i, j: (i, f_col(j))),
                   cs_spec] + cast_specs_out,
        out_shape=[jax.ShapeDtypeStruct((T, 4 * COL_TILE), BF16),
                   jax.ShapeDtypeStruct((T, 3 * COL_TILE), BF16),
                   third_shape,
                   jax.ShapeDtypeStruct((n_batch, CONV_W - 1, D), F32)] + cast_shapes,
        scratch_shapes=conv_scratch + [pltpu.VMEM((V7X_SUBLANES, D), F32),
                                       pltpu.VMEM((W_SLOTS, D, group_width * COL_TILE), BF16),
                                       pltpu.SemaphoreType.DMA((W_SLOTS, group_width))],
        compiler_params=_params(("arbitrary", "arbitrary")),
        name="inproj_init" if has_init else "inproj",
    )(x2d, g, w, cos2, sin2, qd, kd, cw, *st_args, *cast)


N_RET_INPUTS = 5


def _retention_body(a_ref, b_ref, f_ref, dec_ref, cd_ref, s0_ref, s_ref, o_ref, *, chunk, n_chunks, seqs):
    has_init = s0_ref is not None
    fuse = chunk % DK == 0
    q_ref, qd_ref, k_ref, kd_ref = (a_ref.at[:, pl.ds(t * COL_TILE, COL_TILE)] for t in range(4))
    sg_ref = f_ref

    def v_at(rows, h):
        return b_ref[rows, pl.ds(h * DV, DV)]

    for seq in seqs:
        row_slices = [pl.ds((seq * n_chunks + c) * chunk, chunk) for c in range(n_chunks)]
        probs, kvs = {}, {}
        for c, rows in enumerate(row_slices):
            for h in range(N_HEADS):
                kcols = pl.ds(h * DK, DK)
                scores = lax.dot_general(q_ref[rows, kcols], k_ref[rows, kcols], (((1,), (1,)), ((), ())),
                                         preferred_element_type=F32) * dec_ref[h]
                probs[c, h] = scores.astype(BF16)
                kdt = jnp.transpose(kd_ref[rows, kcols].astype(F32)).astype(BF16)
                kvs[c, h] = _dot(kdt, v_at(rows, h))
        if has_init:
            states = [s0_ref[seq, h] for h in range(N_HEADS)]
        else:
            states = [s_ref[seq, h] for h in range(N_HEADS)]
        for c, rows in enumerate(row_slices):
            for h in range(N_HEADS):
                kcols = pl.ds(h * DK, DK)
                ocols = pl.ds(h * DV, DV)
                S = states[h]
                if fuse:
                    o = _dot(jnp.concatenate([probs[c, h], qd_ref[rows, kcols]], axis=1),
                             jnp.concatenate([v_at(rows, h), S.astype(BF16)], axis=0))
                else:
                    o = _dot(probs[c, h], v_at(rows, h)) + _dot(qd_ref[rows, kcols], S.astype(BF16))
                states[h] = S * cd_ref[h] + kvs[c, h]
                o = o * lax.rsqrt(jnp.mean(o * o, axis=-1, keepdims=True) + EPS)
                o_ref[rows, ocols] = (o * sg_ref[rows, ocols]).astype(BF16)
        for h in range(N_HEADS):
            s_ref[seq, h] = states[h]


def _retention_specs(a, b, f, decay, cd, rows, row_block):
    def whole(arr):
        return pl.BlockSpec((rows, arr.shape[1]), lambda *g: (row_block(*g), 0))

    specs = [whole(a), whole(b), whole(f), _resident(decay.shape), pl.BlockSpec(memory_space=pltpu.SMEM)]
    return specs, [a, b, f, decay, cd]


def _retention_mix_kernel(a_ref, b_ref, h_ref, dec_ref, cd_ref, wg0_ref, wg1_ref, wga_ref, wgb_ref,
                          wro_ref, wco_ref, wo_ref, d_ref, s_ref, o_ref, sg_ref, *, chunk, n_chunks):
    @pl.when(pl.program_id(1) == 0)
    def _():
        s_ref[...] = jnp.zeros_like(s_ref)

    h = h_ref[...]
    sg_ref[:, :COL_TILE] = jax.nn.silu(_dot(h, wg0_ref[...]))
    sg_ref[:, COL_TILE:] = jax.nn.silu(_dot(h, wg1_ref[...]))
    _retention_body(a_ref, b_ref, sg_ref, dec_ref, cd_ref, None, s_ref, o_ref,
                    chunk=chunk, n_chunks=n_chunks, seqs=range(1))
    cbv = b_ref[:, 2 * COL_TILE:]
    sig_a = jax.nn.sigmoid(_dot(h, wga_ref[...]))
    sig_b = jax.nn.sigmoid(_dot(h, wgb_ref[...]))
    y_conv = _dot(cbv, wco_ref[...])
    merged = sig_a * _dot(o_ref[...], wro_ref[...]) + sig_b * y_conv
    d_ref[...] = _dot(merged.astype(BF16), wo_ref[...])


def _retention_mix(a, b, h, w_in, decay, cd, wro, wco, wo, *, n_batch, seq_len, chunk, n_chunks):
    T, D = a.shape[0], wo.shape[1]
    rows = n_chunks * chunk
    steps = seq_len // rows

    def row_block(b_, t):
        return b_ * steps + t

    def tile(width, col):
        return pl.BlockSpec((rows, width), lambda b_, t: (row_block(b_, t), col))

    in_specs, args = _retention_specs(a, b, h, decay, cd, rows, row_block)
    in_specs += [pl.BlockSpec((D, COL_TILE), lambda b_, t, tile_=tile_: (0, tile_), pipeline_mode=pl.Buffered(1))
                 for tile_ in GATE_TILES]
    in_specs += [_resident(wro.shape), _resident(wco.shape), _resident(wo.shape)]
    args += [w_in] * len(GATE_TILES) + [wro, wco, wo]
    kernel = functools.partial(_retention_mix_kernel, chunk=chunk, n_chunks=n_chunks)
    return pl.pallas_call(
        kernel,
        grid=(n_batch, steps),
        in_specs=in_specs,
        out_specs=[tile(D, 0), pl.BlockSpec((1, N_HEADS, DK, DV), lambda b_, t: (b_, 0, 0, 0))],
        out_shape=[jax.ShapeDtypeStruct((T, D), F32),
                   jax.ShapeDtypeStruct((n_batch, N_HEADS, DK, DV), F32)],
        scratch_shapes=[pltpu.VMEM((rows, N_HEADS * DV), BF16), pltpu.VMEM((rows, N_HEADS * DV), F32)],
        compiler_params=_params(("parallel", "arbitrary")),
        name="retention_mix",
    )(*args)


def _mix_kernel(x_ref, og_ref, cbv_ref, sa_ref, sb_ref, wro_ref, wco_ref, wo_ref, x1_ref):
    y_ret = _dot(og_ref[...], wro_ref[...])
    y_conv = _dot(cbv_ref[...], wco_ref[...])
    merged = sa_ref[...] * y_ret + sb_ref[...] * y_conv
    x1_ref[...] = x_ref[...] + _dot(merged.astype(BF16), wo_ref[...])


def _mix(x2d, og, b, f, wro, wco, wo, *, tm):
    T, D = x2d.shape

    def tile(width, col):
        return pl.BlockSpec((tm, width), lambda i: (i, col))

    return pl.pallas_call(
        _mix_kernel,
        grid=(T // tm,),
        in_specs=[tile(D, 0), tile(N_HEADS * DV, 0), tile(D, 2), tile(D, 2), tile(D, 3),
                  _resident(wro.shape), _resident(wco.shape), _resident(wo.shape)],
        out_specs=tile(D, 0),
        out_shape=jax.ShapeDtypeStruct((T, D), F32),
        compiler_params=_params(("parallel",)),
        name="mix",
    )(x2d, og, b, f, f, wro, wco, wo)


N_FFN_INPUTS = 9


def _ffn_kernel(*refs, final_norm, side):
    x_ref, p_ref, gf_ref, wu_ref, wd_ref, gp_ref, wpg_ref, wpp_ref, gl_ref = refs[:N_FFN_INPUTS]
    if side is None:
        o_ref, = refs[N_FFN_INPUTS:]
        x = x_ref[...]
    else:
        ret_refs = refs[N_FFN_INPUTS:N_FFN_INPUTS + N_RET_INPUTS]
        s0_ref, base_ref, o_ref, og_ref, s_ref = refs[N_FFN_INPUTS + N_RET_INPUTS:]
        chunk, n_seq = side
        x = base_ref[...] + x_ref[...]
    hf = _rms(x, gf_ref[...]).astype(BF16)
    n_ff = wu_ref.shape[1] // COL_TILE
    acc = x
    for c in range(n_ff):
        cols = pl.ds(c * COL_TILE, COL_TILE)
        hid = jnp.square(jnp.maximum(_dot(hf, wu_ref[:, cols]), 0.0)).astype(BF16)
        acc = acc + _dot(hid, wd_ref[cols, :])
        if side is not None:
            _retention_body(*ret_refs, s0_ref, s_ref, og_ref, chunk=chunk, n_chunks=1,
                            seqs=range(c * n_seq // n_ff, (c + 1) * n_seq // n_ff))
    gate = jax.nn.sigmoid(_dot(_rms(acc, gp_ref[...]).astype(BF16), wpg_ref[...]))
    y = acc + gate * _dot(p_ref[...].astype(BF16), wpp_ref[...])
    if final_norm:
        y = _rms(y, gl_ref[...])
    o_ref[...] = y


def _ffn(x2d, p2d, gf, wu, wd, gp, wpg, wpp, gl, *, tm, final_norm, side=None):
    T, D = x2d.shape
    steps = T // tm
    in_specs = [pl.BlockSpec((tm, D), lambda i: (i, 0)),
                pl.BlockSpec((tm, p2d.shape[1]), lambda i: (i, 0)),
                _resident(gf.shape), _resident(wu.shape), _resident(wd.shape),
                _resident(gp.shape), _resident(wpg.shape), _resident(wpp.shape),
                _resident(gl.shape)]
    args = [x2d, p2d, gf, wu, wd, gp, wpg, wpp, gl]
    out_specs = [pl.BlockSpec((tm, D), lambda i: (i, 0))]
    out_shape = [jax.ShapeDtypeStruct((T, D), F32)]
    kernel_side = None
    if side is not None:
        base, a, b, f, s0, decay, cd, chunk = side
        n_all = s0.shape[0]
        assert n_all % steps == 0 and a.shape[0] == n_all * chunk
        n_seq = n_all // steps
        ret_specs, ret_args = _retention_specs(a, b, f, decay, cd, n_seq * chunk, lambda i: i)
        state_spec = pl.BlockSpec((n_seq,) + s0.shape[1:], lambda i: (i, 0, 0, 0))
        in_specs += ret_specs + [state_spec, pl.BlockSpec((tm, D), lambda i: (i, 0))]
        args += ret_args + [s0, base]
        out_specs += [pl.BlockSpec((n_seq * chunk, N_HEADS * DV), lambda i: (i, 0)), state_spec]
        out_shape += [jax.ShapeDtypeStruct((a.shape[0], N_HEADS * DV), BF16),
                      jax.ShapeDtypeStruct(s0.shape, F32)]
        kernel_side = (chunk, n_seq)
    kernel = functools.partial(_ffn_kernel, final_norm=final_norm, side=kernel_side)
    out = pl.pallas_call(
        kernel,
        grid=(steps,),
        in_specs=in_specs,
        out_specs=out_specs,
        out_shape=out_shape,
        compiler_params=_params(("parallel",)),
        name="ffn" if side is None else "ffn_retention",
    )(*args)
    return out[0] if side is None else out


def _layer(xp, xs, pp, ps, s_ret, s_conv, pos_p, pos_s, w, *, n_prompt, len_prompt, chunk_prompt,
           n_sample, len_sample, final_norm):
    cos_p, sin_p, decay_p, qd_p, kd_p, cd_p = _tables(pos_p, chunk_prompt)
    cos_s, sin_s, decay_s, qd_s, kd_s, cd_s = _tables(pos_s, len_sample)
    ap, bp, hp, conv_p, w_up, w_down = _inproj(
        xp, w["g_mix"], w["w_in"], w["conv_w"], None, cos_p, sin_p, qd_p, kd_p, n_batch=n_prompt,
        seq_len=len_prompt, chunk=chunk_prompt, tm=1024, cast=(w["w_up"], w["w_down"]), gates=False)
    as_, bs, fs, conv_s, w_ret_out, w_conv_out, w_o, w_ple_gate = _inproj(
        xs, w["g_mix"], w["w_in"], w["conv_w"], s_conv, cos_s, sin_s, qd_s, kd_s, n_batch=n_sample,
        seq_len=len_sample, chunk=len_sample, tm=512, cast_chunk_bytes=2 * CAST_CHUNK_BYTES,
        cast=(w["w_ret_out"], w["w_conv_out"], w["w_o"], w["w_ple_gate"]))
    ffn_w = (w["g_ffn"], w_up, w_down, w["g_ple"], w_ple_gate, w["w_ple_proj"], w["g_final"])
    mix_w = (w_ret_out, w_conv_out, w_o)
    dxp, ret_p = _retention_mix(ap, bp, hp, w["w_in"], decay_p, cd_p, *mix_w, n_batch=n_prompt,
                                seq_len=len_prompt, chunk=chunk_prompt, n_chunks=4)
    yp, og_s, ret_s = _ffn(dxp, pp, *ffn_w, tm=512, final_norm=final_norm,
                           side=(xp, as_, bs, fs, s_ret, decay_s, cd_s, len_sample))
    x1s = _mix(xs, og_s, bs, fs, *mix_w, tm=256)
    ys = _ffn(x1s, ps, *ffn_w, tm=512, final_norm=final_norm)
    return yp, ys, ret_p, conv_p, ret_s, conv_s


def kernel(x_prompt, x_sample, state_ret, state_conv, p_prompt, p_sample, g_mix, w_in, w_ret_out,
           conv_w, w_conv_out, w_o, g_ffn, w_up, w_down, g_ple, w_ple_gate, w_ple_proj, g_final):
    depth = w_in.shape[0]
    B, L, D = x_prompt.shape
    Bs, Ls, _ = x_sample.shape
    pos_prompt = jnp.arange(L, dtype=F32)
    pos_sample = PAST_LEN + jnp.arange(Ls, dtype=F32)
    chunk_prompt = min(RET_CHUNK, L)

    hp = x_prompt.reshape(B * L, D)
    hs = x_sample.reshape(Bs * Ls, D)
    rp, cp, rs, cs = [], [], [], []
    for i in range(depth):
        w = {"g_mix": g_mix[i][None], "w_in": w_in[i].astype(BF16),
             "w_ret_out": w_ret_out[i], "conv_w": conv_w[i],
             "w_conv_out": w_conv_out[i], "w_o": w_o[i],
             "g_ffn": g_ffn[i][None], "w_up": w_up[i], "w_down": w_down[i],
             "g_ple": g_ple[i][None], "w_ple_gate": w_ple_gate[i],
             "w_ple_proj": w_ple_proj[i].astype(BF16), "g_final": g_final[None]}
        hp, hs, r_p, c_p, r_s, c_s = _layer(
            hp, hs, p_prompt[i].reshape(B * L, -1), p_sample[i].reshape(Bs * Ls, -1),
            state_ret[i], state_conv[i], pos_prompt, pos_sample, w,
            n_prompt=B, len_prompt=L, chunk_prompt=chunk_prompt, n_sample=Bs, len_sample=Ls,
            final_norm=i == depth - 1)
        rp.append(r_p)
        cp.append(c_p)
        rs.append(r_s)
        cs.append(c_s)
    return (hp.reshape(B, L, D), hs.reshape(Bs, Ls, D), jnp.stack(rp), jnp.stack(cp),
            jnp.stack(rs), jnp.stack(cs))
```

```python
import functools

import jax
import jax.numpy as jnp
from jax import lax
from jax.experimental import pallas as pl
from jax.experimental.pallas import tpu as pltpu

F32 = jnp.float32
BF16 = jnp.bfloat16

N_HEADS = 8
DK = 128
DV = 256
CONV_W = 3
RET_CHUNK = 128
PAST_LEN = 16384
ROPE_BASE = 10000.0
EPS = 1e-6

V7X_SUBLANES = 8
V7X_VMEM_BYTES = 64 * 1024 * 1024
V7X_VMEM_LIMIT_BYTES = V7X_VMEM_BYTES - 4 * 1024 * 1024

COL_TILE = 1024

IN_GROUP_TILES = ((0, 2), (1, 3), (4, 6), (10,), (5, 7), (8, 9))
IN_GROUPS = len(IN_GROUP_TILES)
IN_TILES = sum(len(tiles) for tiles in IN_GROUP_TILES)
IN_GROUP_TILES_NO_GATES = ((0, 2), (1, 3), (6, 7, 8))
GATE_TILES = (4, 5, 9, 10)
W_SLOTS = 3
CAST_CHUNK_BYTES = 512 * 1024


def _rms(x, g):
    return x * lax.rsqrt(jnp.mean(x * x, axis=-1, keepdims=True) + EPS) * g


def _sigmoid(x):
    return 0.5 * jnp.tanh(0.5 * x) + 0.5


def _swish(x):
    hx = 0.5 * x
    return hx + hx * jnp.tanh(hx)


def _dot(a, b):
    return jnp.dot(a, b, preferred_element_type=F32)


def _params(semantics):
    return pltpu.CompilerParams(dimension_semantics=semantics,
                                vmem_limit_bytes=V7X_VMEM_LIMIT_BYTES)


def _resident(shape):
    zeros = (0,) * len(shape)
    return pl.BlockSpec(shape, lambda *_: zeros, pipeline_mode=pl.Buffered(1))


def _tables(pos, chunk):
    inv_freq = ROPE_BASE ** (-jnp.arange(0, DK, 2, dtype=F32) / DK)
    ang = pos.astype(F32)[:, None] * inv_freq[None, :]
    c, s = jnp.cos(ang), jnp.sin(ang)
    cos2 = jnp.concatenate([c, c], axis=-1)
    sin2 = jnp.concatenate([-s, s], axis=-1)
    log_g = jnp.log(1.0 - 2.0 ** (-5.0 - jnp.arange(N_HEADS, dtype=F32)))
    idx = jnp.arange(chunk, dtype=F32)
    diff = idx[:, None] - idx[None, :]
    decay = jnp.where(diff >= 0, jnp.exp(jnp.maximum(diff, 0.0)[None] * log_g[:, None, None]), 0.0)
    q_decay = jnp.exp((idx + 1.0)[None, :] * log_g[:, None])
    k_decay = jnp.exp((chunk - 1.0 - idx)[None, :] * log_g[:, None])
    chunk_decay = jnp.exp(chunk * log_g)
    qd = jnp.repeat(q_decay.T, DK, axis=1)
    kd = jnp.repeat(k_decay.T, DK, axis=1)
    return cos2, sin2, decay, qd, kd, chunk_decay


def _inproj_kernel(*refs, tm, chunk, seq_len, has_init, cast_ranges, gates):
    groups = IN_GROUP_TILES if gates else IN_GROUP_TILES_NO_GATES
    n_groups = len(groups)
    refs = list(refs)
    x_ref, g_ref, w_hbm, cos_ref, sin_ref, qd_ref, kd_ref, cw_ref = refs[:8]
    del refs[:8]
    st_ref = refs.pop(0) if has_init else None
    n_cast = len(cast_ranges)
    cast_in = refs[:n_cast]
    a_ref, b_ref, f_ref, cs_ref = refs[n_cast:n_cast + 4]
    cast_out = refs[n_cast + 4:2 * n_cast + 4]
    if gates:
        h_ref, cx_ref, cb_ref, carry_ref, wbuf, wsem = refs[2 * n_cast + 4:]
    else:
        carry_ref, wbuf, wsem = refs[2 * n_cast + 4:]
        h_ref = f_ref
    i = pl.program_id(0)
    j = pl.program_id(1)
    n_pos = cos_ref.shape[0]

    step = i * n_groups + j
    n_steps = pl.num_programs(0) * n_groups

    for src, dst, (first, last) in zip(cast_in, cast_out, cast_ranges):
        @pl.when((step >= first) & (step < last))
        def _():
            dst[...] = src[...].astype(BF16)

    def w_copies(group):
        slot = group % W_SLOTS
        return [pltpu.make_async_copy(w_hbm.at[:, pl.ds(tile * COL_TILE, COL_TILE)],
                                      wbuf.at[slot, :, pl.ds(t * COL_TILE, COL_TILE)],
                                      wsem.at[slot, t])
                for t, tile in enumerate(groups[group])]

    @pl.when(step == 0)
    def _():
        for group in range(W_SLOTS - 1):
            for copy in w_copies(group):
                copy.start()

    for jj in range(n_groups):
        @pl.when((j == jj) & (step + W_SLOTS - 1 < n_steps))
        def _():
            for copy in w_copies((jj + W_SLOTS - 1) % n_groups):
                copy.start()

        @pl.when(j == jj)
        def _():
            for copy in w_copies(jj):
                copy.wait()

    def proj(jj, t, h=None):
        h = h_ref[...] if h is None else h
        return _dot(h, wbuf[jj % W_SLOTS, :, t * COL_TILE:(t + 1) * COL_TILE])

    def rotary_heads(acc, dec_ref, post_scale):
        plain, scaled = [], []
        for h in range(N_HEADS):
            cols = slice(h * DK, (h + 1) * DK)
            t = acc[:, cols]
            t3 = t.reshape(tm // n_pos, n_pos, DK)
            r3 = pltpu.roll(t, DK // 2, 1).reshape(tm // n_pos, n_pos, DK)
            r = (t3 * cos_ref[...][None] + r3 * sin_ref[...][None]).reshape(tm, DK)
            if post_scale != 1.0:
                r = r * post_scale
            plain.append(r.astype(BF16))
            d = r.reshape(tm // chunk, chunk, DK) * dec_ref[:, cols][None]
            scaled.append(d.reshape(tm, DK).astype(BF16))
        return jnp.concatenate(plain + scaled, axis=1)

    def conv(u, cb):
        def patch(pos, prev2, prev1, r1, r2):
            return (jnp.where(pos == 0, prev1, r1),
                    jnp.where(pos == 0, prev2, jnp.where(pos == 1, prev1, r2)))

        if has_init:
            n = tm // seq_len
            prev2 = jnp.broadcast_to(st_ref[:, 0:1, :], (n, seq_len, u.shape[1])).reshape(u.shape)
            prev1 = jnp.broadcast_to(st_ref[:, 1:2, :], (n, seq_len, u.shape[1])).reshape(u.shape)
            u1, u2 = patch(lax.broadcasted_iota(jnp.int32, u.shape, 0) % seq_len, prev2, prev1,
                           pltpu.roll(u, 1, 0), pltpu.roll(u, 2, 0))
        else:
            s = V7X_SUBLANES
            first = (i % (seq_len // tm)) == 0
            halo = jnp.where(first, 0.0, carry_ref[...])
            prev2 = jnp.broadcast_to(halo[s - 2:s - 1, :], halo.shape)
            prev1 = jnp.broadcast_to(halo[s - 1:s, :], halo.shape)
            r1, r2 = pltpu.roll(u, 1, 0), pltpu.roll(u, 2, 0)
            h1, h2 = patch(lax.broadcasted_iota(jnp.int32, halo.shape, 0), prev2, prev1, r1[:s], r2[:s])
            u1 = jnp.concatenate([h1, r1[s:]], axis=0)
            u2 = jnp.concatenate([h2, r2[s:]], axis=0)
        cv = cw_ref[0:1, :] * u2 + cw_ref[1:2, :] * u1 + cw_ref[2:3, :] * u
        b_ref[...] = (cb * cv).astype(BF16)
        if has_init:
            cs_ref[...] = u.reshape(tm // seq_len, seq_len, u.shape[1])[:, seq_len - (CONV_W - 1):, :]
        else:
            carry_ref[...] = u[tm - V7X_SUBLANES:, :]
            cs_ref[0] = u[tm - (CONV_W - 1):, :]

    @pl.when(j == 0)
    def _():
        h = _rms(x_ref[...], g_ref[...]).astype(BF16)
        h_ref[...] = h
        a_ref[...] = rotary_heads(proj(0, 0, h), qd_ref, DK ** -0.5)
        b_ref[...] = proj(0, 1, h).astype(BF16)

    @pl.when(j == 1)
    def _():
        a_ref[...] = rotary_heads(proj(1, 0), kd_ref, 1.0)
        b_ref[...] = proj(1, 1).astype(BF16)

    if gates:
        @pl.when(j == 2)
        def _():
            f_ref[...] = _swish(proj(2, 0))
            cx_ref[...] = proj(2, 1)

        @pl.when(j == 3)
        def _():
            f_ref[...] = _sigmoid(proj(3, 0))

        @pl.when(j == 4)
        def _():
            f_ref[...] = _swish(proj(4, 0))
            cb_ref[...] = proj(4, 1)

        @pl.when(j == 5)
        def _():
            f_ref[...] = _sigmoid(proj(5, 1))
            conv(proj(5, 0) * cx_ref[...], cb_ref[...])
    else:
        @pl.when(j == 2)
        def _():
            cx, cb, cc = proj(2, 0), proj(2, 1), proj(2, 2)
            conv(cc * cx, cb)


def _inproj(x2d, g, w, cw, conv_state, cos2, sin2, qd, kd, *, n_batch, seq_len, chunk, tm, cast=(),
            cast_chunk_bytes=CAST_CHUNK_BYTES, gates=True):
    T, D = x2d.shape
    has_init = conv_state is not None
    groups = IN_GROUP_TILES if gates else IN_GROUP_TILES_NO_GATES
    n_groups = len(groups)
    group_width = max(len(tiles) for tiles in groups)
    assert w.shape[1] == IN_TILES * COL_TILE and D == COL_TILE and n_groups % W_SLOTS == 0
    if has_init:
        assert tm % seq_len == 0
        n = tm // seq_len
        pos_spec = pl.BlockSpec((seq_len, DK), lambda i, j: (0, 0))
        st_specs = [pl.BlockSpec((n, CONV_W - 1, D), lambda i, j: (i, 0, 0))]
        st_args = [conv_state]
        cs_spec = pl.BlockSpec((n, CONV_W - 1, D), lambda i, j: (i, 0, 0))
    else:
        assert seq_len % tm == 0 and tm % chunk == 0
        tiles = seq_len // tm
        pos_spec = pl.BlockSpec((tm, DK), lambda i, j: (i % tiles, 0))
        st_specs, st_args = [], []
        cs_spec = pl.BlockSpec((1, CONV_W - 1, D), lambda i, j: (i // tiles, 0, 0))

    if gates:
        def b_col(j):
            return jnp.where(j == 0, 0, jnp.where(j < 5, 1, 2))

        def f_col(j):
            return jnp.where(j <= 2, 0, jnp.where(j == 3, 3, j - 3))

        third_shape = jax.ShapeDtypeStruct((T, 4 * COL_TILE), F32)
        conv_scratch = [pltpu.VMEM((tm, D), BF16), pltpu.VMEM((tm, D), F32), pltpu.VMEM((tm, D), F32)]
    else:
        def b_col(j):
            return j

        def f_col(j):
            return 0

        third_shape = jax.ShapeDtypeStruct((T, D), BF16)
        conv_scratch = []

    n_steps = (T // tm) * n_groups
    cast_specs_in, cast_specs_out, cast_shapes, cast_ranges = [], [], [], []
    first = 0
    for m in cast:
        rows = cast_chunk_bytes // (m.shape[1] * m.dtype.itemsize)
        n_chunks = m.shape[0] // rows
        assert m.shape[0] % rows == 0 and n_chunks <= n_steps
        if first + n_chunks > n_steps:
            first = 0

        def chunk_map(i, j, first=first, n_chunks=n_chunks):
            return jnp.clip(i * n_groups + j - first, 0, n_chunks - 1), 0

        cast_specs_in.append(pl.BlockSpec((rows, m.shape[1]), chunk_map))
        cast_specs_out.append(pl.BlockSpec((rows, m.shape[1]), chunk_map))
        cast_shapes.append(jax.ShapeDtypeStruct(m.shape, BF16))
        cast_ranges.append((first, first + n_chunks))
        first += n_chunks

    kernel = functools.partial(_inproj_kernel, tm=tm, chunk=chunk, seq_len=seq_len, has_init=has_init,
                               cast_ranges=tuple(cast_ranges), gates=gates)
    return pl.pallas_call(
        kernel,
        grid=(T // tm, n_groups),
        in_specs=[pl.BlockSpec((tm, D), lambda i, j: (i, 0)),
                  pl.BlockSpec((1, D), lambda i, j: (0, 0)),
                  pl.BlockSpec(memory_space=pl.ANY),
                  pos_spec, pos_spec,
                  _resident(qd.shape), _resident(kd.shape), _resident(cw.shape)] + st_specs + cast_specs_in,
        out_specs=[pl.BlockSpec((tm, 2 * COL_TILE), lambda i, j: (i, jnp.minimum(j, 1))),
                   pl.BlockSpec((tm, COL_TILE), lambda i, j: (i, b_col(j))),
                   pl.BlockSpec((tm, COL_TILE), lambda i, j: (i, f_col(j))),
                   cs_spec] + cast_specs_out,
        out_shape=[jax.ShapeDtypeStruct((T, 4 * COL_TILE), BF16),
                   jax.ShapeDtypeStruct((T, 3 * COL_TILE), BF16),
                   third_shape,
                   jax.ShapeDtypeStruct((n_batch, CONV_W - 1, D), F32)] + cast_shapes,
        scratch_shapes=conv_scratch + [pltpu.VMEM((V7X_SUBLANES, D), F32),
                                       pltpu.VMEM((W_SLOTS, D, group_width * COL_TILE), BF16),
                                       pltpu.SemaphoreType.DMA((W_SLOTS, group_width))],
        compiler_params=_params(("arbitrary", "arbitrary")),
        name="inproj_init" if has_init else "inproj",
    )(x2d, g, w, cos2, sin2, qd, kd, cw, *st_args, *cast)


N_RET_INPUTS = 5


def _retention_body(a_ref, b_ref, f_ref, dec_ref, cd_ref, s0_ref, s_ref, o_ref, *, chunk, n_chunks, seqs):
    has_init = s0_ref is not None
    fuse = chunk % DK == 0
    q_ref, qd_ref, k_ref, kd_ref = (a_ref.at[:, pl.ds(t * COL_TILE, COL_TILE)] for t in range(4))
    sg_ref = f_ref

    def v_at(rows, h):
        return b_ref[rows, pl.ds(h * DV, DV)]

    for seq in seqs:
        row_slices = [pl.ds((seq * n_chunks + c) * chunk, chunk) for c in range(n_chunks)]
        probs, kvs = {}, {}
        for c, rows in enumerate(row_slices):
            for h in range(N_HEADS):
                kcols = pl.ds(h * DK, DK)
                scores = lax.dot_general(q_ref[rows, kcols], k_ref[rows, kcols], (((1,), (1,)), ((), ())),
                                         preferred_element_type=F32) * dec_ref[h]
                probs[c, h] = scores.astype(BF16)
                kdt = jnp.transpose(kd_ref[rows, kcols].astype(F32)).astype(BF16)
                kvs[c, h] = _dot(kdt, v_at(rows, h))
        if has_init:
            states = [s0_ref[seq, h] for h in range(N_HEADS)]
        else:
            states = [s_ref[seq, h] for h in range(N_HEADS)]
        for c, rows in enumerate(row_slices):
            for h in range(N_HEADS):
                kcols = pl.ds(h * DK, DK)
                ocols = pl.ds(h * DV, DV)
                S = states[h]
                if fuse:
                    o = _dot(jnp.concatenate([probs[c, h], qd_ref[rows, kcols]], axis=1),
                             jnp.concatenate([v_at(rows, h), S.astype(BF16)], axis=0))
                else:
                    o = _dot(probs[c, h], v_at(rows, h)) + _dot(qd_ref[rows, kcols], S.astype(BF16))
                states[h] = S * cd_ref[h] + kvs[c, h]
                o = o * lax.rsqrt(jnp.mean(o * o, axis=-1, keepdims=True) + EPS)
                o_ref[rows, ocols] = (o * sg_ref[rows, ocols]).astype(BF16)
        for h in range(N_HEADS):
            s_ref[seq, h] = states[h]


def _retention_specs(a, b, f, decay, cd, rows, row_block):
    def whole(arr):
        return pl.BlockSpec((rows, arr.shape[1]), lambda *g: (row_block(*g), 0))

    specs = [whole(a), whole(b), whole(f), _resident(decay.shape), pl.BlockSpec(memory_space=pltpu.SMEM)]
    return specs, [a, b, f, decay, cd]


def _retention_mix_kernel(a_ref, b_ref, h_ref, dec_ref, cd_ref, wg0_ref, wg1_ref, wga_ref, wgb_ref,
                          wro_ref, wco_ref, wo_ref, d_ref, s_ref, o_ref, sg_ref, *, chunk, n_chunks):
    @pl.when(pl.program_id(1) == 0)
    def _():
        s_ref[...] = jnp.zeros_like(s_ref)

    h = h_ref[...]
    sg_ref[:, :COL_TILE] = _swish(_dot(h, wg0_ref[...]))
    sg_ref[:, COL_TILE:] = _swish(_dot(h, wg1_ref[...]))
    _retention_body(a_ref, b_ref, sg_ref, dec_ref, cd_ref, None, s_ref, o_ref,
                    chunk=chunk, n_chunks=n_chunks, seqs=range(1))
    cbv = b_ref[:, 2 * COL_TILE:]
    sig_a = _sigmoid(_dot(h, wga_ref[...]))
    sig_b = _sigmoid(_dot(h, wgb_ref[...]))
    y_conv = _dot(cbv, wco_ref[...])
    merged = sig_a * _dot(o_ref[...], wro_ref[...]) + sig_b * y_conv
    d_ref[...] = _dot(merged.astype(BF16), wo_ref[...])


def _retention_mix(a, b, h, w_in, decay, cd, wro, wco, wo, *, n_batch, seq_len, chunk, n_chunks):
    T, D = a.shape[0], wo.shape[1]
    rows = n_chunks * chunk
    steps = seq_len // rows

    def row_block(b_, t):
        return b_ * steps + t

    def tile(width, col):
        return pl.BlockSpec((rows, width), lambda b_, t: (row_block(b_, t), col))

    in_specs, args = _retention_specs(a, b, h, decay, cd, rows, row_block)
    in_specs += [pl.BlockSpec((D, COL_TILE), lambda b_, t, tile_=tile_: (0, tile_), pipeline_mode=pl.Buffered(1))
                 for tile_ in GATE_TILES]
    in_specs += [_resident(wro.shape), _resident(wco.shape), _resident(wo.shape)]
    args += [w_in] * len(GATE_TILES) + [wro, wco, wo]
    kernel = functools.partial(_retention_mix_kernel, chunk=chunk, n_chunks=n_chunks)
    return pl.pallas_call(
        kernel,
        grid=(n_batch, steps),
        in_specs=in_specs,
        out_specs=[tile(D, 0), pl.BlockSpec((1, N_HEADS, DK, DV), lambda b_, t: (b_, 0, 0, 0))],
        out_shape=[jax.ShapeDtypeStruct((T, D), F32),
                   jax.ShapeDtypeStruct((n_batch, N_HEADS, DK, DV), F32)],
        scratch_shapes=[pltpu.VMEM((rows, N_HEADS * DV), BF16), pltpu.VMEM((rows, N_HEADS * DV), F32)],
        compiler_params=_params(("parallel", "arbitrary")),
        name="retention_mix",
    )(*args)


def _mix_kernel(x_ref, og_ref, cbv_ref, sa_ref, sb_ref, wro_ref, wco_ref, wo_ref, x1_ref):
    y_ret = _dot(og_ref[...], wro_ref[...])
    y_conv = _dot(cbv_ref[...], wco_ref[...])
    merged = sa_ref[...] * y_ret + sb_ref[...] * y_conv
    x1_ref[...] = x_ref[...] + _dot(merged.astype(BF16), wo_ref[...])


def _mix(x2d, og, b, f, wro, wco, wo, *, tm):
    T, D = x2d.shape

    def tile(width, col):
        return pl.BlockSpec((tm, width), lambda i: (i, col))

    return pl.pallas_call(
        _mix_kernel,
        grid=(T // tm,),
        in_specs=[tile(D, 0), tile(N_HEADS * DV, 0), tile(D, 2), tile(D, 2), tile(D, 3),
                  _resident(wro.shape), _resident(wco.shape), _resident(wo.shape)],
        out_specs=tile(D, 0),
        out_shape=jax.ShapeDtypeStruct((T, D), F32),
        compiler_params=_params(("parallel",)),
        name="mix",
    )(x2d, og, b, f, f, wro, wco, wo)


N_FFN_INPUTS = 9


def _ffn_kernel(*refs, final_norm, side):
    x_ref, p_ref, gf_ref, wu_ref, wd_ref, gp_ref, wpg_ref, wpp_ref, gl_ref = refs[:N_FFN_INPUTS]
    if side is None:
        o_ref, = refs[N_FFN_INPUTS:]
        x = x_ref[...]
    else:
        ret_refs = refs[N_FFN_INPUTS:N_FFN_INPUTS + N_RET_INPUTS]
        s0_ref, base_ref, o_ref, og_ref, s_ref = refs[N_FFN_INPUTS + N_RET_INPUTS:]
        chunk, n_seq = side
        x = base_ref[...] + x_ref[...]
    hf = _rms(x, gf_ref[...]).astype(BF16)
    n_ff = wu_ref.shape[1] // COL_TILE
    acc = x
    for c in range(n_ff):
        cols = pl.ds(c * COL_TILE, COL_TILE)
        hid = jnp.square(jnp.maximum(_dot(hf, wu_ref[:, cols]), 0.0)).astype(BF16)
        acc = acc + _dot(hid, wd_ref[cols, :])
        if side is not None:
            _retention_body(*ret_refs, s0_ref, s_ref, og_ref, chunk=chunk, n_chunks=1,
                            seqs=range(c * n_seq // n_ff, (c + 1) * n_seq // n_ff))
    gate = _sigmoid(_dot(_rms(acc, gp_ref[...]).astype(BF16), wpg_ref[...]))
    y = acc + gate * _dot(p_ref[...].astype(BF16), wpp_ref[...])
    if final_norm:
        y = _rms(y, gl_ref[...])
    o_ref[...] = y


def _ffn(x2d, p2d, gf, wu, wd, gp, wpg, wpp, gl, *, tm, final_norm, side=None):
    T, D = x2d.shape
    steps = T // tm
    in_specs = [pl.BlockSpec((tm, D), lambda i: (i, 0)),
                pl.BlockSpec((tm, p2d.shape[1]), lambda i: (i, 0)),
                _resident(gf.shape), _resident(wu.shape), _resident(wd.shape),
                _resident(gp.shape), _resident(wpg.shape), _resident(wpp.shape),
                _resident(gl.shape)]
    args = [x2d, p2d, gf, wu, wd, gp, wpg, wpp, gl]
    out_specs = [pl.BlockSpec((tm, D), lambda i: (i, 0))]
    out_shape = [jax.ShapeDtypeStruct((T, D), F32)]
    kernel_side = None
    if side is not None:
        base, a, b, f, s0, decay, cd, chunk = side
        n_all = s0.shape[0]
        assert n_all % steps == 0 and a.shape[0] == n_all * chunk
        n_seq = n_all // steps
        ret_specs, ret_args = _retention_specs(a, b, f, decay, cd, n_seq * chunk, lambda i: i)
        state_spec = pl.BlockSpec((n_seq,) + s0.shape[1:], lambda i: (i, 0, 0, 0))
        in_specs += ret_specs + [state_spec, pl.BlockSpec((tm, D), lambda i: (i, 0))]
        args += ret_args + [s0, base]
        out_specs += [pl.BlockSpec((n_seq * chunk, N_HEADS * DV), lambda i: (i, 0)), state_spec]
        out_shape += [jax.ShapeDtypeStruct((a.shape[0], N_HEADS * DV), BF16),
                      jax.ShapeDtypeStruct(s0.shape, F32)]
        kernel_side = (chunk, n_seq)
    kernel = functools.partial(_ffn_kernel, final_norm=final_norm, side=kernel_side)
    out = pl.pallas_call(
        kernel,
        grid=(steps,),
        in_specs=in_specs,
        out_specs=out_specs,
        out_shape=out_shape,
        compiler_params=_params(("parallel",)),
        name="ffn" if side is None else "ffn_retention",
    )(*args)
    return out[0] if side is None else out


def _layer(xp, xs, pp, ps, s_ret, s_conv, pos_p, pos_s, w, *, n_prompt, len_prompt, chunk_prompt,
           n_sample, len_sample, final_norm):
    cos_p, sin_p, decay_p, qd_p, kd_p, cd_p = _tables(pos_p, chunk_prompt)
    cos_s, sin_s, decay_s, qd_s, kd_s, cd_s = _tables(pos_s, len_sample)
    ap, bp, hp, conv_p, w_up, w_down = _inproj(
        xp, w["g_mix"], w["w_in"], w["conv_w"], None, cos_p, sin_p, qd_p, kd_p, n_batch=n_prompt,
        seq_len=len_prompt, chunk=chunk_prompt, tm=1024, cast=(w["w_up"], w["w_down"]), gates=False)
    as_, bs, fs, conv_s, w_ret_out, w_conv_out, w_o, w_ple_gate = _inproj(
        xs, w["g_mix"], w["w_in"], w["conv_w"], s_conv, cos_s, sin_s, qd_s, kd_s, n_batch=n_sample,
        seq_len=len_sample, chunk=len_sample, tm=512, cast_chunk_bytes=2 * CAST_CHUNK_BYTES,
        cast=(w["w_ret_out"], w["w_conv_out"], w["w_o"], w["w_ple_gate"]))
    ffn_w = (w["g_ffn"], w_up, w_down, w["g_ple"], w_ple_gate, w["w_ple_proj"], w["g_final"])
    mix_w = (w_ret_out, w_conv_out, w_o)
    dxp, ret_p = _retention_mix(ap, bp, hp, w["w_in"], decay_p, cd_p, *mix_w, n_batch=n_prompt,
                                seq_len=len_prompt, chunk=chunk_prompt, n_chunks=4)
    yp, og_s, ret_s = _ffn(dxp, pp, *ffn_w, tm=512, final_norm=final_norm,
                           side=(xp, as_, bs, fs, s_ret, decay_s, cd_s, len_sample))
    x1s = _mix(xs, og_s, bs, fs, *mix_w, tm=256)
    ys = _ffn(x1s, ps, *ffn_w, tm=512, final_norm=final_norm)
    return yp, ys, ret_p, conv_p, ret_s, conv_s


def kernel(x_prompt, x_sample, state_ret, state_conv, p_prompt, p_sample, g_mix, w_in, w_ret_out,
           conv_w, w_conv_out, w_o, g_ffn, w_up, w_down, g_ple, w_ple_gate, w_ple_proj, g_final):
    depth = w_in.shape[0]
    B, L, D = x_prompt.shape
    Bs, Ls, _ = x_sample.shape
    pos_prompt = jnp.arange(L, dtype=F32)
    pos_sample = PAST_LEN + jnp.arange(Ls, dtype=F32)
    chunk_prompt = min(RET_CHUNK, L)

    hp = x_prompt.reshape(B * L, D)
    hs = x_sample.reshape(Bs * Ls, D)
    rp, cp, rs, cs = [], [], [], []
    for i in range(depth):
        w = {"g_mix": g_mix[i][None], "w_in": w_in[i].astype(BF16),
             "w_ret_out": w_ret_out[i], "conv_w": conv_w[i],
             "w_conv_out": w_conv_out[i], "w_o": w_o[i],
             "g_ffn": g_ffn[i][None], "w_up": w_up[i], "w_down": w_down[i],
             "g_ple": g_ple[i][None], "w_ple_gate": w_ple_gate[i],
             "w_ple_proj": w_ple_proj[i].astype(BF16), "g_final": g_final[None]}
        hp, hs, r_p, c_p, r_s, c_s = _layer(
            hp, hs, p_prompt[i].reshape(B * L, -1), p_sample[i].reshape(Bs * Ls, -1),
            state_ret[i], state_conv[i], pos_prompt, pos_sample, w,
            n_prompt=B, len_prompt=L, chunk_prompt=chunk_prompt, n_sample=Bs, len_sample=Ls,
            final_norm=i == depth - 1)
        rp.append(r_p)
        cp.append(c_p)
        rs.append(r_s)
        cs.append(c_s)
    return (hp.reshape(B, L, D), hs.reshape(Bs, Ls, D), jnp.stack(rp), jnp.stack(cp),
            jnp.stack(rs), jnp.stack(cs))
```

```python
import functools

import jax
import jax.numpy as jnp
from jax import lax
from jax.experimental import pallas as pl
from jax.experimental.pallas import tpu as pltpu

F32 = jnp.float32
BF16 = jnp.bfloat16

N_HEADS = 8
DK = 128
DV = 256
CONV_W = 3
RET_CHUNK = 128
PAST_LEN = 16384
ROPE_BASE = 10000.0
EPS = 1e-6

V7X_SUBLANES = 8
V7X_VMEM_BYTES = 64 * 1024 * 1024
V7X_VMEM_LIMIT_BYTES = V7X_VMEM_BYTES - 4 * 1024 * 1024

COL_TILE = 1024

IN_GROUP_TILES = ((0, 2), (1, 3), (4, 6), (10,), (5, 7), (8, 9))
IN_GROUPS = len(IN_GROUP_TILES)
IN_TILES = sum(len(tiles) for tiles in IN_GROUP_TILES)
IN_GROUP_TILES_NO_GATES = ((0, 2), (1, 3), (6, 7, 8))
GATE_TILES = (4, 5, 9, 10)
W_SLOTS = 3
RING_DMA_PRIORITY = 1
CAST_CHUNK_BYTES = 512 * 1024


def _rms(x, g):
    return x * lax.rsqrt(jnp.mean(x * x, axis=-1, keepdims=True) + EPS) * g


def _sigmoid(x):
    return 0.5 * jnp.tanh(0.5 * x) + 0.5


def _swish(x):
    hx = 0.5 * x
    return hx + hx * jnp.tanh(hx)


def _dot(a, b):
    return jnp.dot(a, b, preferred_element_type=F32)


def _params(semantics):
    return pltpu.CompilerParams(dimension_semantics=semantics,
                                vmem_limit_bytes=V7X_VMEM_LIMIT_BYTES)


def _resident(shape):
    zeros = (0,) * len(shape)
    return pl.BlockSpec(shape, lambda *_: zeros, pipeline_mode=pl.Buffered(1))


def _tables(pos, chunk):
    inv_freq = ROPE_BASE ** (-jnp.arange(0, DK, 2, dtype=F32) / DK)
    ang = pos.astype(F32)[:, None] * inv_freq[None, :]
    c, s = jnp.cos(ang), jnp.sin(ang)
    cos2 = jnp.concatenate([c, c], axis=-1)
    sin2 = jnp.concatenate([-s, s], axis=-1)
    log_g = jnp.log(1.0 - 2.0 ** (-5.0 - jnp.arange(N_HEADS, dtype=F32)))
    idx = jnp.arange(chunk, dtype=F32)
    diff = idx[:, None] - idx[None, :]
    decay = jnp.where(diff >= 0, jnp.exp(jnp.maximum(diff, 0.0)[None] * log_g[:, None, None]), 0.0)
    q_decay = jnp.exp((idx + 1.0)[None, :] * log_g[:, None])
    k_decay = jnp.exp((chunk - 1.0 - idx)[None, :] * log_g[:, None])
    chunk_decay = jnp.exp(chunk * log_g)
    qd = jnp.repeat(q_decay.T, DK, axis=1)
    kd = jnp.repeat(k_decay.T, DK, axis=1)
    return cos2, sin2, decay, qd, kd, chunk_decay


def _inproj_kernel(*refs, tm, chunk, seq_len, has_init, cast_ranges, gates):
    groups = IN_GROUP_TILES if gates else IN_GROUP_TILES_NO_GATES
    n_groups = len(groups)
    refs = list(refs)
    x_ref, g_ref, w_hbm, cos_ref, sin_ref, qd_ref, kd_ref, cw_ref = refs[:8]
    del refs[:8]
    st_ref = refs.pop(0) if has_init else None
    n_cast = len(cast_ranges)
    cast_in = refs[:n_cast]
    a_ref, b_ref, f_ref, cs_ref = refs[n_cast:n_cast + 4]
    cast_out = refs[n_cast + 4:2 * n_cast + 4]
    if gates:
        h_ref, cx_ref, cb_ref, carry_ref, wbuf, wsem = refs[2 * n_cast + 4:]
    else:
        carry_ref, wbuf, wsem = refs[2 * n_cast + 4:]
        h_ref = f_ref
    i = pl.program_id(0)
    j = pl.program_id(1)
    n_pos = cos_ref.shape[0]

    step = i * n_groups + j
    n_steps = pl.num_programs(0) * n_groups

    for src, dst, (first, last) in zip(cast_in, cast_out, cast_ranges):
        @pl.when((step >= first) & (step < last))
        def _():
            dst[...] = src[...].astype(BF16)

    def w_copies(group):
        slot = group % W_SLOTS
        return [pltpu.make_async_copy(w_hbm.at[:, pl.ds(tile * COL_TILE, COL_TILE)],
                                      wbuf.at[slot, :, pl.ds(t * COL_TILE, COL_TILE)],
                                      wsem.at[slot, t])
                for t, tile in enumerate(groups[group])]

    @pl.when(step == 0)
    def _():
        for group in range(W_SLOTS - 1):
            for copy in w_copies(group):
                copy.start(priority=RING_DMA_PRIORITY)

    for jj in range(n_groups):
        @pl.when((j == jj) & (step + W_SLOTS - 1 < n_steps))
        def _():
            for copy in w_copies((jj + W_SLOTS - 1) % n_groups):
                copy.start(priority=RING_DMA_PRIORITY)

        @pl.when(j == jj)
        def _():
            for copy in w_copies(jj):
                copy.wait()

    def proj(jj, t, h=None):
        h = h_ref[...] if h is None else h
        return _dot(h, wbuf[jj % W_SLOTS, :, t * COL_TILE:(t + 1) * COL_TILE])

    def rotary_heads(acc, dec_ref, post_scale):
        plain, scaled = [], []
        for h in range(N_HEADS):
            cols = slice(h * DK, (h + 1) * DK)
            t = acc[:, cols]
            t3 = t.reshape(tm // n_pos, n_pos, DK)
            r3 = pltpu.roll(t, DK // 2, 1).reshape(tm // n_pos, n_pos, DK)
            r = (t3 * cos_ref[...][None] + r3 * sin_ref[...][None]).reshape(tm, DK)
            if post_scale != 1.0:
                r = r * post_scale
            plain.append(r.astype(BF16))
            d = r.reshape(tm // chunk, chunk, DK) * dec_ref[:, cols][None]
            scaled.append(d.reshape(tm, DK).astype(BF16))
        return jnp.concatenate(plain + scaled, axis=1)

    def conv(u, cb):
        def patch(pos, prev2, prev1, r1, r2):
            return (jnp.where(pos == 0, prev1, r1),
                    jnp.where(pos == 0, prev2, jnp.where(pos == 1, prev1, r2)))

        if has_init:
            n = tm // seq_len
            prev2 = jnp.broadcast_to(st_ref[:, 0:1, :], (n, seq_len, u.shape[1])).reshape(u.shape)
            prev1 = jnp.broadcast_to(st_ref[:, 1:2, :], (n, seq_len, u.shape[1])).reshape(u.shape)
            u1, u2 = patch(lax.broadcasted_iota(jnp.int32, u.shape, 0) % seq_len, prev2, prev1,
                           pltpu.roll(u, 1, 0), pltpu.roll(u, 2, 0))
        else:
            s = V7X_SUBLANES
            first = (i % (seq_len // tm)) == 0
            halo = jnp.where(first, 0.0, carry_ref[...])
            prev2 = jnp.broadcast_to(halo[s - 2:s - 1, :], halo.shape)
            prev1 = jnp.broadcast_to(halo[s - 1:s, :], halo.shape)
            r1, r2 = pltpu.roll(u, 1, 0), pltpu.roll(u, 2, 0)
            h1, h2 = patch(lax.broadcasted_iota(jnp.int32, halo.shape, 0), prev2, prev1, r1[:s], r2[:s])
            u1 = jnp.concatenate([h1, r1[s:]], axis=0)
            u2 = jnp.concatenate([h2, r2[s:]], axis=0)
        cv = cw_ref[0:1, :] * u2 + cw_ref[1:2, :] * u1 + cw_ref[2:3, :] * u
        b_ref[...] = (cb * cv).astype(BF16)
        if has_init:
            cs_ref[...] = u.reshape(tm // seq_len, seq_len, u.shape[1])[:, seq_len - (CONV_W - 1):, :]
        else:
            carry_ref[...] = u[tm - V7X_SUBLANES:, :]
            cs_ref[0] = u[tm - (CONV_W - 1):, :]

    @pl.when(j == 0)
    def _():
        h = _rms(x_ref[...], g_ref[...]).astype(BF16)
        h_ref[...] = h
        a_ref[...] = rotary_heads(proj(0, 0, h), qd_ref, DK ** -0.5)
        b_ref[...] = proj(0, 1, h).astype(BF16)

    @pl.when(j == 1)
    def _():
        a_ref[...] = rotary_heads(proj(1, 0), kd_ref, 1.0)
        b_ref[...] = proj(1, 1).astype(BF16)

    if gates:
        @pl.when(j == 2)
        def _():
            f_ref[...] = _swish(proj(2, 0))
            cx_ref[...] = proj(2, 1)

        @pl.when(j == 3)
        def _():
            f_ref[...] = _sigmoid(proj(3, 0))

        @pl.when(j == 4)
        def _():
            f_ref[...] = _swish(proj(4, 0))
            cb_ref[...] = proj(4, 1)

        @pl.when(j == 5)
        def _():
            f_ref[...] = _sigmoid(proj(5, 1))
            conv(proj(5, 0) * cx_ref[...], cb_ref[...])
    else:
        @pl.when(j == 2)
        def _():
            cx, cb, cc = proj(2, 0), proj(2, 1), proj(2, 2)
            conv(cc * cx, cb)


def _inproj(x2d, g, w, cw, conv_state, cos2, sin2, qd, kd, *, n_batch, seq_len, chunk, tm, cast=(),
            cast_chunk_bytes=CAST_CHUNK_BYTES, gates=True):
    T, D = x2d.shape
    has_init = conv_state is not None
    groups = IN_GROUP_TILES if gates else IN_GROUP_TILES_NO_GATES
    n_groups = len(groups)
    group_width = max(len(tiles) for tiles in groups)
    assert w.shape[1] == IN_TILES * COL_TILE and D == COL_TILE and n_groups % W_SLOTS == 0
    if has_init:
        assert tm % seq_len == 0
        n = tm // seq_len
        pos_spec = pl.BlockSpec((seq_len, DK), lambda i, j: (0, 0))
        st_specs = [pl.BlockSpec((n, CONV_W - 1, D), lambda i, j: (i, 0, 0))]
        st_args = [conv_state]
        cs_spec = pl.BlockSpec((n, CONV_W - 1, D), lambda i, j: (i, 0, 0))
    else:
        assert seq_len % tm == 0 and tm % chunk == 0
        tiles = seq_len // tm
        pos_spec = pl.BlockSpec((tm, DK), lambda i, j: (i % tiles, 0))
        st_specs, st_args = [], []
        cs_spec = pl.BlockSpec((1, CONV_W - 1, D), lambda i, j: (i // tiles, 0, 0))

    if gates:
        def b_col(j):
            return jnp.where(j == 0, 0, jnp.where(j < 5, 1, 2))

        def f_col(j):
            return jnp.where(j <= 2, 0, jnp.where(j == 3, 3, j - 3))

        third_shape = jax.ShapeDtypeStruct((T, 4 * COL_TILE), F32)
        conv_scratch = [pltpu.VMEM((tm, D), BF16), pltpu.VMEM((tm, D), F32), pltpu.VMEM((tm, D), F32)]
    else:
        def b_col(j):
            return j

        def f_col(j):
            return 0

        third_shape = jax.ShapeDtypeStruct((T, D), BF16)
        conv_scratch = []

    n_steps = (T // tm) * n_groups
    cast_specs_in, cast_specs_out, cast_shapes, cast_ranges = [], [], [], []
    first = 0
    for m in cast:
        rows = cast_chunk_bytes // (m.shape[1] * m.dtype.itemsize)
        n_chunks = m.shape[0] // rows
        assert m.shape[0] % rows == 0 and n_chunks <= n_steps
        if first + n_chunks > n_steps:
            first = 0

        def chunk_map(i, j, first=first, n_chunks=n_chunks):
            return jnp.clip(i * n_groups + j - first, 0, n_chunks - 1), 0

        cast_specs_in.append(pl.BlockSpec((rows, m.shape[1]), chunk_map))
        cast_specs_out.append(pl.BlockSpec((rows, m.shape[1]), chunk_map))
        cast_shapes.append(jax.ShapeDtypeStruct(m.shape, BF16))
        cast_ranges.append((first, first + n_chunks))
        first += n_chunks

    kernel = functools.partial(_inproj_kernel, tm=tm, chunk=chunk, seq_len=seq_len, has_init=has_init,
                               cast_ranges=tuple(cast_ranges), gates=gates)
    return pl.pallas_call(
        kernel,
        grid=(T // tm, n_groups),
        in_specs=[pl.BlockSpec((tm, D), lambda i, j: (i, 0)),
                  pl.BlockSpec((1, D), lambda i, j: (0, 0)),
                  pl.BlockSpec(memory_space=pl.ANY),
                  pos_spec, pos_spec,
                  _resident(qd.shape), _resident(kd.shape), _resident(cw.shape)] + st_specs + cast_specs_in,
        out_specs=[pl.BlockSpec((tm, 2 * COL_TILE), lambda i, j: (i, jnp.minimum(j, 1))),
                   pl.BlockSpec((tm, COL_TILE), lambda i, j: (i, b_col(j))),
                   pl.BlockSpec((tm, COL_TILE), lambda i, j: (i, f_col(j))),
                   cs_spec] + cast_specs_out,
        out_shape=[jax.ShapeDtypeStruct((T, 4 * COL_TILE), BF16),
                   jax.ShapeDtypeStruct((T, 3 * COL_TILE), BF16),
                   third_shape,
                   jax.ShapeDtypeStruct((n_batch, CONV_W - 1, D), F32)] + cast_shapes,
        scratch_shapes=conv_scratch + [pltpu.VMEM((V7X_SUBLANES, D), F32),
                                       pltpu.VMEM((W_SLOTS, D, group_width * COL_TILE), BF16),
                                       pltpu.SemaphoreType.DMA((W_SLOTS, group_width))],
        compiler_params=_params(("arbitrary", "arbitrary")),
        name="inproj_init" if has_init else "inproj",
    )(x2d, g, w, cos2, sin2, qd, kd, cw, *st_args, *cast)


N_RET_INPUTS = 5


def _retention_body(a_ref, b_ref, f_ref, dec_ref, cd_ref, s0_ref, s_ref, o_ref, *, chunk, n_chunks, seqs):
    has_init = s0_ref is not None
    fuse = chunk % DK == 0
    q_ref, qd_ref, k_ref, kd_ref = (a_ref.at[:, pl.ds(t * COL_TILE, COL_TILE)] for t in range(4))
    sg_ref = f_ref

    def v_at(rows, h):
        return b_ref[rows, pl.ds(h * DV, DV)]

    for seq in seqs:
        row_slices = [pl.ds((seq * n_chunks + c) * chunk, chunk) for c in range(n_chunks)]
        probs, kvs = {}, {}
        for c, rows in enumerate(row_slices):
            for h in range(N_HEADS):
                kcols = pl.ds(h * DK, DK)
                scores = lax.dot_general(q_ref[rows, kcols], k_ref[rows, kcols], (((1,), (1,)), ((), ())),
                                         preferred_element_type=F32) * dec_ref[h]
                probs[c, h] = scores.astype(BF16)
                kdt = jnp.transpose(kd_ref[rows, kcols].astype(F32)).astype(BF16)
                kvs[c, h] = _dot(kdt, v_at(rows, h))
        if has_init:
            states = [s0_ref[seq, h] for h in range(N_HEADS)]
        else:
            states = [s_ref[seq, h] for h in range(N_HEADS)]
        for c, rows in enumerate(row_slices):
            for h in range(N_HEADS):
                kcols = pl.ds(h * DK, DK)
                ocols = pl.ds(h * DV, DV)
                S = states[h]
                if fuse:
                    o = _dot(jnp.concatenate([probs[c, h], qd_ref[rows, kcols]], axis=1),
                             jnp.concatenate([v_at(rows, h), S.astype(BF16)], axis=0))
                else:
                    o = _dot(probs[c, h], v_at(rows, h)) + _dot(qd_ref[rows, kcols], S.astype(BF16))
                states[h] = S * cd_ref[h] + kvs[c, h]
                o = o * lax.rsqrt(jnp.mean(o * o, axis=-1, keepdims=True) + EPS)
                o_ref[rows, ocols] = (o * sg_ref[rows, ocols]).astype(BF16)
        for h in range(N_HEADS):
            s_ref[seq, h] = states[h]


def _retention_specs(a, b, f, decay, cd, rows, row_block):
    def whole(arr):
        return pl.BlockSpec((rows, arr.shape[1]), lambda *g: (row_block(*g), 0))

    specs = [whole(a), whole(b), whole(f), _resident(decay.shape), pl.BlockSpec(memory_space=pltpu.SMEM)]
    return specs, [a, b, f, decay, cd]


def _retention_mix_kernel(a_ref, b_ref, h_ref, dec_ref, cd_ref, wg0_ref, wg1_ref, wga_ref, wgb_ref,
                          wro_ref, wco_ref, wo_ref, d_ref, s_ref, o_ref, sg_ref, *, chunk, n_chunks):
    @pl.when(pl.program_id(1) == 0)
    def _():
        s_ref[...] = jnp.zeros_like(s_ref)

    h = h_ref[...]
    sg_ref[:, :COL_TILE] = _swish(_dot(h, wg0_ref[...]))
    sg_ref[:, COL_TILE:] = _swish(_dot(h, wg1_ref[...]))
    _retention_body(a_ref, b_ref, sg_ref, dec_ref, cd_ref, None, s_ref, o_ref,
                    chunk=chunk, n_chunks=n_chunks, seqs=range(1))
    cbv = b_ref[:, 2 * COL_TILE:]
    sig_a = _sigmoid(_dot(h, wga_ref[...]))
    sig_b = _sigmoid(_dot(h, wgb_ref[...]))
    y_conv = _dot(cbv, wco_ref[...])
    merged = sig_a * _dot(o_ref[...], wro_ref[...]) + sig_b * y_conv
    d_ref[...] = _dot(merged.astype(BF16), wo_ref[...])


def _retention_mix(a, b, h, w_in, decay, cd, wro, wco, wo, *, n_batch, seq_len, chunk, n_chunks):
    T, D = a.shape[0], wo.shape[1]
    rows = n_chunks * chunk
    steps = seq_len // rows

    def row_block(b_, t):
        return b_ * steps + t

    def tile(width, col):
        return pl.BlockSpec((rows, width), lambda b_, t: (row_block(b_, t), col))

    in_specs, args = _retention_specs(a, b, h, decay, cd, rows, row_block)
    in_specs += [pl.BlockSpec((D, COL_TILE), lambda b_, t, tile_=tile_: (0, tile_), pipeline_mode=pl.Buffered(1))
                 for tile_ in GATE_TILES]
    in_specs += [_resident(wro.shape), _resident(wco.shape), _resident(wo.shape)]
    args += [w_in] * len(GATE_TILES) + [wro, wco, wo]
    kernel = functools.partial(_retention_mix_kernel, chunk=chunk, n_chunks=n_chunks)
    return pl.pallas_call(
        kernel,
        grid=(n_batch, steps),
        in_specs=in_specs,
        out_specs=[tile(D, 0), pl.BlockSpec((1, N_HEADS, DK, DV), lambda b_, t: (b_, 0, 0, 0))],
        out_shape=[jax.ShapeDtypeStruct((T, D), F32),
                   jax.ShapeDtypeStruct((n_batch, N_HEADS, DK, DV), F32)],
        scratch_shapes=[pltpu.VMEM((rows, N_HEADS * DV), BF16), pltpu.VMEM((rows, N_HEADS * DV), F32)],
        compiler_params=_params(("parallel", "arbitrary")),
        name="retention_mix",
    )(*args)


def _mix_kernel(x_ref, og_ref, cbv_ref, sa_ref, sb_ref, wro_ref, wco_ref, wo_ref, x1_ref):
    y_ret = _dot(og_ref[...], wro_ref[...])
    y_conv = _dot(cbv_ref[...], wco_ref[...])
    merged = sa_ref[...] * y_ret + sb_ref[...] * y_conv
    x1_ref[...] = x_ref[...] + _dot(merged.astype(BF16), wo_ref[...])


def _mix(x2d, og, b, f, wro, wco, wo, *, tm):
    T, D = x2d.shape

    def tile(width, col):
        return pl.BlockSpec((tm, width), lambda i: (i, col))

    return pl.pallas_call(
        _mix_kernel,
        grid=(T // tm,),
        in_specs=[tile(D, 0), tile(N_HEADS * DV, 0), tile(D, 2), tile(D, 2), tile(D, 3),
                  _resident(wro.shape), _resident(wco.shape), _resident(wo.shape)],
        out_specs=tile(D, 0),
        out_shape=jax.ShapeDtypeStruct((T, D), F32),
        compiler_params=_params(("parallel",)),
        name="mix",
    )(x2d, og, b, f, f, wro, wco, wo)


N_FFN_INPUTS = 9


def _ffn_kernel(*refs, final_norm, side):
    x_ref, p_ref, gf_ref, wu_ref, wd_ref, gp_ref, wpg_ref, wpp_ref, gl_ref = refs[:N_FFN_INPUTS]
    if side is None:
        o_ref, = refs[N_FFN_INPUTS:]
        x = x_ref[...]
    else:
        ret_refs = refs[N_FFN_INPUTS:N_FFN_INPUTS + N_RET_INPUTS]
        s0_ref, base_ref, o_ref, og_ref, s_ref = refs[N_FFN_INPUTS + N_RET_INPUTS:]
        chunk, n_seq = side
        x = base_ref[...] + x_ref[...]
    hf = _rms(x, gf_ref[...]).astype(BF16)
    n_ff = wu_ref.shape[1] // COL_TILE
    acc = x
    for c in range(n_ff):
        cols = pl.ds(c * COL_TILE, COL_TILE)
        hid = jnp.square(jnp.maximum(_dot(hf, wu_ref[:, cols]), 0.0)).astype(BF16)
        acc = acc + _dot(hid, wd_ref[cols, :])
        if side is not None:
            _retention_body(*ret_refs, s0_ref, s_ref, og_ref, chunk=chunk, n_chunks=1,
                            seqs=range(c * n_seq // n_ff, (c + 1) * n_seq // n_ff))
    gate = _sigmoid(_dot(_rms(acc, gp_ref[...]).astype(BF16), wpg_ref[...]))
    y = acc + gate * _dot(p_ref[...].astype(BF16), wpp_ref[...])
    if final_norm:
        y = _rms(y, gl_ref[...])
    o_ref[...] = y


def _ffn(x2d, p2d, gf, wu, wd, gp, wpg, wpp, gl, *, tm, final_norm, side=None):
    T, D = x2d.shape
    steps = T // tm
    in_specs = [pl.BlockSpec((tm, D), lambda i: (i, 0)),
                pl.BlockSpec((tm, p2d.shape[1]), lambda i: (i, 0)),
                _resident(gf.shape), _resident(wu.shape), _resident(wd.shape),
                _resident(gp.shape), _resident(wpg.shape), _resident(wpp.shape),
                _resident(gl.shape)]
    args = [x2d, p2d, gf, wu, wd, gp, wpg, wpp, gl]
    out_specs = [pl.BlockSpec((tm, D), lambda i: (i, 0))]
    out_shape = [jax.ShapeDtypeStruct((T, D), F32)]
    kernel_side = None
    if side is not None:
        base, a, b, f, s0, decay, cd, chunk = side
        n_all = s0.shape[0]
        assert n_all % steps == 0 and a.shape[0] == n_all * chunk
        n_seq = n_all // steps
        ret_specs, ret_args = _retention_specs(a, b, f, decay, cd, n_seq * chunk, lambda i: i)
        state_spec = pl.BlockSpec((n_seq,) + s0.shape[1:], lambda i: (i, 0, 0, 0))
        in_specs += ret_specs + [state_spec, pl.BlockSpec((tm, D), lambda i: (i, 0))]
        args += ret_args + [s0, base]
        out_specs += [pl.BlockSpec((n_seq * chunk, N_HEADS * DV), lambda i: (i, 0)), state_spec]
        out_shape += [jax.ShapeDtypeStruct((a.shape[0], N_HEADS * DV), BF16),
                      jax.ShapeDtypeStruct(s0.shape, F32)]
        kernel_side = (chunk, n_seq)
    kernel = functools.partial(_ffn_kernel, final_norm=final_norm, side=kernel_side)
    out = pl.pallas_call(
        kernel,
        grid=(steps,),
        in_specs=in_specs,
        out_specs=out_specs,
        out_shape=out_shape,
        compiler_params=_params(("parallel",)),
        name="ffn" if side is None else "ffn_retention",
    )(*args)
    return out[0] if side is None else out


def _layer(xp, xs, pp, ps, s_ret, s_conv, pos_p, pos_s, w, *, n_prompt, len_prompt, chunk_prompt,
           n_sample, len_sample, final_norm):
    cos_p, sin_p, decay_p, qd_p, kd_p, cd_p = _tables(pos_p, chunk_prompt)
    cos_s, sin_s, decay_s, qd_s, kd_s, cd_s = _tables(pos_s, len_sample)
    ap, bp, hp, conv_p, w_up, w_down = _inproj(
        xp, w["g_mix"], w["w_in"], w["conv_w"], None, cos_p, sin_p, qd_p, kd_p, n_batch=n_prompt,
        seq_len=len_prompt, chunk=chunk_prompt, tm=1024, cast=(w["w_up"], w["w_down"]), gates=False)
    as_, bs, fs, conv_s, w_ret_out, w_conv_out, w_o, w_ple_gate = _inproj(
        xs, w["g_mix"], w["w_in"], w["conv_w"], s_conv, cos_s, sin_s, qd_s, kd_s, n_batch=n_sample,
        seq_len=len_sample, chunk=len_sample, tm=512, cast_chunk_bytes=2 * CAST_CHUNK_BYTES,
        cast=(w["w_ret_out"], w["w_conv_out"], w["w_o"], w["w_ple_gate"]))
    ffn_w = (w["g_ffn"], w_up, w_down, w["g_ple"], w_ple_gate, w["w_ple_proj"], w["g_final"])
    mix_w = (w_ret_out, w_conv_out, w_o)
    dxp, ret_p = _retention_mix(ap, bp, hp, w["w_in"], decay_p, cd_p, *mix_w, n_batch=n_prompt,
                                seq_len=len_prompt, chunk=chunk_prompt, n_chunks=4)
    yp, og_s, ret_s = _ffn(dxp, pp, *ffn_w, tm=512, final_norm=final_norm,
                           side=(xp, as_, bs, fs, s_ret, decay_s, cd_s, len_sample))
    x1s = _mix(xs, og_s, bs, fs, *mix_w, tm=256)
    ys = _ffn(x1s, ps, *ffn_w, tm=512, final_norm=final_norm)
    return yp, ys, ret_p, conv_p, ret_s, conv_s


def kernel(x_prompt, x_sample, state_ret, state_conv, p_prompt, p_sample, g_mix, w_in, w_ret_out,
           conv_w, w_conv_out, w_o, g_ffn, w_up, w_down, g_ple, w_ple_gate, w_ple_proj, g_final):
    depth = w_in.shape[0]
    B, L, D = x_prompt.shape
    Bs, Ls, _ = x_sample.shape
    pos_prompt = jnp.arange(L, dtype=F32)
    pos_sample = PAST_LEN + jnp.arange(Ls, dtype=F32)
    chunk_prompt = min(RET_CHUNK, L)

    hp = x_prompt.reshape(B * L, D)
    hs = x_sample.reshape(Bs * Ls, D)
    rp, cp, rs, cs = [], [], [], []
    for i in range(depth):
        w = {"g_mix": g_mix[i][None], "w_in": w_in[i].astype(BF16),
             "w_ret_out": w_ret_out[i], "conv_w": conv_w[i],
             "w_conv_out": w_conv_out[i], "w_o": w_o[i],
             "g_ffn": g_ffn[i][None], "w_up": w_up[i], "w_down": w_down[i],
             "g_ple": g_ple[i][None], "w_ple_gate": w_ple_gate[i],
             "w_ple_proj": w_ple_proj[i].astype(BF16), "g_final": g_final[None]}
        hp, hs, r_p, c_p, r_s, c_s = _layer(
            hp, hs, p_prompt[i].reshape(B * L, -1), p_sample[i].reshape(Bs * Ls, -1),
            state_ret[i], state_conv[i], pos_prompt, pos_sample, w,
            n_prompt=B, len_prompt=L, chunk_prompt=chunk_prompt, n_sample=Bs, len_sample=Ls,
            final_norm=i == depth - 1)
        rp.append(r_p)
        cp.append(c_p)
        rs.append(r_s)
        cs.append(c_s)
    return (hp.reshape(B, L, D), hs.reshape(Bs, Ls, D), jnp.stack(rp), jnp.stack(cp),
            jnp.stack(rs), jnp.stack(cs))
```

```python
import functools

import jax
import jax.numpy as jnp
from jax import lax
from jax.experimental import pallas as pl
from jax.experimental.pallas import tpu as pltpu

F32 = jnp.float32
BF16 = jnp.bfloat16

N_HEADS = 8
DK = 128
DV = 256
CONV_W = 3
RET_CHUNK = 128
PAST_LEN = 16384
ROPE_BASE = 10000.0
EPS = 1e-6

V7X_SUBLANES = 8
V7X_VMEM_BYTES = 64 * 1024 * 1024
V7X_VMEM_LIMIT_BYTES = V7X_VMEM_BYTES - 4 * 1024 * 1024

COL_TILE = 1024

IN_GROUP_TILES = ((0, 2), (1, 3), (4, 6), (10,), (5, 7), (8, 9))
IN_GROUPS = len(IN_GROUP_TILES)
IN_TILES = sum(len(tiles) for tiles in IN_GROUP_TILES)
IN_GROUP_TILES_NO_GATES = ((0, 2), (1, 3), (6, 7, 8))
GATE_TILES = (4, 5, 9, 10)
W_SLOTS = 3
CAST_CHUNK_BYTES = 512 * 1024


def _rms(x, g):
    return x * lax.rsqrt(jnp.mean(x * x, axis=-1, keepdims=True) + EPS) * g


def _sigmoid(x):
    return 0.5 * jnp.tanh(0.5 * x) + 0.5


def _swish(x):
    hx = 0.5 * x
    return hx + hx * jnp.tanh(hx)


def _dot(a, b):
    return jnp.dot(a, b, preferred_element_type=F32)


def _params(semantics):
    return pltpu.CompilerParams(dimension_semantics=semantics,
                                vmem_limit_bytes=V7X_VMEM_LIMIT_BYTES)


def _resident(shape):
    zeros = (0,) * len(shape)
    return pl.BlockSpec(shape, lambda *_: zeros, pipeline_mode=pl.Buffered(1))


def _tables(pos, chunk):
    inv_freq = ROPE_BASE ** (-jnp.arange(0, DK, 2, dtype=F32) / DK)
    ang = pos.astype(F32)[:, None] * inv_freq[None, :]
    c, s = jnp.cos(ang), jnp.sin(ang)
    cos2 = jnp.concatenate([c, c], axis=-1)
    sin2 = jnp.concatenate([-s, s], axis=-1)
    log_g = jnp.log(1.0 - 2.0 ** (-5.0 - jnp.arange(N_HEADS, dtype=F32)))
    idx = jnp.arange(chunk, dtype=F32)
    diff = idx[:, None] - idx[None, :]
    decay = jnp.where(diff >= 0, jnp.exp(jnp.maximum(diff, 0.0)[None] * log_g[:, None, None]), 0.0)
    q_decay = jnp.exp((idx + 1.0)[None, :] * log_g[:, None])
    k_decay = jnp.exp((chunk - 1.0 - idx)[None, :] * log_g[:, None])
    chunk_decay = jnp.exp(chunk * log_g)
    qd = jnp.repeat(q_decay.T, DK, axis=1)
    kd = jnp.repeat(k_decay.T, DK, axis=1)
    return cos2, sin2, decay, qd, kd, chunk_decay


def _inproj_kernel(*refs, tm, chunk, seq_len, has_init, cast_ranges, gates):
    groups = IN_GROUP_TILES if gates else IN_GROUP_TILES_NO_GATES
    n_groups = len(groups)
    refs = list(refs)
    x_ref, g_ref, w_hbm, cos_ref, sin_ref, qd_ref, kd_ref, cw_ref = refs[:8]
    del refs[:8]
    st_ref = refs.pop(0) if has_init else None
    n_cast = len(cast_ranges)
    cast_in = refs[:n_cast]
    a_ref, b_ref, f_ref, cs_ref = refs[n_cast:n_cast + 4]
    cast_out = refs[n_cast + 4:2 * n_cast + 4]
    if gates:
        h_ref, cx_ref, cb_ref, carry_ref, wbuf, wsem = refs[2 * n_cast + 4:]
    else:
        carry_ref, wbuf, wsem = refs[2 * n_cast + 4:]
        h_ref = f_ref
    i = pl.program_id(0)
    j = pl.program_id(1)
    n_pos = cos_ref.shape[0]

    step = i * n_groups + j
    n_steps = pl.num_programs(0) * n_groups

    for src, dst, (first, last) in zip(cast_in, cast_out, cast_ranges):
        @pl.when((step >= first) & (step < last))
        def _():
            dst[...] = src[...].astype(BF16)

    def w_copies(group):
        slot = group % W_SLOTS
        return [pltpu.make_async_copy(w_hbm.at[:, pl.ds(tile * COL_TILE, COL_TILE)],
                                      wbuf.at[slot, :, pl.ds(t * COL_TILE, COL_TILE)],
                                      wsem.at[slot, t])
                for t, tile in enumerate(groups[group])]

    @pl.when(step == 0)
    def _():
        for group in range(W_SLOTS - 1):
            for copy in w_copies(group):
                copy.start()

    for jj in range(n_groups):
        @pl.when((j == jj) & (step + W_SLOTS - 1 < n_steps))
        def _():
            for copy in w_copies((jj + W_SLOTS - 1) % n_groups):
                copy.start()

        @pl.when(j == jj)
        def _():
            for copy in w_copies(jj):
                copy.wait()

    def proj(jj, t, h=None):
        h = h_ref[...] if h is None else h
        return _dot(h, wbuf[jj % W_SLOTS, :, t * COL_TILE:(t + 1) * COL_TILE])

    def rotary_heads(acc, dec_ref, post_scale):
        plain, scaled = [], []
        for h in range(N_HEADS):
            cols = slice(h * DK, (h + 1) * DK)
            t = acc[:, cols]
            t3 = t.reshape(tm // n_pos, n_pos, DK)
            r3 = pltpu.roll(t, DK // 2, 1).reshape(tm // n_pos, n_pos, DK)
            r = (t3 * cos_ref[...][None] + r3 * sin_ref[...][None]).reshape(tm, DK)
            if post_scale != 1.0:
                r = r * post_scale
            plain.append(r.astype(BF16))
            d = r.reshape(tm // chunk, chunk, DK) * dec_ref[:, cols][None]
            scaled.append(d.reshape(tm, DK).astype(BF16))
        return jnp.concatenate(plain + scaled, axis=1)

    def conv(u, cb):
        def patch(pos, prev2, prev1, r1, r2):
            return (jnp.where(pos == 0, prev1, r1),
                    jnp.where(pos == 0, prev2, jnp.where(pos == 1, prev1, r2)))

        if has_init:
            n = tm // seq_len
            prev2 = jnp.broadcast_to(st_ref[:, 0:1, :], (n, seq_len, u.shape[1])).reshape(u.shape)
            prev1 = jnp.broadcast_to(st_ref[:, 1:2, :], (n, seq_len, u.shape[1])).reshape(u.shape)
            u1, u2 = patch(lax.broadcasted_iota(jnp.int32, u.shape, 0) % seq_len, prev2, prev1,
                           pltpu.roll(u, 1, 0), pltpu.roll(u, 2, 0))
        else:
            s = V7X_SUBLANES
            first = (i % (seq_len // tm)) == 0
            halo = jnp.where(first, 0.0, carry_ref[...])
            prev2 = jnp.broadcast_to(halo[s - 2:s - 1, :], halo.shape)
            prev1 = jnp.broadcast_to(halo[s - 1:s, :], halo.shape)
            r1, r2 = pltpu.roll(u, 1, 0), pltpu.roll(u, 2, 0)
            h1, h2 = patch(lax.broadcasted_iota(jnp.int32, halo.shape, 0), prev2, prev1, r1[:s], r2[:s])
            u1 = jnp.concatenate([h1, r1[s:]], axis=0)
            u2 = jnp.concatenate([h2, r2[s:]], axis=0)
        cv = cw_ref[0:1, :] * u2 + cw_ref[1:2, :] * u1 + cw_ref[2:3, :] * u
        b_ref[...] = (cb * cv).astype(BF16)
        if has_init:
            cs_ref[...] = u.reshape(tm // seq_len, seq_len, u.shape[1])[:, seq_len - (CONV_W - 1):, :]
        else:
            carry_ref[...] = u[tm - V7X_SUBLANES:, :]
            cs_ref[0] = u[tm - (CONV_W - 1):, :]

    @pl.when(j == 0)
    def _():
        h = _rms(x_ref[...], g_ref[...]).astype(BF16)
        h_ref[...] = h
        a_ref[...] = rotary_heads(proj(0, 0, h), qd_ref, DK ** -0.5)
        b_ref[...] = proj(0, 1, h).astype(BF16)

    @pl.when(j == 1)
    def _():
        a_ref[...] = rotary_heads(proj(1, 0), kd_ref, 1.0)
        b_ref[...] = proj(1, 1).astype(BF16)

    if gates:
        @pl.when(j == 2)
        def _():
            f_ref[...] = _swish(proj(2, 0))
            cx_ref[...] = proj(2, 1)

        @pl.when(j == 3)
        def _():
            f_ref[...] = _sigmoid(proj(3, 0))

        @pl.when(j == 4)
        def _():
            f_ref[...] = _swish(proj(4, 0))
            cb_ref[...] = proj(4, 1)

        @pl.when(j == 5)
        def _():
            f_ref[...] = _sigmoid(proj(5, 1))
            conv(proj(5, 0) * cx_ref[...], cb_ref[...])
    else:
        @pl.when(j == 2)
        def _():
            cx, cb, cc = proj(2, 0), proj(2, 1), proj(2, 2)
            conv(cc * cx, cb)


def _inproj(x2d, g, w, cw, conv_state, cos2, sin2, qd, kd, *, n_batch, seq_len, chunk, tm, cast=(),
            cast_chunk_bytes=CAST_CHUNK_BYTES, gates=True):
    T, D = x2d.shape
    has_init = conv_state is not None
    groups = IN_GROUP_TILES if gates else IN_GROUP_TILES_NO_GATES
    n_groups = len(groups)
    group_width = max(len(tiles) for tiles in groups)
    assert w.shape[1] == IN_TILES * COL_TILE and D == COL_TILE and n_groups % W_SLOTS == 0
    if has_init:
        assert tm % seq_len == 0
        n = tm // seq_len
        pos_spec = pl.BlockSpec((seq_len, DK), lambda i, j: (0, 0))
        st_specs = [pl.BlockSpec((n, CONV_W - 1, D), lambda i, j: (i, 0, 0))]
        st_args = [conv_state]
        cs_spec = pl.BlockSpec((n, CONV_W - 1, D), lambda i, j: (i, 0, 0))
    else:
        assert seq_len % tm == 0 and tm % chunk == 0
        tiles = seq_len // tm
        pos_spec = pl.BlockSpec((tm, DK), lambda i, j: (i % tiles, 0))
        st_specs, st_args = [], []
        cs_spec = pl.BlockSpec((1, CONV_W - 1, D), lambda i, j: (i // tiles, 0, 0))

    if gates:
        def b_col(j):
            return jnp.where(j == 0, 0, jnp.where(j < 5, 1, 2))

        def f_col(j):
            return jnp.where(j <= 2, 0, jnp.where(j == 3, 3, j - 3))

        third_shape = jax.ShapeDtypeStruct((T, 4 * COL_TILE), F32)
        conv_scratch = [pltpu.VMEM((tm, D), BF16), pltpu.VMEM((tm, D), F32), pltpu.VMEM((tm, D), F32)]
    else:
        def b_col(j):
            return j

        def f_col(j):
            return 0

        third_shape = jax.ShapeDtypeStruct((T, D), BF16)
        conv_scratch = []

    n_steps = (T // tm) * n_groups
    cast_specs_in, cast_specs_out, cast_shapes, cast_ranges = [], [], [], []
    first = 0
    for m in cast:
        rows = cast_chunk_bytes // (m.shape[1] * m.dtype.itemsize)
        n_chunks = m.shape[0] // rows
        assert m.shape[0] % rows == 0 and n_chunks <= n_steps
        if first + n_chunks > n_steps:
            first = 0

        def chunk_map(i, j, first=first, n_chunks=n_chunks):
            return jnp.clip(i * n_groups + j - first, 0, n_chunks - 1), 0

        cast_specs_in.append(pl.BlockSpec((rows, m.shape[1]), chunk_map))
        cast_specs_out.append(pl.BlockSpec((rows, m.shape[1]), chunk_map))
        cast_shapes.append(jax.ShapeDtypeStruct(m.shape, BF16))
        cast_ranges.append((first, first + n_chunks))
        first += n_chunks

    kernel = functools.partial(_inproj_kernel, tm=tm, chunk=chunk, seq_len=seq_len, has_init=has_init,
                               cast_ranges=tuple(cast_ranges), gates=gates)
    return pl.pallas_call(
        kernel,
        grid=(T // tm, n_groups),
        in_specs=[pl.BlockSpec((tm, D), lambda i, j: (i, 0)),
                  pl.BlockSpec((1, D), lambda i, j: (0, 0)),
                  pl.BlockSpec(memory_space=pl.ANY),
                  pos_spec, pos_spec,
                  _resident(qd.shape), _resident(kd.shape), _resident(cw.shape)] + st_specs + cast_specs_in,
        out_specs=[pl.BlockSpec((tm, 2 * COL_TILE), lambda i, j: (i, jnp.minimum(j, 1))),
                   pl.BlockSpec((tm, COL_TILE), lambda i, j: (i, b_col(j))),
                   pl.BlockSpec((tm, COL_TILE), lambda i, j: (i, f_col(j))),
                   cs_spec] + cast_specs_out,
        out_shape=[jax.ShapeDtypeStruct((T, 4 * COL_TILE), BF16),
                   jax.ShapeDtypeStruct((T, 3 * COL_TILE), BF16),
                   third_shape,
                   jax.ShapeDtypeStruct((n_batch, CONV_W - 1, D), F32)] + cast_shapes,
        scratch_shapes=conv_scratch + [pltpu.VMEM((V7X_SUBLANES, D), F32),
                                       pltpu.VMEM((W_SLOTS, D, group_width * COL_TILE), BF16),
                                       pltpu.SemaphoreType.DMA((W_SLOTS, group_width))],
        compiler_params=_params(("arbitrary", "arbitrary")),
        name="inproj_init" if has_init else "inproj",
    )(x2d, g, w, cos2, sin2, qd, kd, cw, *st_args, *cast)


N_RET_INPUTS = 5


def _retention_body(a_ref, b_ref, f_ref, dec_ref, cd_ref, s0_ref, s_ref, o_ref, *, chunk, n_chunks, seqs):
    has_init = s0_ref is not None
    fuse = chunk % DK == 0
    q_ref, qd_ref, k_ref, kd_ref = (a_ref.at[:, pl.ds(t * COL_TILE, COL_TILE)] for t in range(4))
    sg_ref = f_ref

    def v_at(rows, h):
        return b_ref[rows, pl.ds(h * DV, DV)]

    for seq in seqs:
        row_slices = [pl.ds((seq * n_chunks + c) * chunk, chunk) for c in range(n_chunks)]
        probs, kvs = {}, {}
        for c, rows in enumerate(row_slices):
            for h in range(N_HEADS):
                kcols = pl.ds(h * DK, DK)
                scores = lax.dot_general(q_ref[rows, kcols], k_ref[rows, kcols], (((1,), (1,)), ((), ())),
                                         preferred_element_type=F32) * dec_ref[h]
                probs[c, h] = scores.astype(BF16)
                kdt = jnp.transpose(kd_ref[rows, kcols].astype(F32)).astype(BF16)
                kvs[c, h] = _dot(kdt, v_at(rows, h))
        if has_init:
            states = [s0_ref[seq, h] for h in range(N_HEADS)]
        else:
            states = [s_ref[seq, h] for h in range(N_HEADS)]
        for c, rows in enumerate(row_slices):
            for h in range(N_HEADS):
                kcols = pl.ds(h * DK, DK)
                ocols = pl.ds(h * DV, DV)
                S = states[h]
                if fuse:
                    o = _dot(jnp.concatenate([probs[c, h], qd_ref[rows, kcols]], axis=1),
                             jnp.concatenate([v_at(rows, h), S.astype(BF16)], axis=0))
                else:
                    o = _dot(probs[c, h], v_at(rows, h)) + _dot(qd_ref[rows, kcols], S.astype(BF16))
                states[h] = S * cd_ref[h] + kvs[c, h]
                o = o * lax.rsqrt(jnp.mean(o * o, axis=-1, keepdims=True) + EPS)
                o_ref[rows, ocols] = (o * sg_ref[rows, ocols]).astype(BF16)
        for h in range(N_HEADS):
            s_ref[seq, h] = states[h]


def _retention_specs(a, b, f, decay, cd, rows, row_block):
    def whole(arr):
        return pl.BlockSpec((rows, arr.shape[1]), lambda *g: (row_block(*g), 0))

    specs = [whole(a), whole(b), whole(f), _resident(decay.shape), pl.BlockSpec(memory_space=pltpu.SMEM)]
    return specs, [a, b, f, decay, cd]


def _retention_mix_kernel(a_ref, b_ref, h_ref, dec_ref, cd_ref, wg0_ref, wg1_ref, wga_ref, wgb_ref,
                          wro_ref, wco_ref, wo_ref, d_ref, s_ref, o_ref, sg_ref, *, chunk, n_chunks):
    @pl.when(pl.program_id(1) == 0)
    def _():
        s_ref[...] = jnp.zeros_like(s_ref)

    h = h_ref[...]
    sg_ref[:, :COL_TILE] = _swish(_dot(h, wg0_ref[...]))
    sg_ref[:, COL_TILE:] = _swish(_dot(h, wg1_ref[...]))
    _retention_body(a_ref, b_ref, sg_ref, dec_ref, cd_ref, None, s_ref, o_ref,
                    chunk=chunk, n_chunks=n_chunks, seqs=range(1))
    cbv = b_ref[:, 2 * COL_TILE:]
    sig_a = _sigmoid(_dot(h, wga_ref[...]))
    sig_b = _sigmoid(_dot(h, wgb_ref[...]))
    y_conv = _dot(cbv, wco_ref[...])
    merged = sig_a * _dot(o_ref[...], wro_ref[...]) + sig_b * y_conv
    d_ref[...] = _dot(merged.astype(BF16), wo_ref[...])


def _retention_mix(a, b, h, w_in, decay, cd, wro, wco, wo, *, n_batch, seq_len, chunk, n_chunks):
    T, D = a.shape[0], wo.shape[1]
    rows = n_chunks * chunk
    steps = seq_len // rows

    def row_block(b_, t):
        return b_ * steps + t

    def tile(width, col):
        return pl.BlockSpec((rows, width), lambda b_, t: (row_block(b_, t), col))

    in_specs, args = _retention_specs(a, b, h, decay, cd, rows, row_block)
    in_specs += [pl.BlockSpec((D, COL_TILE), lambda b_, t, tile_=tile_: (0, tile_), pipeline_mode=pl.Buffered(1))
                 for tile_ in GATE_TILES]
    in_specs += [_resident(wro.shape), _resident(wco.shape), _resident(wo.shape)]
    args += [w_in] * len(GATE_TILES) + [wro, wco, wo]
    kernel = functools.partial(_retention_mix_kernel, chunk=chunk, n_chunks=n_chunks)
    return pl.pallas_call(
        kernel,
        grid=(n_batch, steps),
        in_specs=in_specs,
        out_specs=[tile(D, 0), pl.BlockSpec((1, N_HEADS, DK, DV), lambda b_, t: (b_, 0, 0, 0))],
        out_shape=[jax.ShapeDtypeStruct((T, D), F32),
                   jax.ShapeDtypeStruct((n_batch, N_HEADS, DK, DV), F32)],
        scratch_shapes=[pltpu.VMEM((rows, N_HEADS * DV), BF16), pltpu.VMEM((rows, N_HEADS * DV), F32)],
        compiler_params=_params(("parallel", "arbitrary")),
        name="retention_mix",
    )(*args)


N_FFN_INPUTS = 9


def _ffn_kernel(*refs, final_norm, side, mix):
    x_ref, p_ref, gf_ref, wu_ref, wd_ref, gp_ref, wpg_ref, wpp_ref, gl_ref = refs[:N_FFN_INPUTS]
    if mix:
        og_ref, cbv_ref, sa_ref, sb_ref, wro_ref, wco_ref, wo_ref, o_ref = refs[N_FFN_INPUTS:]
        merged = sa_ref[...] * _dot(og_ref[...], wro_ref[...]) + sb_ref[...] * _dot(cbv_ref[...], wco_ref[...])
        x = x_ref[...] + _dot(merged.astype(BF16), wo_ref[...])
    elif side is None:
        o_ref, = refs[N_FFN_INPUTS:]
        x = x_ref[...]
    else:
        ret_refs = refs[N_FFN_INPUTS:N_FFN_INPUTS + N_RET_INPUTS]
        s0_ref, base_ref, o_ref, og_ref, s_ref = refs[N_FFN_INPUTS + N_RET_INPUTS:]
        chunk, n_seq = side
        x = base_ref[...] + x_ref[...]
    hf = _rms(x, gf_ref[...]).astype(BF16)
    n_ff = wu_ref.shape[1] // COL_TILE
    acc = x
    for c in range(n_ff):
        cols = pl.ds(c * COL_TILE, COL_TILE)
        hid = jnp.square(jnp.maximum(_dot(hf, wu_ref[:, cols]), 0.0)).astype(BF16)
        acc = acc + _dot(hid, wd_ref[cols, :])
        if side is not None:
            _retention_body(*ret_refs, s0_ref, s_ref, og_ref, chunk=chunk, n_chunks=1,
                            seqs=range(c * n_seq // n_ff, (c + 1) * n_seq // n_ff))
    gate = _sigmoid(_dot(_rms(acc, gp_ref[...]).astype(BF16), wpg_ref[...]))
    y = acc + gate * _dot(p_ref[...].astype(BF16), wpp_ref[...])
    if final_norm:
        y = _rms(y, gl_ref[...])
    o_ref[...] = y


def _ffn(x2d, p2d, gf, wu, wd, gp, wpg, wpp, gl, *, tm, final_norm, side=None, mix=None):
    T, D = x2d.shape
    steps = T // tm
    in_specs = [pl.BlockSpec((tm, D), lambda i: (i, 0)),
                pl.BlockSpec((tm, p2d.shape[1]), lambda i: (i, 0)),
                _resident(gf.shape), _resident(wu.shape), _resident(wd.shape),
                _resident(gp.shape), _resident(wpg.shape), _resident(wpp.shape),
                _resident(gl.shape)]
    args = [x2d, p2d, gf, wu, wd, gp, wpg, wpp, gl]
    out_specs = [pl.BlockSpec((tm, D), lambda i: (i, 0))]
    out_shape = [jax.ShapeDtypeStruct((T, D), F32)]
    kernel_side = None
    if side is not None:
        base, a, b, f, s0, decay, cd, chunk = side
        n_all = s0.shape[0]
        assert n_all % steps == 0 and a.shape[0] == n_all * chunk
        n_seq = n_all // steps
        ret_specs, ret_args = _retention_specs(a, b, f, decay, cd, n_seq * chunk, lambda i: i)
        state_spec = pl.BlockSpec((n_seq,) + s0.shape[1:], lambda i: (i, 0, 0, 0))
        in_specs += ret_specs + [state_spec, pl.BlockSpec((tm, D), lambda i: (i, 0))]
        args += ret_args + [s0, base]
        out_specs += [pl.BlockSpec((n_seq * chunk, N_HEADS * DV), lambda i: (i, 0)), state_spec]
        out_shape += [jax.ShapeDtypeStruct((a.shape[0], N_HEADS * DV), BF16),
                      jax.ShapeDtypeStruct(s0.shape, F32)]
        kernel_side = (chunk, n_seq)
    if mix is not None:
        assert side is None
        og, b, f, wro, wco, wo = mix
        in_specs += [pl.BlockSpec((tm, N_HEADS * DV), lambda i: (i, 0)),
                     pl.BlockSpec((tm, D), lambda i: (i, 2)), pl.BlockSpec((tm, D), lambda i: (i, 2)),
                     pl.BlockSpec((tm, D), lambda i: (i, 3)),
                     _resident(wro.shape), _resident(wco.shape), _resident(wo.shape)]
        args += [og, b, f, f, wro, wco, wo]
    kernel = functools.partial(_ffn_kernel, final_norm=final_norm, side=kernel_side, mix=mix is not None)
    out = pl.pallas_call(
        kernel,
        grid=(steps,),
        in_specs=in_specs,
        out_specs=out_specs,
        out_shape=out_shape,
        compiler_params=_params(("parallel",)),
        name="ffn_retention" if side is not None else ("mix_ffn" if mix is not None else "ffn"),
    )(*args)
    return out[0] if side is None else out


def _layer(xp, xs, pp, ps, s_ret, s_conv, pos_p, pos_s, w, *, n_prompt, len_prompt, chunk_prompt,
           n_sample, len_sample, final_norm):
    cos_p, sin_p, decay_p, qd_p, kd_p, cd_p = _tables(pos_p, chunk_prompt)
    cos_s, sin_s, decay_s, qd_s, kd_s, cd_s = _tables(pos_s, len_sample)
    ap, bp, hp, conv_p, w_up, w_down = _inproj(
        xp, w["g_mix"], w["w_in"], w["conv_w"], None, cos_p, sin_p, qd_p, kd_p, n_batch=n_prompt,
        seq_len=len_prompt, chunk=chunk_prompt, tm=1024, cast=(w["w_up"], w["w_down"]), gates=False)
    as_, bs, fs, conv_s, w_ret_out, w_conv_out, w_o, w_ple_gate = _inproj(
        xs, w["g_mix"], w["w_in"], w["conv_w"], s_conv, cos_s, sin_s, qd_s, kd_s, n_batch=n_sample,
        seq_len=len_sample, chunk=len_sample, tm=512, cast_chunk_bytes=2 * CAST_CHUNK_BYTES,
        cast=(w["w_ret_out"], w["w_conv_out"], w["w_o"], w["w_ple_gate"]))
    ffn_w = (w["g_ffn"], w_up, w_down, w["g_ple"], w_ple_gate, w["w_ple_proj"], w["g_final"])
    mix_w = (w_ret_out, w_conv_out, w_o)
    dxp, ret_p = _retention_mix(ap, bp, hp, w["w_in"], decay_p, cd_p, *mix_w, n_batch=n_prompt,
                                seq_len=len_prompt, chunk=chunk_prompt, n_chunks=4)
    yp, og_s, ret_s = _ffn(dxp, pp, *ffn_w, tm=512, final_norm=final_norm,
                           side=(xp, as_, bs, fs, s_ret, decay_s, cd_s, len_sample))
    ys = _ffn(xs, ps, *ffn_w, tm=512, final_norm=final_norm, mix=(og_s, bs, fs, *mix_w))
    return yp, ys, ret_p, conv_p, ret_s, conv_s


def kernel(x_prompt, x_sample, state_ret, state_conv, p_prompt, p_sample, g_mix, w_in, w_ret_out,
           conv_w, w_conv_out, w_o, g_ffn, w_up, w_down, g_ple, w_ple_gate, w_ple_proj, g_final):
    depth = w_in.shape[0]
    B, L, D = x_prompt.shape
    Bs, Ls, _ = x_sample.shape
    pos_prompt = jnp.arange(L, dtype=F32)
    pos_sample = PAST_LEN + jnp.arange(Ls, dtype=F32)
    chunk_prompt = min(RET_CHUNK, L)

    hp = x_prompt.reshape(B * L, D)
    hs = x_sample.reshape(Bs * Ls, D)
    rp, cp, rs, cs = [], [], [], []
    for i in range(depth):
        w = {"g_mix": g_mix[i][None], "w_in": w_in[i].astype(BF16),
             "w_ret_out": w_ret_out[i], "conv_w": conv_w[i],
             "w_conv_out": w_conv_out[i], "w_o": w_o[i],
             "g_ffn": g_ffn[i][None], "w_up": w_up[i], "w_down": w_down[i],
             "g_ple": g_ple[i][None], "w_ple_gate": w_ple_gate[i],
             "w_ple_proj": w_ple_proj[i].astype(BF16), "g_final": g_final[None]}
        hp, hs, r_p, c_p, r_s, c_s = _layer(
            hp, hs, p_prompt[i].reshape(B * L, -1), p_sample[i].reshape(Bs * Ls, -1),
            state_ret[i], state_conv[i], pos_prompt, pos_sample, w,
            n_prompt=B, len_prompt=L, chunk_prompt=chunk_prompt, n_sample=Bs, len_sample=Ls,
            final_norm=i == depth - 1)
        rp.append(r_p)
        cp.append(c_p)
        rs.append(r_s)
        cs.append(c_s)
    return (hp.reshape(B, L, D), hs.reshape(Bs, Ls, D), jnp.stack(rp), jnp.stack(cp),
            jnp.stack(rs), jnp.stack(cs))
```

```python
import functools

import jax
import jax.numpy as jnp
from jax import lax
from jax.experimental import pallas as pl
from jax.experimental.pallas import tpu as pltpu

F32 = jnp.float32
BF16 = jnp.bfloat16

N_HEADS = 8
DK = 128
DV = 256
CONV_W = 3
RET_CHUNK = 128
PAST_LEN = 16384
ROPE_BASE = 10000.0
EPS = 1e-6

V7X_SUBLANES = 8
V7X_VMEM_BYTES = 64 * 1024 * 1024
V7X_VMEM_LIMIT_BYTES = V7X_VMEM_BYTES - 4 * 1024 * 1024

COL_TILE = 1024

IN_GROUP_TILES = ((0, 2), (1, 3), (4, 6), (10,), (5, 7), (8, 9))
IN_GROUPS = len(IN_GROUP_TILES)
IN_TILES = sum(len(tiles) for tiles in IN_GROUP_TILES)
IN_GROUP_TILES_NO_GATES = ((0, 2), (1, 3), (6, 7, 8))
GATE_TILES = (4, 5, 9, 10)
W_SLOTS = 3
CAST_CHUNK_BYTES = 512 * 1024


def _rms(x, g):
    return x * lax.rsqrt(jnp.mean(x * x, axis=-1, keepdims=True) + EPS) * g


def _sigmoid(x):
    return 0.5 * jnp.tanh(0.5 * x) + 0.5


def _swish(x):
    hx = 0.5 * x
    return hx + hx * jnp.tanh(hx)


def _dot(a, b):
    return jnp.dot(a, b, preferred_element_type=F32)


def _params(semantics):
    return pltpu.CompilerParams(dimension_semantics=semantics,
                                vmem_limit_bytes=V7X_VMEM_LIMIT_BYTES)


def _resident(shape):
    zeros = (0,) * len(shape)
    return pl.BlockSpec(shape, lambda *_: zeros, pipeline_mode=pl.Buffered(1))


def _tables(pos, chunk):
    inv_freq = ROPE_BASE ** (-jnp.arange(0, DK, 2, dtype=F32) / DK)
    ang = pos.astype(F32)[:, None] * inv_freq[None, :]
    c, s = jnp.cos(ang), jnp.sin(ang)
    cos2 = jnp.concatenate([c, c], axis=-1)
    sin2 = jnp.concatenate([-s, s], axis=-1)
    log_g = jnp.log(1.0 - 2.0 ** (-5.0 - jnp.arange(N_HEADS, dtype=F32)))
    idx = jnp.arange(chunk, dtype=F32)
    diff = idx[:, None] - idx[None, :]
    decay = jnp.where(diff >= 0, jnp.exp(jnp.maximum(diff, 0.0)[None] * log_g[:, None, None]), 0.0)
    q_decay = jnp.exp((idx + 1.0)[None, :] * log_g[:, None])
    k_decay = jnp.exp((chunk - 1.0 - idx)[None, :] * log_g[:, None])
    chunk_decay = jnp.exp(chunk * log_g)
    qd = jnp.repeat(q_decay.T, DK, axis=1)
    kd = jnp.repeat(k_decay.T, DK, axis=1)
    return cos2, sin2, decay, qd, kd, chunk_decay


def _inproj_kernel(*refs, tm, chunk, seq_len, has_init, cast_ranges, gates):
    groups = IN_GROUP_TILES if gates else IN_GROUP_TILES_NO_GATES
    n_groups = len(groups)
    refs = list(refs)
    x_ref, g_ref, w_hbm, cos_ref, sin_ref, qd_ref, kd_ref, cw_ref = refs[:8]
    del refs[:8]
    st_ref = refs.pop(0) if has_init else None
    n_cast = len(cast_ranges)
    cast_in = refs[:n_cast]
    a_ref, b_ref, f_ref, cs_ref = refs[n_cast:n_cast + 4]
    cast_out = refs[n_cast + 4:2 * n_cast + 4]
    if gates:
        h_ref, cx_ref, cb_ref, carry_ref, wbuf, wsem = refs[2 * n_cast + 4:]
    else:
        carry_ref, wbuf, wsem = refs[2 * n_cast + 4:]
        h_ref = f_ref
    i = pl.program_id(0)
    j = pl.program_id(1)
    n_pos = cos_ref.shape[0]

    step = i * n_groups + j
    n_steps = pl.num_programs(0) * n_groups

    for src, dst, (first, last) in zip(cast_in, cast_out, cast_ranges):
        @pl.when((step >= first) & (step < last))
        def _():
            dst[...] = src[...].astype(BF16)

    def w_copies(group):
        slot = group % W_SLOTS
        return [pltpu.make_async_copy(w_hbm.at[:, pl.ds(tile * COL_TILE, COL_TILE)],
                                      wbuf.at[slot, :, pl.ds(t * COL_TILE, COL_TILE)],
                                      wsem.at[slot, t])
                for t, tile in enumerate(groups[group])]

    @pl.when(step == 0)
    def _():
        for group in range(W_SLOTS - 1):
            for copy in w_copies(group):
                copy.start()

    for jj in range(n_groups):
        @pl.when((j == jj) & (step + W_SLOTS - 1 < n_steps))
        def _():
            for copy in w_copies((jj + W_SLOTS - 1) % n_groups):
                copy.start()

        @pl.when(j == jj)
        def _():
            for copy in w_copies(jj):
                copy.wait()

    def proj(jj, t, h=None):
        h = h_ref[...] if h is None else h
        return _dot(h, wbuf[jj % W_SLOTS, :, t * COL_TILE:(t + 1) * COL_TILE])

    def rotary_heads(acc, dec_ref, post_scale):
        plain, scaled = [], []
        for h in range(N_HEADS):
            cols = slice(h * DK, (h + 1) * DK)
            t = acc[:, cols]
            t3 = t.reshape(tm // n_pos, n_pos, DK)
            r3 = pltpu.roll(t, DK // 2, 1).reshape(tm // n_pos, n_pos, DK)
            r = (t3 * cos_ref[...][None] + r3 * sin_ref[...][None]).reshape(tm, DK)
            if post_scale != 1.0:
                r = r * post_scale
            plain.append(r.astype(BF16))
            d = r.reshape(tm // chunk, chunk, DK) * dec_ref[:, cols][None]
            scaled.append(d.reshape(tm, DK).astype(BF16))
        return jnp.concatenate(plain + scaled, axis=1)

    def conv(u, cb):
        def patch(pos, prev2, prev1, r1, r2):
            return (jnp.where(pos == 0, prev1, r1),
                    jnp.where(pos == 0, prev2, jnp.where(pos == 1, prev1, r2)))

        if has_init:
            n = tm // seq_len
            prev2 = jnp.broadcast_to(st_ref[:, 0:1, :], (n, seq_len, u.shape[1])).reshape(u.shape)
            prev1 = jnp.broadcast_to(st_ref[:, 1:2, :], (n, seq_len, u.shape[1])).reshape(u.shape)
            u1, u2 = patch(lax.broadcasted_iota(jnp.int32, u.shape, 0) % seq_len, prev2, prev1,
                           pltpu.roll(u, 1, 0), pltpu.roll(u, 2, 0))
        else:
            s = V7X_SUBLANES
            first = (i % (seq_len // tm)) == 0
            halo = jnp.where(first, 0.0, carry_ref[...])
            prev2 = jnp.broadcast_to(halo[s - 2:s - 1, :], halo.shape)
            prev1 = jnp.broadcast_to(halo[s - 1:s, :], halo.shape)
            r1, r2 = pltpu.roll(u, 1, 0), pltpu.roll(u, 2, 0)
            h1, h2 = patch(lax.broadcasted_iota(jnp.int32, halo.shape, 0), prev2, prev1, r1[:s], r2[:s])
            u1 = jnp.concatenate([h1, r1[s:]], axis=0)
            u2 = jnp.concatenate([h2, r2[s:]], axis=0)
        cv = cw_ref[0:1, :] * u2 + cw_ref[1:2, :] * u1 + cw_ref[2:3, :] * u
        b_ref[...] = (cb * cv).astype(BF16)
        if has_init:
            cs_ref[...] = u.reshape(tm // seq_len, seq_len, u.shape[1])[:, seq_len - (CONV_W - 1):, :]
        else:
            carry_ref[...] = u[tm - V7X_SUBLANES:, :]
            cs_ref[0] = u[tm - (CONV_W - 1):, :]

    @pl.when(j == 0)
    def _():
        h = _rms(x_ref[...], g_ref[...]).astype(BF16)
        h_ref[...] = h
        a_ref[...] = rotary_heads(proj(0, 0, h), qd_ref, DK ** -0.5)
        b_ref[...] = proj(0, 1, h).astype(BF16)

    @pl.when(j == 1)
    def _():
        a_ref[...] = rotary_heads(proj(1, 0), kd_ref, 1.0)
        b_ref[...] = proj(1, 1).astype(BF16)

    if gates:
        @pl.when(j == 2)
        def _():
            f_ref[...] = _swish(proj(2, 0))
            cx_ref[...] = proj(2, 1)

        @pl.when(j == 3)
        def _():
            f_ref[...] = _sigmoid(proj(3, 0))

        @pl.when(j == 4)
        def _():
            f_ref[...] = _swish(proj(4, 0))
            cb_ref[...] = proj(4, 1)

        @pl.when(j == 5)
        def _():
            f_ref[...] = _sigmoid(proj(5, 1))
            conv(proj(5, 0) * cx_ref[...], cb_ref[...])
    else:
        @pl.when(j == 2)
        def _():
            cx, cb, cc = proj(2, 0), proj(2, 1), proj(2, 2)
            conv(cc * cx, cb)


def _inproj(x2d, g, w, cw, conv_state, cos2, sin2, qd, kd, *, n_batch, seq_len, chunk, tm, cast=(),
            cast_chunk_bytes=CAST_CHUNK_BYTES, gates=True):
    T, D = x2d.shape
    has_init = conv_state is not None
    groups = IN_GROUP_TILES if gates else IN_GROUP_TILES_NO_GATES
    n_groups = len(groups)
    group_width = max(len(tiles) for tiles in groups)
    assert w.shape[1] == IN_TILES * COL_TILE and D == COL_TILE and n_groups % W_SLOTS == 0
    if has_init:
        assert tm % seq_len == 0
        n = tm // seq_len
        pos_spec = pl.BlockSpec((seq_len, DK), lambda i, j: (0, 0))
        st_specs = [pl.BlockSpec((n, CONV_W - 1, D), lambda i, j: (i, 0, 0))]
        st_args = [conv_state]
        cs_spec = pl.BlockSpec((n, CONV_W - 1, D), lambda i, j: (i, 0, 0))
    else:
        assert seq_len % tm == 0 and tm % chunk == 0
        tiles = seq_len // tm
        pos_spec = pl.BlockSpec((tm, DK), lambda i, j: (i % tiles, 0))
        st_specs, st_args = [], []
        cs_spec = pl.BlockSpec((1, CONV_W - 1, D), lambda i, j: (i // tiles, 0, 0))

    if gates:
        def b_col(j):
            return jnp.where(j == 0, 0, jnp.where(j < 5, 1, 2))

        def f_col(j):
            return jnp.where(j <= 2, 0, jnp.where(j == 3, 3, j - 3))

        third_shape = jax.ShapeDtypeStruct((T, 4 * COL_TILE), F32)
        conv_scratch = [pltpu.VMEM((tm, D), BF16), pltpu.VMEM((tm, D), F32), pltpu.VMEM((tm, D), F32)]
    else:
        def b_col(j):
            return j

        def f_col(j):
            return 0

        third_shape = jax.ShapeDtypeStruct((T, D), BF16)
        conv_scratch = []

    n_steps = (T // tm) * n_groups
    cast_specs_in, cast_specs_out, cast_shapes, cast_ranges = [], [], [], []
    first = 0
    for m in cast:
        rows = cast_chunk_bytes // (m.shape[1] * m.dtype.itemsize)
        n_chunks = m.shape[0] // rows
        assert m.shape[0] % rows == 0 and n_chunks <= n_steps
        if first + n_chunks > n_steps:
            first = 0

        def chunk_map(i, j, first=first, n_chunks=n_chunks):
            return jnp.clip(i * n_groups + j - first, 0, n_chunks - 1), 0

        cast_specs_in.append(pl.BlockSpec((rows, m.shape[1]), chunk_map))
        cast_specs_out.append(pl.BlockSpec((rows, m.shape[1]), chunk_map))
        cast_shapes.append(jax.ShapeDtypeStruct(m.shape, BF16))
        cast_ranges.append((first, first + n_chunks))
        first += n_chunks

    kernel = functools.partial(_inproj_kernel, tm=tm, chunk=chunk, seq_len=seq_len, has_init=has_init,
                               cast_ranges=tuple(cast_ranges), gates=gates)
    return pl.pallas_call(
        kernel,
        grid=(T // tm, n_groups),
        in_specs=[pl.BlockSpec((tm, D), lambda i, j: (i, 0)),
                  pl.BlockSpec((1, D), lambda i, j: (0, 0)),
                  pl.BlockSpec(memory_space=pl.ANY),
                  pos_spec, pos_spec,
                  _resident(qd.shape), _resident(kd.shape), _resident(cw.shape)] + st_specs + cast_specs_in,
        out_specs=[pl.BlockSpec((tm, 2 * COL_TILE), lambda i, j: (i, jnp.minimum(j, 1))),
                   pl.BlockSpec((tm, COL_TILE), lambda i, j: (i, b_col(j))),
                   pl.BlockSpec((tm, COL_TILE), lambda i, j: (i, f_col(j))),
                   cs_spec] + cast_specs_out,
        out_shape=[jax.ShapeDtypeStruct((T, 4 * COL_TILE), BF16),
                   jax.ShapeDtypeStruct((T, 3 * COL_TILE), BF16),
                   third_shape,
                   jax.ShapeDtypeStruct((n_batch, CONV_W - 1, D), F32)] + cast_shapes,
        scratch_shapes=conv_scratch + [pltpu.VMEM((V7X_SUBLANES, D), F32),
                                       pltpu.VMEM((W_SLOTS, D, group_width * COL_TILE), BF16),
                                       pltpu.SemaphoreType.DMA((W_SLOTS, group_width))],
        compiler_params=_params(("arbitrary", "arbitrary")),
        name="inproj_init" if has_init else "inproj",
    )(x2d, g, w, cos2, sin2, qd, kd, cw, *st_args, *cast)


N_RET_INPUTS = 5


def _retention_body(a_ref, b_ref, f_ref, dec_ref, cd_ref, s0_ref, s_ref, o_ref, *, chunk, n_chunks, seqs):
    has_init = s0_ref is not None
    fuse = chunk % DK == 0
    q_ref, qd_ref, k_ref, kd_ref = (a_ref.at[:, pl.ds(t * COL_TILE, COL_TILE)] for t in range(4))
    sg_ref = f_ref

    def v_at(rows, h):
        return b_ref[rows, pl.ds(h * DV, DV)]

    for seq in seqs:
        row_slices = [pl.ds((seq * n_chunks + c) * chunk, chunk) for c in range(n_chunks)]
        probs, kvs = {}, {}
        for c, rows in enumerate(row_slices):
            for h in range(N_HEADS):
                kcols = pl.ds(h * DK, DK)
                scores = lax.dot_general(q_ref[rows, kcols], k_ref[rows, kcols], (((1,), (1,)), ((), ())),
                                         preferred_element_type=F32) * dec_ref[h]
                probs[c, h] = scores.astype(BF16)
                kdt = jnp.transpose(kd_ref[rows, kcols].astype(F32)).astype(BF16)
                kvs[c, h] = _dot(kdt, v_at(rows, h))
        if has_init:
            states = [s0_ref[seq, h] for h in range(N_HEADS)]
        else:
            states = [s_ref[seq, h] for h in range(N_HEADS)]
        for c, rows in enumerate(row_slices):
            for h in range(N_HEADS):
                kcols = pl.ds(h * DK, DK)
                ocols = pl.ds(h * DV, DV)
                S = states[h]
                if fuse:
                    o = _dot(jnp.concatenate([probs[c, h], qd_ref[rows, kcols]], axis=1),
                             jnp.concatenate([v_at(rows, h), S.astype(BF16)], axis=0))
                else:
                    o = _dot(probs[c, h], v_at(rows, h)) + _dot(qd_ref[rows, kcols], S.astype(BF16))
                states[h] = S * cd_ref[h] + kvs[c, h]
                o = o * lax.rsqrt(jnp.mean(o * o, axis=-1, keepdims=True) + EPS)
                o_ref[rows, ocols] = (o * sg_ref[rows, ocols]).astype(BF16)
        for h in range(N_HEADS):
            s_ref[seq, h] = states[h]


def _retention_specs(a, b, f, decay, cd, rows, row_block):
    def whole(arr):
        return pl.BlockSpec((rows, arr.shape[1]), lambda *g: (row_block(*g), 0))

    specs = [whole(a), whole(b), whole(f), _resident(decay.shape), pl.BlockSpec(memory_space=pltpu.SMEM)]
    return specs, [a, b, f, decay, cd]


def _retention_mix_kernel(a_ref, b_ref, h_ref, dec_ref, cd_ref, wg0_ref, wg1_ref, wga_ref, wgb_ref,
                          wro_ref, wco_ref, wo_ref, d_ref, s_ref, o_ref, sg_ref, *, chunk, n_chunks):
    @pl.when(pl.program_id(1) == 0)
    def _():
        s_ref[...] = jnp.zeros_like(s_ref)

    h = h_ref[...]
    sg_ref[:, :COL_TILE] = _swish(_dot(h, wg0_ref[...]))
    sg_ref[:, COL_TILE:] = _swish(_dot(h, wg1_ref[...]))
    _retention_body(a_ref, b_ref, sg_ref, dec_ref, cd_ref, None, s_ref, o_ref,
                    chunk=chunk, n_chunks=n_chunks, seqs=range(1))
    cbv = b_ref[:, 2 * COL_TILE:]
    sig_a = _sigmoid(_dot(h, wga_ref[...]))
    sig_b = _sigmoid(_dot(h, wgb_ref[...]))
    y_conv = _dot(cbv, wco_ref[...])
    merged = sig_a * _dot(o_ref[...], wro_ref[...]) + sig_b * y_conv
    d_ref[...] = _dot(merged.astype(BF16), wo_ref[...])


def _retention_mix(a, b, h, w_in, decay, cd, wro, wco, wo, *, n_batch, seq_len, chunk, n_chunks):
    T, D = a.shape[0], wo.shape[1]
    rows = n_chunks * chunk
    steps = seq_len // rows

    def row_block(b_, t):
        return b_ * steps + t

    def tile(width, col):
        return pl.BlockSpec((rows, width), lambda b_, t: (row_block(b_, t), col))

    in_specs, args = _retention_specs(a, b, h, decay, cd, rows, row_block)
    in_specs += [pl.BlockSpec((D, COL_TILE), lambda b_, t, tile_=tile_: (0, tile_), pipeline_mode=pl.Buffered(1))
                 for tile_ in GATE_TILES]
    in_specs += [_resident(wro.shape), _resident(wco.shape), _resident(wo.shape)]
    args += [w_in] * len(GATE_TILES) + [wro, wco, wo]
    kernel = functools.partial(_retention_mix_kernel, chunk=chunk, n_chunks=n_chunks)
    return pl.pallas_call(
        kernel,
        grid=(n_batch, steps),
        in_specs=in_specs,
        out_specs=[tile(D, 0), pl.BlockSpec((1, N_HEADS, DK, DV), lambda b_, t: (b_, 0, 0, 0))],
        out_shape=[jax.ShapeDtypeStruct((T, D), F32),
                   jax.ShapeDtypeStruct((n_batch, N_HEADS, DK, DV), F32)],
        scratch_shapes=[pltpu.VMEM((rows, N_HEADS * DV), BF16), pltpu.VMEM((rows, N_HEADS * DV), F32)],
        compiler_params=_params(("parallel", "arbitrary")),
        name="retention_mix",
    )(*args)


N_FFN_INPUTS = 9


def _ffn_kernel(*refs, final_norm, side, mix):
    x_ref, p_ref, gf_ref, wu_ref, wd_ref, gp_ref, wpg_ref, wpp_ref, gl_ref = refs[:N_FFN_INPUTS]
    if mix:
        og_ref, cbv_ref, sa_ref, sb_ref, wro_ref, wco_ref, wo_ref, o_ref = refs[N_FFN_INPUTS:]
        merged = sa_ref[...] * _dot(og_ref[...], wro_ref[...]) + sb_ref[...] * _dot(cbv_ref[...], wco_ref[...])
        x = x_ref[...] + _dot(merged.astype(BF16), wo_ref[...])
    elif side is None:
        o_ref, = refs[N_FFN_INPUTS:]
        x = x_ref[...]
    else:
        ret_refs = refs[N_FFN_INPUTS:N_FFN_INPUTS + N_RET_INPUTS]
        s0_ref, base_ref, o_ref, og_ref, s_ref = refs[N_FFN_INPUTS + N_RET_INPUTS:]
        chunk, n_seq = side
        x = base_ref[...] + x_ref[...]
    hf = _rms(x, gf_ref[...]).astype(BF16)
    n_ff = wu_ref.shape[1] // COL_TILE
    acc = x
    for c in range(n_ff):
        cols = pl.ds(c * COL_TILE, COL_TILE)
        hid = jnp.square(jnp.maximum(_dot(hf, wu_ref[:, cols]), 0.0)).astype(BF16)
        acc = acc + _dot(hid, wd_ref[cols, :])
        if side is not None:
            _retention_body(*ret_refs, s0_ref, s_ref, og_ref, chunk=chunk, n_chunks=1,
                            seqs=range(c * n_seq // n_ff, (c + 1) * n_seq // n_ff))
    gate = _sigmoid(_dot(_rms(acc, gp_ref[...]).astype(BF16), wpg_ref[...]))
    y = acc + gate * _dot(p_ref[...].astype(BF16), wpp_ref[...])
    if final_norm:
        y = _rms(y, gl_ref[...])
    o_ref[...] = y


def _ffn(x2d, p2d, gf, wu, wd, gp, wpg, wpp, gl, *, tm, final_norm, side=None, mix=None):
    T, D = x2d.shape
    steps = T // tm
    in_specs = [pl.BlockSpec((tm, D), lambda i: (i, 0)),
                pl.BlockSpec((tm, p2d.shape[1]), lambda i: (i, 0)),
                _resident(gf.shape), _resident(wu.shape), _resident(wd.shape),
                _resident(gp.shape), _resident(wpg.shape), _resident(wpp.shape),
                _resident(gl.shape)]
    args = [x2d, p2d, gf, wu, wd, gp, wpg, wpp, gl]
    out_specs = [pl.BlockSpec((tm, D), lambda i: (i, 0))]
    out_shape = [jax.ShapeDtypeStruct((T, D), F32)]
    kernel_side = None
    if side is not None:
        base, a, b, f, s0, decay, cd, chunk = side
        n_all = s0.shape[0]
        assert n_all % steps == 0 and a.shape[0] == n_all * chunk
        n_seq = n_all // steps
        ret_specs, ret_args = _retention_specs(a, b, f, decay, cd, n_seq * chunk, lambda i: i)
        state_spec = pl.BlockSpec((n_seq,) + s0.shape[1:], lambda i: (i, 0, 0, 0))
        in_specs += ret_specs + [state_spec, pl.BlockSpec((tm, D), lambda i: (i, 0))]
        args += ret_args + [s0, base]
        out_specs += [pl.BlockSpec((n_seq * chunk, N_HEADS * DV), lambda i: (i, 0)), state_spec]
        out_shape += [jax.ShapeDtypeStruct((a.shape[0], N_HEADS * DV), BF16),
                      jax.ShapeDtypeStruct(s0.shape, F32)]
        kernel_side = (chunk, n_seq)
    if mix is not None:
        assert side is None
        og, b, f, wro, wco, wo = mix
        in_specs += [pl.BlockSpec((tm, N_HEADS * DV), lambda i: (i, 0)),
                     pl.BlockSpec((tm, D), lambda i: (i, 2)), pl.BlockSpec((tm, D), lambda i: (i, 2)),
                     pl.BlockSpec((tm, D), lambda i: (i, 3)),
                     _resident(wro.shape), _resident(wco.shape), _resident(wo.shape)]
        args += [og, b, f, f, wro, wco, wo]
    kernel = functools.partial(_ffn_kernel, final_norm=final_norm, side=kernel_side, mix=mix is not None)
    out = pl.pallas_call(
        kernel,
        grid=(steps,),
        in_specs=in_specs,
        out_specs=out_specs,
        out_shape=out_shape,
        compiler_params=_params(("parallel",)),
        name="ffn_retention" if side is not None else ("mix_ffn" if mix is not None else "ffn"),
    )(*args)
    return out[0] if side is None else out


def _layer(xp, xs, pp, ps, s_ret, s_conv, pos_p, pos_s, w, *, n_prompt, len_prompt, chunk_prompt,
           n_sample, len_sample, final_norm):
    cos_p, sin_p, decay_p, qd_p, kd_p, cd_p = _tables(pos_p, chunk_prompt)
    cos_s, sin_s, decay_s, qd_s, kd_s, cd_s = _tables(pos_s, len_sample)
    ap, bp, hp, conv_p, w_up, w_down = _inproj(
        xp, w["g_mix"], w["w_in"], w["conv_w"], None, cos_p, sin_p, qd_p, kd_p, n_batch=n_prompt,
        seq_len=len_prompt, chunk=chunk_prompt, tm=1024, cast=(w["w_up"], w["w_down"]), gates=False)
    as_, bs, fs, conv_s, w_ret_out, w_conv_out, w_o, w_ple_gate = _inproj(
        xs, w["g_mix"], w["w_in"], w["conv_w"], s_conv, cos_s, sin_s, qd_s, kd_s, n_batch=n_sample,
        seq_len=len_sample, chunk=len_sample, tm=512, cast_chunk_bytes=2 * CAST_CHUNK_BYTES,
        cast=(w["w_ret_out"], w["w_conv_out"], w["w_o"], w["w_ple_gate"]))
    ffn_w = (w["g_ffn"], w_up, w_down, w["g_ple"], w_ple_gate, w["w_ple_proj"], w["g_final"])
    mix_w = (w_ret_out, w_conv_out, w_o)
    dxp, ret_p = _retention_mix(ap, bp, hp, w["w_in"], decay_p, cd_p, *mix_w, n_batch=n_prompt,
                                seq_len=len_prompt, chunk=chunk_prompt, n_chunks=4)
    yp, og_s, ret_s = _ffn(dxp, pp, *ffn_w, tm=512, final_norm=final_norm,
                           side=(xp, as_, bs, fs, s_ret, decay_s, cd_s, len_sample))
    ys = _ffn(xs, ps, *ffn_w, tm=256, final_norm=final_norm, mix=(og_s, bs, fs, *mix_w))
    return yp, ys, ret_p, conv_p, ret_s, conv_s


def kernel(x_prompt, x_sample, state_ret, state_conv, p_prompt, p_sample, g_mix, w_in, w_ret_out,
           conv_w, w_conv_out, w_o, g_ffn, w_up, w_down, g_ple, w_ple_gate, w_ple_proj, g_final):
    depth = w_in.shape[0]
    B, L, D = x_prompt.shape
    Bs, Ls, _ = x_sample.shape
    pos_prompt = jnp.arange(L, dtype=F32)
    pos_sample = PAST_LEN + jnp.arange(Ls, dtype=F32)
    chunk_prompt = min(RET_CHUNK, L)

    hp = x_prompt.reshape(B * L, D)
    hs = x_sample.reshape(Bs * Ls, D)
    rp, cp, rs, cs = [], [], [], []
    for i in range(depth):
        w = {"g_mix": g_mix[i][None], "w_in": w_in[i].astype(BF16),
             "w_ret_out": w_ret_out[i], "conv_w": conv_w[i],
             "w_conv_out": w_conv_out[i], "w_o": w_o[i],
             "g_ffn": g_ffn[i][None], "w_up": w_up[i], "w_down": w_down[i],
             "g_ple": g_ple[i][None], "w_ple_gate": w_ple_gate[i],
             "w_ple_proj": w_ple_proj[i].astype(BF16), "g_final": g_final[None]}
        hp, hs, r_p, c_p, r_s, c_s = _layer(
            hp, hs, p_prompt[i].reshape(B * L, -1), p_sample[i].reshape(Bs * Ls, -1),
            state_ret[i], state_conv[i], pos_prompt, pos_sample, w,
            n_prompt=B, len_prompt=L, chunk_prompt=chunk_prompt, n_sample=Bs, len_sample=Ls,
            final_norm=i == depth - 1)
        rp.append(r_p)
        cp.append(c_p)
        rs.append(r_s)
        cs.append(c_s)
    return (hp.reshape(B, L, D), hs.reshape(Bs, Ls, D), jnp.stack(rp), jnp.stack(cp),
            jnp.stack(rs), jnp.stack(cs))
```
